```python
import jax, jax.numpy as jnp
from jax import lax
import numpy as np

D_MODEL = 1024
BATCH = 8
SEQ = 2048
DEPTH = 2
DEC_BATCH = 16
DEC_SEQ = 2048
PAST_LEN = 128

PLE_DIM = 256
GRID_W = 64
N_BRANCH = 4
BRANCH_W = D_MODEL // 2
RET_HEADS = 4
RET_QK = BRANCH_W // (2 * RET_HEADS)
RET_V = BRANCH_W // RET_HEADS
RET_CHUNK = 128
ROPE_BASE = 10000.0
NAT_HEADS = 8
NAT_HD = BRANCH_W // NAT_HEADS
NAT_WIN_ROWS = 8
NAT_WIN_COLS = 16
LRU_BLOCKS = 8
LRU_BW = BRANCH_W // LRU_BLOCKS
LRU_CONV = 4
LRU_C = 8.0
HGRN_HEADS = 4
HGRN_DK = BRANCH_W // HGRN_HEADS
HGRN_DV = BRANCH_W // HGRN_HEADS
HGRN_CHUNK = 32
EPS = 1e-6

IN_SPLIT_SIZES = (
    RET_HEADS * RET_QK, RET_HEADS * RET_QK, RET_HEADS * RET_V, BRANCH_W,
    BRANCH_W, BRANCH_W, BRANCH_W, BRANCH_W,
    BRANCH_W, BRANCH_W,
    HGRN_HEADS * HGRN_DK, HGRN_HEADS * HGRN_DK, HGRN_HEADS * HGRN_DK,
    HGRN_HEADS * HGRN_DV, BRANCH_W,
)
W_IN = sum(IN_SPLIT_SIZES)
IN_SPLIT_POINTS = tuple(int(c) for c in np.cumsum(IN_SPLIT_SIZES)[:-1])

kernel_name = 'hybrid_bidir_gated_encoder'

F32 = jnp.float32


def rms_norm(x, g):
    xf = x.astype(F32)
    y = xf * lax.rsqrt(jnp.mean(xf * xf, axis=-1, keepdims=True) + EPS)
    return (y * g.astype(F32)).astype(x.dtype)


def head_rms(o):
    return o * lax.rsqrt(jnp.mean(o * o, axis=-1, keepdims=True) + EPS)


def rev_t(a):
    return jnp.flip(a, axis=1)


def rotary(x, pos):
    half = x.shape[-1] // 2
    inv = ROPE_BASE ** (-jnp.arange(half, dtype=F32) / half)
    ang = pos[:, None] * inv[None, :]
    cos = jnp.cos(ang)[:, None, :]
    sin = jnp.sin(ang)[:, None, :]
    x1, x2 = x[..., :half], x[..., half:]
    return jnp.concatenate([x1 * cos - x2 * sin, x1 * sin + x2 * cos], axis=-1)


def retention_scan(q, k, v, log_gamma):
    B, T, H, dk = q.shape
    dv = v.shape[-1]
    C = RET_CHUNK
    N = T // C
    qc = q.reshape(B, N, C, H, dk)
    kc = k.reshape(B, N, C, H, dk)
    vc = v.reshape(B, N, C, H, dv)
    tt = jnp.arange(C, dtype=F32)
    diff = tt[:, None] - tt[None, :]
    intra = jnp.where(diff >= 0, jnp.exp(jnp.maximum(diff, 0.0)[None] * log_gamma[:, None, None]), 0.0)
    scores = jnp.einsum('bnthd,bnshd->bnhts', qc, kc) * intra
    o = jnp.einsum('bnhts,bnshv->bnthv', scores, vc)
    k_tail = jnp.exp((C - 1 - tt)[:, None] * log_gamma[None, :])
    local = jnp.einsum('bnshd,sh,bnshv->bnhdv', kc, k_tail, vc)
    chunk_decay = jnp.exp(C * log_gamma)[:, None, None]

    def step(state, loc):
        return chunk_decay * state + loc, state

    _, prev = lax.scan(step, jnp.zeros((B, H, dk, dv), F32), jnp.moveaxis(local, 1, 0))
    prev = jnp.moveaxis(prev, 0, 1)
    q_head = jnp.exp((tt + 1.0)[:, None] * log_gamma[None, :])
    o = o + jnp.einsum('bnthd,th,bnhdv->bnthv', qc, q_head, prev)
    return o.reshape(B, T, H, dv)


def retention_branch(q, k, v, decay_logit):
    B, T, _ = q.shape
    pos = jnp.arange(T, dtype=F32)
    qh = rotary(q.astype(F32).reshape(B, T, RET_HEADS, RET_QK), pos)
    kh = rotary(k.astype(F32).reshape(B, T, RET_HEADS, RET_QK), pos) * (RET_QK ** -0.5)
    vh = v.astype(F32).reshape(B, T, RET_HEADS, RET_V)
    log_gamma = jax.nn.log_sigmoid(decay_logit.astype(F32))
    o = retention_scan(qh, kh, vh, log_gamma[0]) + rev_t(
        retention_scan(rev_t(qh), rev_t(kh), rev_t(vh), log_gamma[1]))
    return head_rms(o).reshape(B, T, BRANCH_W).astype(q.dtype)


def neighbourhood_attention_branch(q, k, v, rpb):
    B, T, _ = q.shape
    rows = T // GRID_W
    wr = min(NAT_WIN_ROWS, rows)
    n_cb = GRID_W // NAT_WIN_COLS
    kw = 2 * NAT_WIN_COLS
    grid = (B, rows, GRID_W, NAT_HEADS, NAT_HD)
    qg = (q * (NAT_HD ** -0.5)).reshape(B, rows, n_cb, NAT_WIN_COLS, NAT_HEADS, NAT_HD)
    kg = k.reshape(grid)
    vg = v.reshape(grid)
    qcols = jnp.arange(GRID_W).reshape(n_cb, NAT_WIN_COLS)
    kstart = jnp.clip(jnp.arange(n_cb) * NAT_WIN_COLS - NAT_WIN_COLS // 2, 0, GRID_W - kw)
    kcols = kstart[:, None] + jnp.arange(kw)
    wstart = jnp.clip(qcols - NAT_WIN_COLS // 2, 0, GRID_W - NAT_WIN_COLS)
    col_ok = (kcols[:, None, :] >= wstart[..., None]) & (kcols[:, None, :] < wstart[..., None] + NAT_WIN_COLS)
    dc_idx = jnp.clip(kcols[:, None, :] - qcols[..., None] + NAT_WIN_COLS - 1, 0, 2 * NAT_WIN_COLS - 2)
    rpb = rpb.astype(F32)

    def row_block(r):
        rs = jnp.clip(r - wr // 2, 0, rows - wr)
        kb = lax.dynamic_slice_in_dim(kg, rs, wr, axis=1)[:, :, kcols]
        vb = lax.dynamic_slice_in_dim(vg, rs, wr, axis=1)[:, :, kcols]
        qr = qg[:, r]
        s = jnp.einsum('bjqhd,bwjkhd->bhjqwk', qr, kb).astype(F32)
        dr_idx = rs + jnp.arange(wr) - r + NAT_WIN_ROWS - 1
        bias = rpb[:, dr_idx[:, None, None, None], dc_idx[None]]
        bias = bias.transpose(0, 2, 3, 1, 4)
        s = jnp.where(col_ok[:, :, None, :], s + bias, -jnp.inf)
        p = jax.nn.softmax(s, axis=(-2, -1)).astype(vb.dtype)
        return jnp.einsum('bhjqwk,bwjkhd->bjqhd', p, vb)

    o = lax.map(row_block, jnp.arange(rows))
    return jnp.moveaxis(o, 0, 1).reshape(B, T, BRANCH_W).astype(q.dtype)


def rg_lru(xc, wa, ba, wx, bx, lam, reverse):
    B, T, W = xc.shape
    xb = xc.reshape(B, T, LRU_BLOCKS, LRU_BW)
    r = jax.nn.sigmoid(jnp.einsum('btnj,njk->btnk', xb, wa.astype(F32)).reshape(B, T, W) + ba.astype(F32))
    i = jax.nn.sigmoid(jnp.einsum('btnj,njk->btnk', xb, wx.astype(F32)).reshape(B, T, W) + bx.astype(F32))
    log_a = -LRU_C * r * jax.nn.softplus(-lam.astype(F32))
    a = jnp.exp(log_a)
    b = jnp.sqrt(-jnp.expm1(2.0 * log_a)) * (i * xc)

    def combine(e1, e2):
        a1, b1 = e1
        a2, b2 = e2
        return a1 * a2, a2 * b1 + b2

    _, h = lax.associative_scan(combine, (a, b), axis=1, reverse=reverse)
    return h


def rglru_branch(xin, conv_w, conv_b, wa, ba, wx, bx, lam):
    C = xin.shape[-1]
    xc = lax.conv_general_dilated(
        xin.astype(F32), conv_w.astype(F32)[:, None, :], window_strides=(1,),
        padding=[(LRU_CONV // 2, LRU_CONV - 1 - LRU_CONV // 2)],
        dimension_numbers=('NWC', 'WIO', 'NWC'), feature_group_count=C) + conv_b.astype(F32)
    h = rg_lru(xc, wa[0], ba[0], wx[0], bx[0], lam[0], False) + rg_lru(xc, wa[1], ba[1], wx[1], bx[1], lam[1], True)
    return h.astype(xin.dtype)


def gla_chunk(q, k, v, log_f):
    B, T, H, dk = q.shape
    dv = v.shape[-1]
    C = HGRN_CHUNK
    N = T // C

    def chunks(a):
        return jnp.moveaxis(a.reshape(B, N, C, H, a.shape[-1]), 1, 0)

    causal = jnp.tril(jnp.ones((C, C), bool))[None, :, :, None, None]

    def step(S, inp):
        qn, kn, vn, gn = inp
        b = jnp.cumsum(gn, axis=1)
        o_inter = jnp.einsum('bthk,bhkv->bthv', qn * jnp.exp(b), S)
        diff = b[:, :, None] - b[:, None, :]
        dec = jnp.exp(jnp.where(causal, diff, -jnp.inf))
        att = jnp.einsum('btshk,bshk->bhts', qn[:, :, None] * dec, kn)
        o_intra = jnp.einsum('bhts,bshv->bthv', att, vn)
        b_last = b[:, -1]
        S = jnp.exp(b_last)[..., None] * S + jnp.einsum('bshk,bshv->bhkv', kn * jnp.exp(b_last[:, None] - b), vn)
        return S, o_inter + o_intra

    _, o = lax.scan(step, jnp.zeros((B, H, dk, dv), F32), (chunks(q), chunks(k), chunks(v), chunks(log_f)))
    return jnp.moveaxis(o, 0, 1).reshape(B, T, H, dv)


def hgrn2_branch(q, f_fwd, f_bwd, i, lb, gain):
    B, T, _ = q.shape
    kshape = (B, T, HGRN_HEADS, HGRN_DK)
    qh = jax.nn.silu(q.astype(F32)).reshape(kshape)
    vh = i.astype(F32).reshape(B, T, HGRN_HEADS, HGRN_DV)

    def gates(zf, lb_d):
        f = lb_d + (1.0 - lb_d) * jax.nn.sigmoid(zf.astype(F32))
        return (1.0 - f).reshape(kshape), jnp.log(f).reshape(kshape)

    k_f, g_f = gates(f_fwd, lb[0])
    k_b, g_b = gates(f_bwd, lb[1])
    o = gla_chunk(qh, k_f, vh, g_f) + rev_t(gla_chunk(rev_t(qh), rev_t(k_b), rev_t(vh), rev_t(g_b)))
    o = head_rms(o) * gain.astype(F32).reshape(HGRN_HEADS, HGRN_DV)
    return o.reshape(B, T, BRANCH_W).astype(q.dtype)


def hgrn_lower_bounds(logits):
    sm = jax.nn.softmax(logits.astype(F32), axis=0)
    return jnp.cumsum(sm, axis=0) - sm[0:1]


def mixer_layer(x, p_l, g_mix, w_in_l, ret_logit, rpb, conv_w, conv_b, wa, ba, wx, bx, lam, lb, hgrn_gain,
                w_br, w_mg, w_o, g_ple, w_pg, w_pp):
    h = rms_norm(x, g_mix)
    z = h @ w_in_l
    (rq, rk, rv, rg, nq, nk, nv, ng, lx, lg, hq, hff, hfb, hi, hg) = jnp.split(z, IN_SPLIT_POINTS, axis=-1)
    branches = (
        retention_branch(rq, rk, rv, ret_logit) * jax.nn.silu(rg),
        neighbourhood_attention_branch(nq, nk, nv, rpb) * jax.nn.silu(ng),
        rglru_branch(lx, conv_w, conv_b, wa, ba, wx, bx, lam) * jax.nn.silu(lg),
        hgrn2_branch(hq, hff, hfb, hi, lb, hgrn_gain) * jax.nn.silu(hg),
    )
    merged = jax.nn.sigmoid(h @ w_mg[0]) * (branches[0] @ w_br[0])
    for j in range(1, N_BRANCH):
        merged = merged + jax.nn.sigmoid(h @ w_mg[j]) * (branches[j] @ w_br[j])
    x = x + merged @ w_o
    gate = jax.nn.sigmoid(rms_norm(x, g_ple) @ w_pg)
    return x + gate * (p_l.astype(x.dtype) @ w_pp)


def encoder(x, p, lb_all, weights):
    (norm_mix, w_in, ret_decay_logit, nat_rpb, lru_conv_w, lru_conv_b, lru_wa, lru_ba, lru_wx, lru_bx,
     lru_lambda, hgrn_norm, w_branch, w_merge, w_out, ple_norm, w_ple_gate, w_ple_proj, final_norm) = weights
    for l in range(DEPTH):
        x = mixer_layer(x, p[l], norm_mix[l], w_in[l], ret_decay_logit[l], nat_rpb[l], lru_conv_w[l], lru_conv_b[l],
                        lru_wa[l], lru_ba[l], lru_wx[l], lru_bx[l], lru_lambda[l], lb_all[l], hgrn_norm[l],
                        w_branch[l], w_merge[l], w_out[l], ple_norm[l], w_ple_gate[l], w_ple_proj[l])
    return rms_norm(x, final_norm)


def setup_inputs(seed: int = 0) -> dict:
    key = jax.random.key(seed)
    ks = jax.random.split(key, 24)

    def nrm(k, shape, scale):
        return jax.random.normal(k, shape, F32) * scale

    ret_base = jnp.asarray(np.log(2.0 ** (5.0 + np.arange(RET_HEADS)) - 1.0), F32)
    a_c = jax.random.uniform(ks[14], (DEPTH, 2, BRANCH_W), F32, 0.9, 0.999)
    a = a_c ** (1.0 / LRU_C)
    return {
        'x_prompt': nrm(ks[0], (BATCH, SEQ, D_MODEL), 1.0),
        'x_sample': nrm(ks[1], (DEC_BATCH, DEC_SEQ, D_MODEL), 1.0),
        'p_prompt': nrm(ks[2], (DEPTH, BATCH, SEQ, PLE_DIM), 1.0),
        'p_sample': nrm(ks[3], (DEPTH, DEC_BATCH, DEC_SEQ, PLE_DIM), 1.0),
        'norm_mix': 1.0 + nrm(ks[4], (DEPTH, D_MODEL), 0.05),
        'w_in': nrm(ks[5], (DEPTH, D_MODEL, W_IN), D_MODEL ** -0.5),
        'ret_decay_logit': ret_base + nrm(ks[6], (DEPTH, 2, RET_HEADS), 0.05),
        'nat_rpb': nrm(ks[7], (DEPTH, NAT_HEADS, 2 * NAT_WIN_ROWS - 1, 2 * NAT_WIN_COLS - 1), 0.02),
        'lru_conv_w': nrm(ks[8], (DEPTH, LRU_CONV, BRANCH_W), LRU_CONV ** -0.5),
        'lru_conv_b': nrm(ks[9], (DEPTH, BRANCH_W), 0.02),
        'lru_wa': nrm(ks[10], (DEPTH, 2, LRU_BLOCKS, LRU_BW, LRU_BW), LRU_BW ** -0.5),
        'lru_ba': nrm(ks[11], (DEPTH, 2, BRANCH_W), 0.02),
        'lru_wx': nrm(ks[12], (DEPTH, 2, LRU_BLOCKS, LRU_BW, LRU_BW), LRU_BW ** -0.5),
        'lru_bx': nrm(ks[13], (DEPTH, 2, BRANCH_W), 0.02),
        'lru_lambda': jnp.log(a) - jnp.log1p(-a),
        'hgrn_lb_logits': nrm(ks[15], (DEPTH, 2, HGRN_HEADS * HGRN_DK), 0.5),
        'hgrn_norm': 1.0 + nrm(ks[16], (DEPTH, HGRN_HEADS * HGRN_DV), 0.05),
        'w_branch': nrm(ks[17], (DEPTH, N_BRANCH, BRANCH_W, D_MODEL), BRANCH_W ** -0.5),
        'w_merge': nrm(ks[18], (DEPTH, N_BRANCH, D_MODEL, D_MODEL), D_MODEL ** -0.5),
        'w_out': nrm(ks[19], (DEPTH, D_MODEL, D_MODEL), D_MODEL ** -0.5),
        'ple_norm': 1.0 + nrm(ks[20], (DEPTH, D_MODEL), 0.05),
        'w_ple_gate': nrm(ks[21], (DEPTH, D_MODEL, D_MODEL), D_MODEL ** -0.5),
        'w_ple_proj': nrm(ks[22], (DEPTH, PLE_DIM, D_MODEL), PLE_DIM ** -0.5),
        'final_norm': 1.0 + nrm(ks[23], (D_MODEL,), 0.05),
    }


def reference(x_prompt, x_sample, p_prompt, p_sample, norm_mix, w_in, ret_decay_logit, nat_rpb, lru_conv_w,
              lru_conv_b, lru_wa, lru_ba, lru_wx, lru_bx, lru_lambda, hgrn_lb_logits, hgrn_norm, w_branch,
              w_merge, w_out, ple_norm, w_ple_gate, w_ple_proj, final_norm):
    lb_all = hgrn_lower_bounds(hgrn_lb_logits)
    weights = (norm_mix, w_in, ret_decay_logit, nat_rpb, lru_conv_w, lru_conv_b, lru_wa, lru_ba, lru_wx, lru_bx,
               lru_lambda, hgrn_norm, w_branch, w_merge, w_out, ple_norm, w_ple_gate, w_ple_proj, final_norm)
    y_prompt = encoder(x_prompt, p_prompt, lb_all, weights)
    y_sample = encoder(x_sample, p_sample, lb_all, weights)
    return (y_prompt, y_sample)
```

```python
import functools

import numpy as np
import jax
import jax.numpy as jnp
from jax import lax
from jax.experimental import pallas as pl
from jax.experimental.pallas import tpu as pltpu

F32 = jnp.float32
BF16 = jnp.bfloat16

D_MODEL = 1024
PLE_DIM = 256
GRID_W = 64
N_BRANCH = 4
BRANCH_W = 512
RET_HEADS = 4
RET_QK = 64
RET_V = 128
ROPE_BASE = 10000.0
NAT_HEADS = 8
NAT_HD = 64
NAT_WIN_ROWS = 8
NAT_WIN_COLS = 16
LRU_BLOCKS = 8
LRU_BW = 64
LRU_CONV = 4
LRU_C = 8.0
HGRN_HEADS = 4
HGRN_DK = 128
EPS = 1e-6
W_IN = 7168

OFF_RQ, OFF_RK, OFF_RV, OFF_RG = 0, 256, 512, 1024
OFF_NQ, OFF_NK, OFF_NV, OFF_NG = 1536, 2048, 2560, 3072
OFF_LX, OFF_LG = 3584, 4096
OFF_HQ, OFF_HFF, OFF_HFB, OFF_HI, OFF_HG = 4608, 5120, 5632, 6144, 6656

LANES = 128
SUBLANES = 8
VMEM_LIMIT = 56 * 1024 * 1024

CHUNK = 128


def _params(*sem):
    return pltpu.CompilerParams(dimension_semantics=sem, vmem_limit_bytes=VMEM_LIMIT)


def _dot(a, b):
    return jnp.dot(a.astype(BF16), b.astype(BF16), preferred_element_type=F32)


def _dot_nt(a, b):
    return lax.dot_general(a.astype(BF16), b.astype(BF16), (((1,), (1,)), ((), ())),
                           preferred_element_type=F32)


def _dot_tn(a, b):
    return lax.dot_general(a.astype(BF16), b.astype(BF16), (((0,), (0,)), ((), ())),
                           preferred_element_type=F32)


def _rms(x, g):
    return x * lax.rsqrt(jnp.mean(x * x, axis=-1, keepdims=True) + EPS) * g


def _sigmoid(x):
    return 1.0 / (1.0 + jnp.exp(-x))


def _silu(x):
    return x * _sigmoid(x)


def _zspec(t, width, off, grid_pos):
    base = off // width
    if grid_pos == 0:
        return pl.BlockSpec((None, t, width), lambda b, h: (b, 0, base + h))
    return pl.BlockSpec((None, t, width), lambda h, b: (b, 0, base + h))


def _inproj_kernel(x_ref, g_ref, w_ref, z_ref, h_scr):
    @pl.when(pl.program_id(1) == 0)
    def _():
        h_scr[...] = _rms(x_ref[...], g_ref[...]).astype(BF16)

    z_ref[...] = jnp.dot(h_scr[...], w_ref[...], preferred_element_type=F32)


def _inproj(x2d, g, w_bf16):
    m = x2d.shape[0]
    tm, tn = 512, 1792
    return pl.pallas_call(
        _inproj_kernel,
        grid=(m // tm, W_IN // tn),
        in_specs=[
            pl.BlockSpec((tm, D_MODEL), lambda i, j: (i, 0)),
            pl.BlockSpec((1, D_MODEL), lambda i, j: (0, 0)),
            pl.BlockSpec((D_MODEL, tn), lambda i, j: (0, j)),
        ],
        out_specs=pl.BlockSpec((tm, tn), lambda i, j: (i, j)),
        out_shape=jax.ShapeDtypeStruct((m, W_IN), F32),
        scratch_shapes=[pltpu.VMEM((tm, D_MODEL), BF16)],
        compiler_params=_params("parallel", "arbitrary"),
        name="inproj",
    )(x2d, g.reshape(1, D_MODEL), w_bf16)


def _merge_kernel(x_ref, ba_ref, bb_ref, bc_ref, bd_ref, p_ref, gmix_ref, wmg_ref, wbr_ref, wo_ref,
                  gple_ref, wpg_ref, wpp_ref, gfin_ref, out_ref, *, final):
    x = x_ref[...]
    h = _rms(x, gmix_ref[...]).astype(BF16)
    merged = None
    for j, b_ref in enumerate((ba_ref, bb_ref, bc_ref, bd_ref)):
        gate = _sigmoid(jnp.dot(h, wmg_ref[j], preferred_element_type=F32))
        term = gate * jnp.dot(b_ref[...], wbr_ref[j], preferred_element_type=F32)
        merged = term if merged is None else merged + term
    x1 = x + _dot(merged, wo_ref[...])
    gate2 = _sigmoid(_dot(_rms(x1, gple_ref[...]), wpg_ref[...]))
    x2 = x1 + gate2 * _dot(p_ref[...], wpp_ref[...])
    if final:
        x2 = _rms(x2, gfin_ref[...])
    out_ref[...] = x2


def _merge(x2d, branches, p2d, gmix, wmg, wbr, wo, gple, wpg, wpp, gfin, final):
    m = x2d.shape[0]
    tm = 256
    row = lambda i: (i, 0)
    const2 = lambda i: (0, 0)
    const3 = lambda i: (0, 0, 0)
    vec = pl.BlockSpec((1, D_MODEL), const2)
    return pl.pallas_call(
        functools.partial(_merge_kernel, final=final),
        grid=(m // tm,),
        in_specs=[
            pl.BlockSpec((tm, D_MODEL), row),
            pl.BlockSpec((tm, BRANCH_W), row),
            pl.BlockSpec((tm, BRANCH_W), row),
            pl.BlockSpec((tm, BRANCH_W), row),
            pl.BlockSpec((tm, BRANCH_W), row),
            pl.BlockSpec((tm, PLE_DIM), row),
            vec,
            pl.BlockSpec((N_BRANCH, D_MODEL, D_MODEL), const3),
            pl.BlockSpec((N_BRANCH, BRANCH_W, D_MODEL), const3),
            pl.BlockSpec((D_MODEL, D_MODEL), const2),
            vec,
            pl.BlockSpec((D_MODEL, D_MODEL), const2),
            pl.BlockSpec((PLE_DIM, D_MODEL), const2),
            vec,
        ],
        out_specs=pl.BlockSpec((tm, D_MODEL), row),
        out_shape=jax.ShapeDtypeStruct((m, D_MODEL), F32),
        compiler_params=_params("parallel"),
        name="merge",
    )(x2d, *branches, p2d, gmix.reshape(1, -1), wmg, wbr, wo, gple.reshape(1, -1), wpg, wpp,
      gfin.reshape(1, -1))


def _retention_kernel(q_ref, k_ref, v_ref, g_ref, cos_ref, sin_ref, lgq_ref, lgv_ref, out_ref,
                      qr_scr, kr_scr, sb_scr, sf_scr):
    t = q_ref.shape[0]
    c = CHUNK
    n_chunks = t // c
    hd = RET_QK

    lane = lax.broadcasted_iota(jnp.int32, (1, 2 * hd), 1)
    first_half = (lane % hd) < (hd // 2)
    head0_q = lane < hd
    lane_v = lax.broadcasted_iota(jnp.int32, (1, 2 * RET_V), 1)
    head0_v = lane_v < RET_V

    def rotary(x, cos, sin):
        partner = jnp.where(first_half, pltpu.roll(x, 2 * hd - hd // 2, 1), pltpu.roll(x, hd // 2, 1))
        return x * cos + partner * sin

    def rot_body(n, carry):
        sl = pl.ds(pl.multiple_of(n * c, c), c)
        cos = cos_ref[sl, :]
        sin = sin_ref[sl, :]
        qr_scr[sl, :] = rotary(q_ref[sl, :], cos, sin)
        kr_scr[sl, :] = rotary(k_ref[sl, :], cos, sin) * (hd ** -0.5)
        return carry

    lax.fori_loop(0, n_chunks, rot_body, 0)

    lg_f = -jnp.log(1.0 + jnp.exp(-lgq_ref[0:1, :]))
    lg_b = -jnp.log(1.0 + jnp.exp(-lgq_ref[1:2, :]))
    lgv_f = -jnp.log(1.0 + jnp.exp(-lgv_ref[0:1, :]))
    lgv_b = -jnp.log(1.0 + jnp.exp(-lgv_ref[1:2, :]))
    tcol = lax.broadcasted_iota(jnp.int32, (c, 1), 0).astype(F32)
    head_f = jnp.exp((tcol + 1.0) * lg_f)
    head_b = jnp.exp((c - tcol) * lg_b)
    tail_f = jnp.exp((c - 1.0 - tcol) * lg_f)
    tail_b = jnp.exp(tcol * lg_b)
    dec_f = jnp.exp(c * lgv_f)
    dec_b = jnp.exp(c * lgv_b)
    rowk = lax.broadcasted_iota(jnp.int32, (2 * hd, 2 * RET_V), 0)
    colv = lax.broadcasted_iota(jnp.int32, (2 * hd, 2 * RET_V), 1)
    blockdiag = (rowk // hd) == (colv // RET_V)

    diff = (lax.broadcasted_iota(jnp.int32, (c, c), 0) - lax.broadcasted_iota(jnp.int32, (c, c), 1)).astype(F32)

    def decay_mask(lf, lb):
        fwd = jnp.exp(jnp.maximum(diff, 0.0) * lf)
        bwd = jnp.exp(jnp.maximum(-diff, 0.0) * lb)
        return jnp.where(diff > 0, fwd, jnp.where(diff < 0, bwd, 2.0))

    dmask = jnp.concatenate([decay_mask(lgv_f[:, :c], lgv_b[:, :c]),
                             decay_mask(lgv_f[:, RET_V:RET_V + c], lgv_b[:, RET_V:RET_V + c])], axis=1)

    sf_scr[...] = jnp.zeros_like(sf_scr)

    def bwd_body(i, carry):
        n = n_chunks - 1 - i
        sl = pl.ds(pl.multiple_of(n * c, c), c)
        sb_scr[n] = sf_scr[...]
        loc = _dot_tn(kr_scr[sl, :] * tail_b, v_ref[sl, :])
        sf_scr[...] = sf_scr[...] * dec_b + jnp.where(blockdiag, loc, 0.0)
        return carry

    lax.fori_loop(0, n_chunks, bwd_body, 0)

    sf_scr[...] = jnp.zeros_like(sf_scr)

    def fwd_body(n, carry):
        sl = pl.ds(pl.multiple_of(n * c, c), c)
        qn = qr_scr[sl, :]
        kn = kr_scr[sl, :]
        vn = v_ref[sl, :]
        kstack = jnp.concatenate([jnp.where(head0_q, kn, 0.0), jnp.where(head0_q, 0.0, kn)], axis=0)
        scores = _dot_nt(qn, kstack) * dmask
        vstack = jnp.concatenate([jnp.where(head0_v, vn, 0.0), jnp.where(head0_v, 0.0, vn)], axis=0)
        lhs = jnp.concatenate([scores, qn * head_f, qn * head_b], axis=1)
        rhs = jnp.concatenate([vstack, sf_scr[...], sb_scr[n]], axis=0)
        o = _dot(lhs, rhs)
        loc = _dot_tn(kn * tail_f, vn)
        sf_scr[...] = sf_scr[...] * dec_f + jnp.where(blockdiag, loc, 0.0)
        o0 = o[:, :RET_V]
        o1 = o[:, RET_V:]
        o0 = o0 * lax.rsqrt(jnp.mean(o0 * o0, axis=-1, keepdims=True) + EPS)
        o1 = o1 * lax.rsqrt(jnp.mean(o1 * o1, axis=-1, keepdims=True) + EPS)
        on = jnp.concatenate([o0, o1], axis=1)
        out_ref[sl, :] = (on * _silu(g_ref[sl, :])).astype(out_ref.dtype)
        return carry

    lax.fori_loop(0, n_chunks, fwd_body, 0)


def _retention(z3, cos_tab, sin_tab, lgq, lgv):
    b, t, _ = z3.shape
    n_chunks = t // CHUNK
    return pl.pallas_call(
        _retention_kernel,
        grid=(b, RET_HEADS // 2),
        in_specs=[
            _zspec(t, 2 * RET_QK, OFF_RQ, 0),
            _zspec(t, 2 * RET_QK, OFF_RK, 0),
            _zspec(t, 2 * RET_V, OFF_RV, 0),
            _zspec(t, 2 * RET_V, OFF_RG, 0),
            pl.BlockSpec((t, 2 * RET_QK), lambda i, h: (0, 0)),
            pl.BlockSpec((t, 2 * RET_QK), lambda i, h: (0, 0)),
            pl.BlockSpec((2, 2 * RET_QK), lambda i, h: (0, h)),
            pl.BlockSpec((2, 2 * RET_V), lambda i, h: (0, h)),
        ],
        out_specs=pl.BlockSpec((None, t, 2 * RET_V), lambda i, h: (i, 0, h)),
        out_shape=jax.ShapeDtypeStruct((b, t, BRANCH_W), BF16),
        scratch_shapes=[
            pltpu.VMEM((t, 2 * RET_QK), F32),
            pltpu.VMEM((t, 2 * RET_QK), F32),
            pltpu.VMEM((n_chunks, 2 * RET_QK, 2 * RET_V), F32),
            pltpu.VMEM((2 * RET_QK, 2 * RET_V), F32),
        ],
        compiler_params=_params("parallel", "parallel"),
        name="retention",
    )(z3, z3, z3, z3, cos_tab, sin_tab, lgq, lgv)


def _rotary_tables(t):
    half = RET_QK // 2
    inv = ROPE_BASE ** (-jnp.arange(half, dtype=F32) / half)
    ang = jnp.arange(t, dtype=F32)[:, None] * inv[None, :]
    cos = jnp.cos(ang)
    sin = jnp.sin(ang)
    cos_tab = jnp.tile(jnp.concatenate([cos, cos], axis=1), (1, 2))
    sin_tab = jnp.tile(jnp.concatenate([-sin, sin], axis=1), (1, 2))
    return cos_tab, sin_tab


NAT_QROWS = 8
NAT_KROWS = 16
NAT_KCOLS = 2 * NAT_WIN_COLS


def _nat_kstart(j):
    return int(np.clip(j * NAT_WIN_COLS - NAT_WIN_COLS // 2, 0, GRID_W - NAT_KCOLS))


def _nat_key_row_base(g, rows):
    return int(np.clip(g * NAT_QROWS - NAT_WIN_ROWS // 2, 0, rows - NAT_KROWS))


def _nat_bias_tables(rpb, rows):
    n_g = rows // NAT_QROWS
    n_cb = GRID_W // NAT_WIN_COLS
    dr = np.zeros((n_g, NAT_QROWS, NAT_KROWS), np.int32)
    row_ok = np.zeros((n_g, NAT_QROWS, NAT_KROWS), bool)
    for g in range(n_g):
        kb = _nat_key_row_base(g, rows)
        for rr in range(NAT_QROWS):
            r = g * NAT_QROWS + rr
            rs = int(np.clip(r - NAT_WIN_ROWS // 2, 0, rows - NAT_WIN_ROWS))
            for kr in range(NAT_KROWS):
                ka = kb + kr
                row_ok[g, rr, kr] = rs <= ka < rs + NAT_WIN_ROWS
                dr[g, rr, kr] = int(np.clip(ka - r + NAT_WIN_ROWS - 1, 0, 2 * NAT_WIN_ROWS - 2))
    dc = np.zeros((n_cb, NAT_WIN_COLS, NAT_KCOLS), np.int32)
    col_ok = np.zeros((n_cb, NAT_WIN_COLS, NAT_KCOLS), bool)
    for j in range(n_cb):
        ks = _nat_kstart(j)
        for qq in range(NAT_WIN_COLS):
            qc = j * NAT_WIN_COLS + qq
            ws = int(np.clip(qc - NAT_WIN_COLS // 2, 0, GRID_W - NAT_WIN_COLS))
            for kc in range(NAT_KCOLS):
                ka = ks + kc
                col_ok[j, qq, kc] = ws <= ka < ws + NAT_WIN_COLS
                dc[j, qq, kc] = int(np.clip(ka - qc + NAT_WIN_COLS - 1, 0, 2 * NAT_WIN_COLS - 2))
    dr6 = np.broadcast_to(dr[:, None, :, None, :, None], (n_g, n_cb, NAT_QROWS, NAT_WIN_COLS, NAT_KROWS, NAT_KCOLS))
    dc6 = np.broadcast_to(dc[None, :, None, :, None, :], dr6.shape)
    ok6 = row_ok[:, None, :, None, :, None] & col_ok[None, :, None, :, None, :]
    bias = rpb.astype(F32)[:, dr6, dc6]
    bias = jnp.where(ok6[None], bias, -jnp.inf)
    nq = NAT_QROWS * NAT_WIN_COLS
    nk = NAT_KROWS * NAT_KCOLS
    bias = bias.reshape(NAT_HEADS // 2, 2, n_g, n_cb, nq, nk)
    return bias.transpose(0, 2, 3, 1, 4, 5)


def _nat_kernel(q_ref, k_ref, v_ref, g_ref, bias_ref, out_ref):
    t = q_ref.shape[0]
    rows = t // GRID_W
    n_g = rows // NAT_QROWS
    n_cb = GRID_W // NAT_WIN_COLS
    nq = NAT_QROWS * NAT_WIN_COLS
    lane = lax.broadcasted_iota(jnp.int32, (1, 2 * NAT_HD), 1)
    head0 = lane < NAT_HD

    def group_body(g, carry):
        kb = jnp.clip(g * NAT_QROWS - NAT_WIN_ROWS // 2, 0, rows - NAT_KROWS)
        for j in range(n_cb):
            ks = _nat_kstart(j)

            def qrows(ref, rr):
                start = pl.multiple_of((g * NAT_QROWS + rr) * GRID_W + j * NAT_WIN_COLS, NAT_WIN_COLS)
                return ref[pl.ds(start, NAT_WIN_COLS), :]

            def krows(ref, kr):
                start = pl.multiple_of((kb + kr) * GRID_W + ks, SUBLANES)
                return ref[pl.ds(start, NAT_KCOLS), :]

            qs = jnp.concatenate([qrows(q_ref, rr) for rr in range(NAT_QROWS)], axis=0) * (NAT_HD ** -0.5)
            q2 = jnp.concatenate([jnp.where(head0, qs, 0.0), jnp.where(head0, 0.0, qs)], axis=0)
            kblk = jnp.concatenate([krows(k_ref, kr) for kr in range(NAT_KROWS)], axis=0)
            vblk = jnp.concatenate([krows(v_ref, kr) for kr in range(NAT_KROWS)], axis=0)
            s = _dot_nt(q2, kblk)
            s = s + jnp.concatenate([bias_ref[g, j, 0], bias_ref[g, j, 1]], axis=0)
            m = jnp.max(s, axis=-1, keepdims=True)
            p = jnp.exp(s - m)
            denom = jnp.sum(p, axis=-1, keepdims=True)
            o2 = _dot(p, vblk) / denom
            o = jnp.where(head0, o2[:nq], o2[nq:])
            gs = jnp.concatenate([qrows(g_ref, rr) for rr in range(NAT_QROWS)], axis=0)
            res = (o * _silu(gs)).astype(out_ref.dtype)
            for rr in range(NAT_QROWS):
                start = pl.multiple_of((g * NAT_QROWS + rr) * GRID_W + j * NAT_WIN_COLS, NAT_WIN_COLS)
                out_ref[pl.ds(start, NAT_WIN_COLS), :] = res[rr * NAT_WIN_COLS:(rr + 1) * NAT_WIN_COLS]
        return carry

    lax.fori_loop(0, n_g, group_body, 0)


def _nat(z3, bias_tab):
    b, t, _ = z3.shape
    n_hp = NAT_HEADS // 2
    w = 2 * NAT_HD
    return pl.pallas_call(
        _nat_kernel,
        grid=(n_hp, b),
        in_specs=[
            _zspec(t, w, OFF_NQ, 1),
            _zspec(t, w, OFF_NK, 1),
            _zspec(t, w, OFF_NV, 1),
            _zspec(t, w, OFF_NG, 1),
            pl.BlockSpec((None,) + bias_tab.shape[1:], lambda h, i: (h, 0, 0, 0, 0, 0)),
        ],
        out_specs=pl.BlockSpec((None, t, w), lambda h, i: (i, 0, h)),
        out_shape=jax.ShapeDtypeStruct((b, t, BRANCH_W), BF16),
        compiler_params=_params("parallel", "parallel"),
        name="nat",
    )(z3, z3, z3, z3, bias_tab)


LRU_TC = 256


def _rglru_kernel(x_ref, g_ref, cw_ref, cb_ref, wg_ref, bg_ref, lam_ref, out_ref,
                  xp_scr, af_scr, bf_scr, ab_scr, bb_scr):
    t = x_ref.shape[0]
    tc = LRU_TC
    pad = SUBLANES
    w = LANES

    xp_scr[0:pad, :] = jnp.zeros((pad, w), F32)
    xp_scr[pad + t:pad + t + pad, :] = jnp.zeros((pad, w), F32)
    xp_scr[pad:pad + t, :] = x_ref[...]

    lam = lam_ref[...]
    neg = -lam
    softplus = jnp.maximum(neg, 0.0) + jnp.log(1.0 + jnp.exp(-jnp.abs(neg)))
    cw = cw_ref[...]
    cb = cb_ref[...]
    bg = bg_ref[...]

    def gate_body(n, carry):
        t0 = pl.multiple_of(n * tc, tc)
        xx = xp_scr[pl.ds(t0, tc + 2 * pad), :]
        total = tc + 2 * pad
        xc = cb
        for j in range(LRU_CONV):
            shift = LRU_CONV // 2 - j
            xs = xx if shift == 0 else pltpu.roll(xx, shift % total, 0)
            xc = xc + xs[pad:pad + tc] * cw[j:j + 1, :]
        gates = _dot(xc, wg_ref[...]) + bg
        sl = pl.ds(t0, tc)
        for d, (a_scr, b_scr) in enumerate(((af_scr, bf_scr), (ab_scr, bb_scr))):
            r = _sigmoid(gates[:, (2 * d) * w:(2 * d + 1) * w])
            i = _sigmoid(gates[:, (2 * d + 1) * w:(2 * d + 2) * w])
            log_a = -LRU_C * r * softplus[d:d + 1, :]
            a = jnp.exp(log_a)
            a_scr[sl, :] = a
            b_scr[sl, :] = jnp.sqrt(1.0 - a * a) * (i * xc)
        return carry

    lax.fori_loop(0, t // tc, gate_body, 0)

    n_tiles = t // SUBLANES
    row = lax.broadcasted_iota(jnp.int32, (SUBLANES, w), 0)

    def scan_body(i, carry):
        hf_prev, hb_next = carry
        sl = pl.ds(pl.multiple_of(i * SUBLANES, SUBLANES), SUBLANES)
        a = af_scr[sl, :]
        b = bf_scr[sl, :]
        for s in (1, 2, 4):
            keep = row >= s
            b = jnp.where(keep, a * pltpu.roll(b, s, 0) + b, b)
            a = jnp.where(keep, a * pltpu.roll(a, s, 0), a)
        hf = a * hf_prev + b
        af_scr[sl, :] = hf
        hf_prev = jnp.broadcast_to(hf[SUBLANES - 1:SUBLANES, :], (SUBLANES, w))
        slb = pl.ds(pl.multiple_of((n_tiles - 1 - i) * SUBLANES, SUBLANES), SUBLANES)
        a = ab_scr[slb, :]
        b = bb_scr[slb, :]
        for s in (1, 2, 4):
            keep = row < SUBLANES - s
            b = jnp.where(keep, a * pltpu.roll(b, SUBLANES - s, 0) + b, b)
            a = jnp.where(keep, a * pltpu.roll(a, SUBLANES - s, 0), a)
        hb = a * hb_next + b
        ab_scr[slb, :] = hb
        hb_next = jnp.broadcast_to(hb[0:1, :], (SUBLANES, w))
        return hf_prev, hb_next

    zero = jnp.zeros((SUBLANES, w), F32)
    lax.fori_loop(0, n_tiles, scan_body, (zero, zero))

    def out_body(n, carry):
        sl = pl.ds(pl.multiple_of(n * tc, tc), tc)
        out_ref[sl, :] = ((af_scr[sl, :] + ab_scr[sl, :]) * _silu(g_ref[sl, :])).astype(out_ref.dtype)
        return carry

    lax.fori_loop(0, t // tc, out_body, 0)


def _rglru(z3, conv_w, conv_b, wg, bg, lam):
    b, t, _ = z3.shape
    n_cb = BRANCH_W // LANES
    return pl.pallas_call(
        _rglru_kernel,
        grid=(b, n_cb),
        in_specs=[
            _zspec(t, LANES, OFF_LX, 0),
            _zspec(t, LANES, OFF_LG, 0),
            pl.BlockSpec((LRU_CONV, LANES), lambda i, h: (0, h)),
            pl.BlockSpec((1, LANES), lambda i, h: (0, h)),
            pl.BlockSpec((None, LANES, 4 * LANES), lambda i, h: (h, 0, 0)),
            pl.BlockSpec((None, 1, 4 * LANES), lambda i, h: (h, 0, 0)),
            pl.BlockSpec((2, LANES), lambda i, h: (0, h)),
        ],
        out_specs=pl.BlockSpec((None, t, LANES), lambda i, h: (i, 0, h)),
        out_shape=jax.ShapeDtypeStruct((b, t, BRANCH_W), BF16),
        scratch_shapes=[pltpu.VMEM((t + 2 * SUBLANES, LANES), F32)] + [pltpu.VMEM((t, LANES), F32)] * 4,
        compiler_params=_params("parallel", "parallel"),
        name="rglru",
    )(z3, z3, conv_w, conv_b.reshape(1, BRANCH_W), wg, bg, lam)


def _rglru_gate_weights(wa, ba, wx, bx):
    n_cb = BRANCH_W // LANES
    per = LANES // LRU_BW

    def blockdiag(w):
        w = w.reshape(n_cb, per, LRU_BW, LRU_BW)
        eye = jnp.eye(per, dtype=w.dtype)
        return jnp.einsum('cpjk,pq->cpjqk', w, eye).reshape(n_cb, LANES, LANES)

    wg = jnp.concatenate([blockdiag(wa[0]), blockdiag(wx[0]), blockdiag(wa[1]), blockdiag(wx[1])], axis=-1)
    bg = jnp.concatenate([ba[0].reshape(n_cb, 1, LANES), bx[0].reshape(n_cb, 1, LANES),
                          ba[1].reshape(n_cb, 1, LANES), bx[1].reshape(n_cb, 1, LANES)], axis=-1)
    return wg.astype(BF16), bg.astype(F32)


def _prefix_sum_rows(x):
    c = x.shape[0]
    row = lax.broadcasted_iota(jnp.int32, x.shape, 0)
    s = 1
    while s < c:
        x = x + jnp.where(row >= s, pltpu.roll(x, s, 0), 0.0)
        s *= 2
    return x


def _block_row(x, size, pos):
    c, w = x.shape
    if size >= SUBLANES:
        x3 = x.reshape(c // size, size, w)
        return jnp.broadcast_to(x3[:, pos:pos + 1, :], x3.shape).reshape(c, w)
    rpos = lax.broadcasted_iota(jnp.int32, x.shape, 0) % size
    out = x
    for p in range(size):
        if p != pos:
            out = jnp.where(rpos == p, pltpu.roll(x, (p - pos) % c, 0), out)
    return out


def _hgrn_gates(zf, lb):
    f = lb + (1.0 - lb) * _sigmoid(zf)
    return 1.0 - f, jnp.log(f)


def _hgrn_kernel(q_ref, ff_ref, fb_ref, v_ref, g_ref, lbl_ref, gain_ref, out_ref,
                 kb_scr, cs_scr, sb_scr, st_scr, *, layer):
    t = q_ref.shape[0]
    c = CHUNK
    n_chunks = t // c
    w = HGRN_DK

    logits = lbl_ref[...]
    mx = jnp.max(logits, axis=0)
    ex = jnp.exp(logits - mx[None])
    tot = jnp.sum(ex, axis=0)
    lb = jnp.zeros_like(tot)
    for i in range(1, layer + 1):
        lb = lb + ex[i] / tot
    lb_f = lb[0:1, :]
    lb_b = lb[1:2, :]

    rowi = lax.broadcasted_iota(jnp.int32, (c, c), 0)
    coli = lax.broadcasted_iota(jnp.int32, (c, c), 1)
    trow = lax.broadcasted_iota(jnp.int32, (c, 1), 0)

    st_scr[...] = jnp.zeros_like(st_scr)

    def bwd_body(i, carry):
        n = n_chunks - 1 - i
        sl = pl.ds(pl.multiple_of(n * c, c), c)
        kb, gb = _hgrn_gates(fb_ref[sl, :], lb_b)
        pre = _prefix_sum_rows(gb)
        total = pre[c - 1:c, :]
        cs = total - pre + gb
        kb_scr[sl, :] = kb
        cs_scr[sl, :] = cs
        sb_scr[n] = st_scr[...]
        st_scr[...] = st_scr[...] * jnp.exp(total) + _dot_tn(v_ref[sl, :], kb * jnp.exp(total - cs))
        return carry

    lax.fori_loop(0, n_chunks, bwd_body, 0)

    st_scr[...] = jnp.zeros_like(st_scr)

    def fwd_body(n, carry):
        sl = pl.ds(pl.multiple_of(n * c, c), c)
        q = _silu(q_ref[sl, :])
        v = v_ref[sl, :]
        kf, gf = _hgrn_gates(ff_ref[sl, :], lb_f)
        bs = _prefix_sum_rows(gf)
        kb = kb_scr[sl, :]
        cs = cs_scr[sl, :]

        att = jnp.zeros((c, c), F32)
        size = c
        while size >= 2:
            half = size // 2
            upper = (trow % size) >= half
            e_f = jnp.exp(-jnp.abs(bs - _block_row(bs, size, half - 1)))
            e_b = jnp.exp(-jnp.abs(cs - _block_row(cs, size, half)))
            qf = q * e_f
            qb = q * e_b
            x = jnp.concatenate([jnp.where(upper, qf, 0.0), jnp.where(upper, 0.0, qb)], axis=1)
            y = jnp.concatenate([jnp.where(upper, 0.0, kf * e_f), jnp.where(upper, kb * e_b, 0.0)], axis=1)
            prod = _dot_nt(x, y)
            if size == c:
                att = att + prod
            else:
                att = att + jnp.where((rowi // size) == (coli // size), prod, 0.0)
            size = half

        diag = jnp.sum(q * (kf + kb), axis=-1, keepdims=True)
        o = _dot(att, v) + diag * v
        inter = jnp.concatenate([q * jnp.exp(bs), q * jnp.exp(cs)], axis=1)
        states = jnp.concatenate([st_scr[...], sb_scr[n]], axis=1)
        o = o + _dot_nt(inter, states)
        last = bs[c - 1:c, :]
        st_scr[...] = st_scr[...] * jnp.exp(last) + _dot_tn(v, kf * jnp.exp(last - bs))
        o = o * lax.rsqrt(jnp.mean(o * o, axis=-1, keepdims=True) + EPS) * gain_ref[...]
        out_ref[sl, :] = (o * _silu(g_ref[sl, :])).astype(out_ref.dtype)
        return carry

    lax.fori_loop(0, n_chunks, fwd_body, 0)


def _hgrn(z3, lb_logits, gain, layer):
    b, t, _ = z3.shape
    depth = lb_logits.shape[0]
    w = HGRN_DK
    n_chunks = t // CHUNK
    return pl.pallas_call(
        functools.partial(_hgrn_kernel, layer=layer),
        grid=(b, HGRN_HEADS),
        in_specs=[
            _zspec(t, w, OFF_HQ, 0),
            _zspec(t, w, OFF_HFF, 0),
            _zspec(t, w, OFF_HFB, 0),
            _zspec(t, w, OFF_HI, 0),
            _zspec(t, w, OFF_HG, 0),
            pl.BlockSpec((depth, 2, w), lambda i, h: (0, 0, h)),
            pl.BlockSpec((1, w), lambda i, h: (0, h)),
        ],
        out_specs=pl.BlockSpec((None, t, w), lambda i, h: (i, 0, h)),
        out_shape=jax.ShapeDtypeStruct((b, t, BRANCH_W), BF16),
        scratch_shapes=[
            pltpu.VMEM((t, w), F32),
            pltpu.VMEM((t, w), F32),
            pltpu.VMEM((n_chunks, w, w), F32),
            pltpu.VMEM((w, w), F32),
        ],
        compiler_params=_params("parallel", "parallel"),
        name="hgrn",
    )(z3, z3, z3, z3, z3, lb_logits, gain.reshape(1, -1))


def _encoder(x, p, w):
    b, t, _ = x.shape
    depth = w['w_in'].shape[0]
    x2d = x.reshape(b * t, D_MODEL)
    for l in range(depth):
        z3 = _inproj(x2d, w['norm_mix'][l], w['w_in'][l]).reshape(b, t, W_IN)
        br_a = _retention(z3, w['cos'], w['sin'], w['lgq'][l], w['lgv'][l])
        br_b = _nat(z3, w['nat_bias'][l])
        br_c = _rglru(z3, w['conv_w'][l], w['conv_b'][l], w['lru_wg'][l], w['lru_bg'][l], w['lam'][l])
        br_d = _hgrn(z3, w['lb_logits'], w['hgrn_gain'][l], l)
        branches = [a.reshape(b * t, BRANCH_W) for a in (br_a, br_b, br_c, br_d)]
        x2d = _merge(x2d, branches, p[l].reshape(b * t, PLE_DIM), w['norm_mix'][l], w['w_merge'][l],
                     w['w_branch'][l], w['w_out'][l], w['ple_norm'][l], w['w_ple_gate'][l],
                     w['w_ple_proj'][l], w['final_norm'], l == depth - 1)
    return x2d.reshape(b, t, D_MODEL)


def kernel(x_prompt, x_sample, p_prompt, p_sample, norm_mix, w_in, ret_decay_logit, nat_rpb, lru_conv_w,
           lru_conv_b, lru_wa, lru_ba, lru_wx, lru_bx, lru_lambda, hgrn_lb_logits, hgrn_norm, w_branch,
           w_merge, w_out, ple_norm, w_ple_gate, w_ple_proj, final_norm):
    depth = w_in.shape[0]
    t = x_prompt.shape[1]
    rows = t // GRID_W
    cos_tab, sin_tab = _rotary_tables(t)
    gate_w = [_rglru_gate_weights(lru_wa[l], lru_ba[l], lru_wx[l], lru_bx[l]) for l in range(depth)]
    weights = {
        'norm_mix': norm_mix,
        'w_in': w_in.astype(BF16),
        'cos': cos_tab,
        'sin': sin_tab,
        'lgq': jnp.repeat(ret_decay_logit.astype(F32), RET_QK, axis=-1),
        'lgv': jnp.repeat(ret_decay_logit.astype(F32), RET_V, axis=-1),
        'nat_bias': [_nat_bias_tables(nat_rpb[l], rows) for l in range(depth)],
        'conv_w': lru_conv_w,
        'conv_b': lru_conv_b,
        'lru_wg': [g[0] for g in gate_w],
        'lru_bg': [g[1] for g in gate_w],
        'lam': lru_lambda,
        'lb_logits': hgrn_lb_logits,
        'hgrn_gain': hgrn_norm,
        'w_branch': w_branch.astype(BF16),
        'w_merge': w_merge.astype(BF16),
        'w_out': w_out.astype(BF16),
        'ple_norm': ple_norm,
        'w_ple_gate': w_ple_gate.astype(BF16),
        'w_ple_proj': w_ple_proj.astype(BF16),
        'final_norm': final_norm,
    }
    y_prompt = _encoder(x_prompt, p_prompt, weights)
    y_sample = _encoder(x_sample, p_sample, weights)
    return (y_prompt, y_sample)
```

```python
import functools

import numpy as np
import jax
import jax.numpy as jnp
from jax import lax
from jax.experimental import pallas as pl
from jax.experimental.pallas import tpu as pltpu

F32 = jnp.float32
BF16 = jnp.bfloat16

D_MODEL = 1024
PLE_DIM = 256
GRID_W = 64
N_BRANCH = 4
BRANCH_W = 512
RET_HEADS = 4
RET_QK = 64
RET_V = 128
ROPE_BASE = 10000.0
NAT_HEADS = 8
NAT_HD = 64
NAT_WIN_ROWS = 8
NAT_WIN_COLS = 16
LRU_BLOCKS = 8
LRU_BW = 64
LRU_CONV = 4
LRU_C = 8.0
HGRN_HEADS = 4
HGRN_DK = 128
EPS = 1e-6
W_IN = 7168

OFF_RQ, OFF_RK, OFF_RV, OFF_RG = 0, 256, 512, 1024
OFF_NQ, OFF_NK, OFF_NV, OFF_NG = 1536, 2048, 2560, 3072
OFF_LX, OFF_LG = 3584, 4096
OFF_HQ, OFF_HFF, OFF_HFB, OFF_HI, OFF_HG = 4608, 5120, 5632, 6144, 6656

LANES = 128
SUBLANES = 8
VMEM_LIMIT = 56 * 1024 * 1024

CHUNK = 128


def _params(*sem):
    return pltpu.CompilerParams(dimension_semantics=sem, vmem_limit_bytes=VMEM_LIMIT)


def _dot(a, b):
    return jnp.dot(a.astype(BF16), b.astype(BF16), preferred_element_type=F32)


def _dot_nt(a, b):
    return lax.dot_general(a.astype(BF16), b.astype(BF16), (((1,), (1,)), ((), ())),
                           preferred_element_type=F32)


def _dot_tn(a, b):
    return lax.dot_general(a.astype(BF16), b.astype(BF16), (((0,), (0,)), ((), ())),
                           preferred_element_type=F32)


def _rms(x, g):
    return x * lax.rsqrt(jnp.mean(x * x, axis=-1, keepdims=True) + EPS) * g


def _sigmoid(x):
    return jax.nn.sigmoid(x)


def _silu(x):
    return x * jax.nn.sigmoid(x)


def _zspec(t, width, off, grid_pos):
    base = off // width
    if grid_pos == 0:
        return pl.BlockSpec((None, t, width), lambda b, h: (b, 0, base + h))
    return pl.BlockSpec((None, t, width), lambda h, b: (b, 0, base + h))


def _inproj_kernel(x_ref, g_ref, w_ref, z_ref, h_scr):
    @pl.when(pl.program_id(1) == 0)
    def _():
        h_scr[...] = _rms(x_ref[...], g_ref[...]).astype(BF16)

    z_ref[...] = jnp.dot(h_scr[...], w_ref[...], preferred_element_type=F32)


def _inproj(x2d, g, w_bf16):
    m = x2d.shape[0]
    tm, tn = 1024, 1792
    return pl.pallas_call(
        _inproj_kernel,
        grid=(m // tm, W_IN // tn),
        in_specs=[
            pl.BlockSpec((tm, D_MODEL), lambda i, j: (i, 0)),
            pl.BlockSpec((1, D_MODEL), lambda i, j: (0, 0)),
            pl.BlockSpec((D_MODEL, tn), lambda i, j: (0, j)),
        ],
        out_specs=pl.BlockSpec((tm, tn), lambda i, j: (i, j)),
        out_shape=jax.ShapeDtypeStruct((m, W_IN), F32),
        scratch_shapes=[pltpu.VMEM((tm, D_MODEL), BF16)],
        compiler_params=_params("parallel", "arbitrary"),
        name="inproj",
    )(x2d, g.reshape(1, D_MODEL), w_bf16)


def _merge_kernel(x_ref, ba_ref, bb_ref, bc_ref, bd_ref, p_ref, gmix_ref, wmg_ref, wbr_ref, wo_ref,
                  gple_ref, wpg_ref, wpp_ref, gfin_ref, out_ref, *, final):
    x = x_ref[...]
    h = _rms(x, gmix_ref[...]).astype(BF16)
    merged = None
    for j, b_ref in enumerate((ba_ref, bb_ref, bc_ref, bd_ref)):
        gate = _sigmoid(jnp.dot(h, wmg_ref[j], preferred_element_type=F32))
        term = gate * jnp.dot(b_ref[...], wbr_ref[j], preferred_element_type=F32)
        merged = term if merged is None else merged + term
    x1 = x + _dot(merged, wo_ref[...])
    gate2 = _sigmoid(_dot(_rms(x1, gple_ref[...]), wpg_ref[...]))
    x2 = x1 + gate2 * _dot(p_ref[...], wpp_ref[...])
    if final:
        x2 = _rms(x2, gfin_ref[...])
    out_ref[...] = x2


def _merge(x2d, branches, p2d, gmix, wmg, wbr, wo, gple, wpg, wpp, gfin, final):
    m = x2d.shape[0]
    tm = 256
    row = lambda i: (i, 0)
    const2 = lambda i: (0, 0)
    const3 = lambda i: (0, 0, 0)
    vec = pl.BlockSpec((1, D_MODEL), const2)
    return pl.pallas_call(
        functools.partial(_merge_kernel, final=final),
        grid=(m // tm,),
        in_specs=[
            pl.BlockSpec((tm, D_MODEL), row),
            pl.BlockSpec((tm, BRANCH_W), row),
            pl.BlockSpec((tm, BRANCH_W), row),
            pl.BlockSpec((tm, BRANCH_W), row),
            pl.BlockSpec((tm, BRANCH_W), row),
            pl.BlockSpec((tm, PLE_DIM), row),
            vec,
            pl.BlockSpec((N_BRANCH, D_MODEL, D_MODEL), const3),
            pl.BlockSpec((N_BRANCH, BRANCH_W, D_MODEL), const3),
            pl.BlockSpec((D_MODEL, D_MODEL), const2),
            vec,
            pl.BlockSpec((D_MODEL, D_MODEL), const2),
            pl.BlockSpec((PLE_DIM, D_MODEL), const2),
            vec,
        ],
        out_specs=pl.BlockSpec((tm, D_MODEL), row),
        out_shape=jax.ShapeDtypeStruct((m, D_MODEL), F32),
        compiler_params=_params("parallel"),
        name="merge",
    )(x2d, *branches, p2d, gmix.reshape(1, -1), wmg, wbr, wo, gple.reshape(1, -1), wpg, wpp,
      gfin.reshape(1, -1))


def _retention_kernel(q_ref, k_ref, v_ref, g_ref, cos_ref, sin_ref, lgq_ref, lgv_ref, out_ref,
                      qr_scr, kr_scr, sb_scr, sf_scr):
    t = q_ref.shape[0]
    c = CHUNK
    n_chunks = t // c
    hd = RET_QK

    lane = lax.broadcasted_iota(jnp.int32, (1, 2 * hd), 1)
    first_half = (lane % hd) < (hd // 2)
    head0_q = lane < hd
    lane_v = lax.broadcasted_iota(jnp.int32, (1, 2 * RET_V), 1)
    head0_v = lane_v < RET_V

    def rotary(x, cos, sin):
        partner = jnp.where(first_half, pltpu.roll(x, 2 * hd - hd // 2, 1), pltpu.roll(x, hd // 2, 1))
        return x * cos + partner * sin

    def rot_body(n, carry):
        sl = pl.ds(pl.multiple_of(n * c, c), c)
        cos = cos_ref[sl, :]
        sin = sin_ref[sl, :]
        qr_scr[sl, :] = rotary(q_ref[sl, :], cos, sin)
        kr_scr[sl, :] = rotary(k_ref[sl, :], cos, sin) * (hd ** -0.5)
        return carry

    lax.fori_loop(0, n_chunks, rot_body, 0)

    lg_f = -jnp.log(1.0 + jnp.exp(-lgq_ref[0:1, :]))
    lg_b = -jnp.log(1.0 + jnp.exp(-lgq_ref[1:2, :]))
    lgv_f = -jnp.log(1.0 + jnp.exp(-lgv_ref[0:1, :]))
    lgv_b = -jnp.log(1.0 + jnp.exp(-lgv_ref[1:2, :]))
    tcol = lax.broadcasted_iota(jnp.int32, (c, 1), 0).astype(F32)
    head_f = jnp.exp((tcol + 1.0) * lg_f)
    head_b = jnp.exp((c - tcol) * lg_b)
    tail_f = jnp.exp((c - 1.0 - tcol) * lg_f)
    tail_b = jnp.exp(tcol * lg_b)
    dec_f = jnp.exp(c * lgv_f)
    dec_b = jnp.exp(c * lgv_b)
    rowk = lax.broadcasted_iota(jnp.int32, (2 * hd, 2 * RET_V), 0)
    colv = lax.broadcasted_iota(jnp.int32, (2 * hd, 2 * RET_V), 1)
    blockdiag = (rowk // hd) == (colv // RET_V)

    diff = (lax.broadcasted_iota(jnp.int32, (c, c), 0) - lax.broadcasted_iota(jnp.int32, (c, c), 1)).astype(F32)

    def decay_mask(lf, lb):
        fwd = jnp.exp(jnp.maximum(diff, 0.0) * lf)
        bwd = jnp.exp(jnp.maximum(-diff, 0.0) * lb)
        return jnp.where(diff > 0, fwd, jnp.where(diff < 0, bwd, 2.0))

    dmask = jnp.concatenate([decay_mask(lgv_f[:, :c], lgv_b[:, :c]),
                             decay_mask(lgv_f[:, RET_V:RET_V + c], lgv_b[:, RET_V:RET_V + c])], axis=1)

    sf_scr[...] = jnp.zeros_like(sf_scr)

    def bwd_body(i, carry):
        n = n_chunks - 1 - i
        sl = pl.ds(pl.multiple_of(n * c, c), c)
        sb_scr[n] = sf_scr[...]
        loc = _dot_tn(kr_scr[sl, :] * tail_b, v_ref[sl, :])
        sf_scr[...] = sf_scr[...] * dec_b + jnp.where(blockdiag, loc, 0.0)
        return carry

    lax.fori_loop(0, n_chunks, bwd_body, 0, unroll=2)

    sf_scr[...] = jnp.zeros_like(sf_scr)

    def fwd_body(n, carry):
        sl = pl.ds(pl.multiple_of(n * c, c), c)
        qn = qr_scr[sl, :]
        kn = kr_scr[sl, :]
        vn = v_ref[sl, :]
        kstack = jnp.concatenate([jnp.where(head0_q, kn, 0.0), jnp.where(head0_q, 0.0, kn)], axis=0)
        scores = _dot_nt(qn, kstack) * dmask
        vstack = jnp.concatenate([jnp.where(head0_v, vn, 0.0), jnp.where(head0_v, 0.0, vn)], axis=0)
        lhs = jnp.concatenate([scores, qn * head_f, qn * head_b], axis=1)
        rhs = jnp.concatenate([vstack, sf_scr[...], sb_scr[n]], axis=0)
        o = _dot(lhs, rhs)
        loc = _dot_tn(kn * tail_f, vn)
        sf_scr[...] = sf_scr[...] * dec_f + jnp.where(blockdiag, loc, 0.0)
        o0 = o[:, :RET_V]
        o1 = o[:, RET_V:]
        o0 = o0 * lax.rsqrt(jnp.mean(o0 * o0, axis=-1, keepdims=True) + EPS)
        o1 = o1 * lax.rsqrt(jnp.mean(o1 * o1, axis=-1, keepdims=True) + EPS)
        on = jnp.concatenate([o0, o1], axis=1)
        out_ref[sl, :] = (on * _silu(g_ref[sl, :])).astype(out_ref.dtype)
        return carry

    lax.fori_loop(0, n_chunks, fwd_body, 0, unroll=2)


def _retention(z3, cos_tab, sin_tab, lgq, lgv):
    b, t, _ = z3.shape
    n_chunks = t // CHUNK
    return pl.pallas_call(
        _retention_kernel,
        grid=(b, RET_HEADS // 2),
        in_specs=[
            _zspec(t, 2 * RET_QK, OFF_RQ, 0),
            _zspec(t, 2 * RET_QK, OFF_RK, 0),
            _zspec(t, 2 * RET_V, OFF_RV, 0),
            _zspec(t, 2 * RET_V, OFF_RG, 0),
            pl.BlockSpec((t, 2 * RET_QK), lambda i, h: (0, 0)),
            pl.BlockSpec((t, 2 * RET_QK), lambda i, h: (0, 0)),
            pl.BlockSpec((2, 2 * RET_QK), lambda i, h: (0, h)),
            pl.BlockSpec((2, 2 * RET_V), lambda i, h: (0, h)),
        ],
        out_specs=pl.BlockSpec((None, t, 2 * RET_V), lambda i, h: (i, 0, h)),
        out_shape=jax.ShapeDtypeStruct((b, t, BRANCH_W), BF16),
        scratch_shapes=[
            pltpu.VMEM((t, 2 * RET_QK), F32),
            pltpu.VMEM((t, 2 * RET_QK), F32),
            pltpu.VMEM((n_chunks, 2 * RET_QK, 2 * RET_V), F32),
            pltpu.VMEM((2 * RET_QK, 2 * RET_V), F32),
        ],
        compiler_params=_params("parallel", "parallel"),
        name="retention",
    )(z3, z3, z3, z3, cos_tab, sin_tab, lgq, lgv)


def _rotary_tables(t):
    half = RET_QK // 2
    inv = ROPE_BASE ** (-jnp.arange(half, dtype=F32) / half)
    ang = jnp.arange(t, dtype=F32)[:, None] * inv[None, :]
    cos = jnp.cos(ang)
    sin = jnp.sin(ang)
    cos_tab = jnp.tile(jnp.concatenate([cos, cos], axis=1), (1, 2))
    sin_tab = jnp.tile(jnp.concatenate([-sin, sin], axis=1), (1, 2))
    return cos_tab, sin_tab


NAT_QROWS = 8
NAT_KROWS = 16
NAT_KCOLS = 2 * NAT_WIN_COLS


def _nat_kstart(j):
    return int(np.clip(j * NAT_WIN_COLS - NAT_WIN_COLS // 2, 0, GRID_W - NAT_KCOLS))


def _nat_key_row_base(g, rows):
    return int(np.clip(g * NAT_QROWS - NAT_WIN_ROWS // 2, 0, rows - NAT_KROWS))


def _nat_bias_tables(rpb, rows):
    n_g = rows // NAT_QROWS
    n_cb = GRID_W // NAT_WIN_COLS
    dr = np.zeros((n_g, NAT_QROWS, NAT_KROWS), np.int32)
    row_ok = np.zeros((n_g, NAT_QROWS, NAT_KROWS), bool)
    for g in range(n_g):
        kb = _nat_key_row_base(g, rows)
        for rr in range(NAT_QROWS):
            r = g * NAT_QROWS + rr
            rs = int(np.clip(r - NAT_WIN_ROWS // 2, 0, rows - NAT_WIN_ROWS))
            for kr in range(NAT_KROWS):
                ka = kb + kr
                row_ok[g, rr, kr] = rs <= ka < rs + NAT_WIN_ROWS
                dr[g, rr, kr] = int(np.clip(ka - r + NAT_WIN_ROWS - 1, 0, 2 * NAT_WIN_ROWS - 2))
    dc = np.zeros((n_cb, NAT_WIN_COLS, NAT_KCOLS), np.int32)
    col_ok = np.zeros((n_cb, NAT_WIN_COLS, NAT_KCOLS), bool)
    for j in range(n_cb):
        ks = _nat_kstart(j)
        for qq in range(NAT_WIN_COLS):
            qc = j * NAT_WIN_COLS + qq
            ws = int(np.clip(qc - NAT_WIN_COLS // 2, 0, GRID_W - NAT_WIN_COLS))
            for kc in range(NAT_KCOLS):
                ka = ks + kc
                col_ok[j, qq, kc] = ws <= ka < ws + NAT_WIN_COLS
                dc[j, qq, kc] = int(np.clip(ka - qc + NAT_WIN_COLS - 1, 0, 2 * NAT_WIN_COLS - 2))
    row_sel = (dr[..., None] == np.arange(2 * NAT_WIN_ROWS - 1)).astype(np.float32)
    col_sel = (dc[..., None] == np.arange(2 * NAT_WIN_COLS - 1)).astype(np.float32)
    by_row = jnp.einsum('grka,hab->hgrkb', row_sel, rpb.astype(F32), precision=lax.Precision.HIGHEST)
    bias = jnp.einsum('hgrkb,jqcb->hgjrqkc', by_row, col_sel, precision=lax.Precision.HIGHEST)
    ok6 = row_ok[:, None, :, None, :, None] & col_ok[None, :, None, :, None, :]
    bias = jnp.where(ok6[None], bias, -jnp.inf)
    nq = NAT_QROWS * NAT_WIN_COLS
    nk = NAT_KROWS * NAT_KCOLS
    bias = bias.reshape(NAT_HEADS // 2, 2, n_g, n_cb, nq, nk)
    return bias.transpose(0, 2, 3, 1, 4, 5)


def _nat_kernel(q_ref, k_ref, v_ref, g_ref, bias_ref, out_ref):
    t = q_ref.shape[0]
    rows = t // GRID_W
    n_g = rows // NAT_QROWS
    n_cb = GRID_W // NAT_WIN_COLS
    nq = NAT_QROWS * NAT_WIN_COLS
    lane = lax.broadcasted_iota(jnp.int32, (1, 2 * NAT_HD), 1)
    head0 = lane < NAT_HD

    def group_body(g, carry):
        kb = jnp.clip(g * NAT_QROWS - NAT_WIN_ROWS // 2, 0, rows - NAT_KROWS)
        for j in range(n_cb):
            ks = _nat_kstart(j)

            def qrows(ref, rr):
                start = pl.multiple_of((g * NAT_QROWS + rr) * GRID_W + j * NAT_WIN_COLS, NAT_WIN_COLS)
                return ref[pl.ds(start, NAT_WIN_COLS), :]

            def krows(ref, kr):
                start = pl.multiple_of((kb + kr) * GRID_W + ks, SUBLANES)
                return ref[pl.ds(start, NAT_KCOLS), :]

            qs = jnp.concatenate([qrows(q_ref, rr) for rr in range(NAT_QROWS)], axis=0) * (NAT_HD ** -0.5)
            q2 = jnp.concatenate([jnp.where(head0, qs, 0.0), jnp.where(head0, 0.0, qs)], axis=0)
            kblk = jnp.concatenate([krows(k_ref, kr) for kr in range(NAT_KROWS)], axis=0)
            vblk = jnp.concatenate([krows(v_ref, kr) for kr in range(NAT_KROWS)], axis=0)
            s = _dot_nt(q2, kblk)
            s = s + jnp.concatenate([bias_ref[g, j, 0], bias_ref[g, j, 1]], axis=0)
            m = jnp.max(s, axis=-1, keepdims=True)
            p = jnp.exp(s - m)
            denom = jnp.sum(p, axis=-1, keepdims=True)
            o2 = _dot(p, vblk) / denom
            o = jnp.where(head0, o2[:nq], o2[nq:])
            gs = jnp.concatenate([qrows(g_ref, rr) for rr in range(NAT_QROWS)], axis=0)
            res = (o * _silu(gs)).astype(out_ref.dtype)
            for rr in range(NAT_QROWS):
                start = pl.multiple_of((g * NAT_QROWS + rr) * GRID_W + j * NAT_WIN_COLS, NAT_WIN_COLS)
                out_ref[pl.ds(start, NAT_WIN_COLS), :] = res[rr * NAT_WIN_COLS:(rr + 1) * NAT_WIN_COLS]
        return carry

    lax.fori_loop(0, n_g, group_body, 0)


def _nat(z3, bias_tab):
    b, t, _ = z3.shape
    n_hp = NAT_HEADS // 2
    w = 2 * NAT_HD
    return pl.pallas_call(
        _nat_kernel,
        grid=(n_hp, b),
        in_specs=[
            _zspec(t, w, OFF_NQ, 1),
            _zspec(t, w, OFF_NK, 1),
            _zspec(t, w, OFF_NV, 1),
            _zspec(t, w, OFF_NG, 1),
            pl.BlockSpec((None,) + bias_tab.shape[1:], lambda h, i: (h, 0, 0, 0, 0, 0)),
        ],
        out_specs=pl.BlockSpec((None, t, w), lambda h, i: (i, 0, h)),
        out_shape=jax.ShapeDtypeStruct((b, t, BRANCH_W), BF16),
        compiler_params=_params("parallel", "parallel"),
        name="nat",
    )(z3, z3, z3, z3, bias_tab)


LRU_TC = 256


def _rglru_kernel(x_ref, g_ref, cw_ref, cb_ref, wg_ref, bg_ref, lam_ref, out_ref,
                  xp_scr, af_scr, bf_scr, ab_scr, bb_scr, hf_scr, hb_scr):
    t = x_ref.shape[0]
    tc = LRU_TC
    pad = SUBLANES
    w = LANES

    xp_scr[0:pad, :] = jnp.zeros((pad, w), F32)
    xp_scr[pad + t:pad + t + pad, :] = jnp.zeros((pad, w), F32)
    xp_scr[pad:pad + t, :] = x_ref[...]

    lam = lam_ref[...]
    neg = -lam
    softplus = jnp.maximum(neg, 0.0) + jnp.log(1.0 + jnp.exp(-jnp.abs(neg)))
    cw = cw_ref[...]
    cb = cb_ref[...]
    bg = bg_ref[...]

    def gate_body(n, carry):
        t0 = pl.multiple_of(n * tc, tc)
        xx = xp_scr[pl.ds(t0, tc + 2 * pad), :]
        total = tc + 2 * pad
        xc = cb
        for j in range(LRU_CONV):
            shift = LRU_CONV // 2 - j
            xs = xx if shift == 0 else pltpu.roll(xx, shift % total, 0)
            xc = xc + xs[pad:pad + tc] * cw[j:j + 1, :]
        gates = _dot(xc, wg_ref[...]) + bg
        sl = pl.ds(t0, tc)
        for d, (a_scr, b_scr) in enumerate(((af_scr, bf_scr), (ab_scr, bb_scr))):
            r = _sigmoid(gates[:, (2 * d) * w:(2 * d + 1) * w])
            i = _sigmoid(gates[:, (2 * d + 1) * w:(2 * d + 2) * w])
            log_a = -LRU_C * r * softplus[d:d + 1, :]
            a = jnp.exp(log_a)
            a_scr[sl, :] = a
            gap = 1.0 - a * a
            b_scr[sl, :] = gap * lax.rsqrt(jnp.maximum(gap, 1e-30)) * (i * xc)
        return carry

    lax.fori_loop(0, t // tc, gate_body, 0)

    n_tiles = t // SUBLANES
    row = lax.broadcasted_iota(jnp.int32, (SUBLANES, w), 0)

    def scan_body(i, carry):
        hf_prev, hb_next = carry
        sl = pl.ds(pl.multiple_of(i * SUBLANES, SUBLANES), SUBLANES)
        a = af_scr[sl, :]
        b = bf_scr[sl, :]
        for s in (1, 2, 4):
            keep = row >= s
            b = jnp.where(keep, a * pltpu.roll(b, s, 0) + b, b)
            a = jnp.where(keep, a * pltpu.roll(a, s, 0), a)
        hf_scr[sl, :] = a * hf_prev + b
        hf_prev = (jnp.broadcast_to(a[SUBLANES - 1:SUBLANES, :], (SUBLANES, w)) * hf_prev
                   + jnp.broadcast_to(b[SUBLANES - 1:SUBLANES, :], (SUBLANES, w)))
        slb = pl.ds(pl.multiple_of((n_tiles - 1 - i) * SUBLANES, SUBLANES), SUBLANES)
        a = ab_scr[slb, :]
        b = bb_scr[slb, :]
        for s in (1, 2, 4):
            keep = row < SUBLANES - s
            b = jnp.where(keep, a * pltpu.roll(b, SUBLANES - s, 0) + b, b)
            a = jnp.where(keep, a * pltpu.roll(a, SUBLANES - s, 0), a)
        hb_scr[slb, :] = a * hb_next + b
        hb_next = (jnp.broadcast_to(a[0:1, :], (SUBLANES, w)) * hb_next
                   + jnp.broadcast_to(b[0:1, :], (SUBLANES, w)))
        return hf_prev, hb_next

    zero = jnp.zeros((SUBLANES, w), F32)
    lax.fori_loop(0, n_tiles, scan_body, (zero, zero), unroll=4)

    def out_body(n, carry):
        sl = pl.ds(pl.multiple_of(n * tc, tc), tc)
        out_ref[sl, :] = ((hf_scr[sl, :] + hb_scr[sl, :]) * _silu(g_ref[sl, :])).astype(out_ref.dtype)
        return carry

    lax.fori_loop(0, t // tc, out_body, 0)


def _rglru(z3, conv_w, conv_b, wg, bg, lam):
    b, t, _ = z3.shape
    n_cb = BRANCH_W // LANES
    return pl.pallas_call(
        _rglru_kernel,
        grid=(b, n_cb),
        in_specs=[
            _zspec(t, LANES, OFF_LX, 0),
            _zspec(t, LANES, OFF_LG, 0),
            pl.BlockSpec((LRU_CONV, LANES), lambda i, h: (0, h)),
            pl.BlockSpec((1, LANES), lambda i, h: (0, h)),
            pl.BlockSpec((None, LANES, 4 * LANES), lambda i, h: (h, 0, 0)),
            pl.BlockSpec((None, 1, 4 * LANES), lambda i, h: (h, 0, 0)),
            pl.BlockSpec((2, LANES), lambda i, h: (0, h)),
        ],
        out_specs=pl.BlockSpec((None, t, LANES), lambda i, h: (i, 0, h)),
        out_shape=jax.ShapeDtypeStruct((b, t, BRANCH_W), BF16),
        scratch_shapes=[pltpu.VMEM((t + 2 * SUBLANES, LANES), F32)] + [pltpu.VMEM((t, LANES), F32)] * 6,
        compiler_params=_params("parallel", "parallel"),
        name="rglru",
    )(z3, z3, conv_w, conv_b.reshape(1, BRANCH_W), wg, bg, lam)


def _rglru_gate_weights(wa, ba, wx, bx):
    n_cb = BRANCH_W // LANES
    per = LANES // LRU_BW

    def blockdiag(w):
        w = w.reshape(n_cb, per, LRU_BW, LRU_BW)
        eye = jnp.eye(per, dtype=w.dtype)
        return jnp.einsum('cpjk,pq->cpjqk', w, eye).reshape(n_cb, LANES, LANES)

    wg = jnp.concatenate([blockdiag(wa[0]), blockdiag(wx[0]), blockdiag(wa[1]), blockdiag(wx[1])], axis=-1)
    bg = jnp.concatenate([ba[0].reshape(n_cb, 1, LANES), bx[0].reshape(n_cb, 1, LANES),
                          ba[1].reshape(n_cb, 1, LANES), bx[1].reshape(n_cb, 1, LANES)], axis=-1)
    return wg.astype(BF16), bg.astype(F32)


def _prefix_sum_rows(x):
    c, w = x.shape
    x3 = x.reshape(c // SUBLANES, SUBLANES, w)
    row = lax.broadcasted_iota(jnp.int32, (1, SUBLANES, w), 1)
    s = 1
    while s < SUBLANES:
        x3 = x3 + jnp.where(row >= s, pltpu.roll(x3, s, 1), 0.0)
        s *= 2
    x = x3.reshape(c, w)
    tile_tot = _block_row(x, SUBLANES, SUBLANES - 1)
    while s < c:
        shifted = jnp.concatenate([jnp.zeros((s, w), x.dtype), tile_tot[:c - s]], axis=0)
        x = x + shifted
        tile_tot = tile_tot + shifted
        s *= 2
    return x


def _block_row(x, size, pos):
    c, w = x.shape
    if size >= SUBLANES:
        x3 = x.reshape(c // size, size, w)
        return jnp.broadcast_to(x3[:, pos:pos + 1, :], x3.shape).reshape(c, w)
    rpos = lax.broadcasted_iota(jnp.int32, x.shape, 0) % size
    out = x
    for p in range(size):
        if p != pos:
            out = jnp.where(rpos == p, pltpu.roll(x, (p - pos) % c, 0), out)
    return out


def _hgrn_forget(zf, lb):
    f = lb + (1.0 - lb) * _sigmoid(zf)
    return f, jnp.log2(f)


def _hgrn_level_decays(size, f, bs, fb, cs, pos):
    c = f.shape[0]
    half = size // 2
    if size == 2:
        return jnp.where(pos == 1, f, 1.0), jnp.where(pos == 0, fb, 1.0)
    if size == 4:
        f_prev, f_next = pltpu.roll(f, 1, 0), pltpu.roll(f, c - 1, 0)
        fb_prev, fb_next = pltpu.roll(fb, 1, 0), pltpu.roll(fb, c - 1, 0)
        e_f = jnp.where(pos == 0, f_next, jnp.where(pos == 1, 1.0, jnp.where(pos == 2, f, f_prev * f)))
        e_b = jnp.where(pos == 0, fb * fb_next, jnp.where(pos == 1, fb, jnp.where(pos == 2, 1.0, fb_prev)))
        return e_f, e_b
    sign = jnp.where(pos >= half, 1.0, -1.0)
    e_f = jnp.exp2((bs - _block_row(bs, size, half - 1)) * sign)
    e_b = jnp.exp2((_block_row(cs, size, half) - cs) * sign)
    return e_f, e_b


def _hgrn_level_operands(size, q, kf, kb, f, fb, bs, cs, trow):
    c, w = q.shape
    half = size // 2
    if half < SUBLANES:
        pos = trow % size
        upper = pos >= half
        e_f, e_b = _hgrn_level_decays(size, f, bs, fb, cs, pos)
        z_f = e_f * jnp.where(upper, q, kf)
        z_b = e_b * jnp.where(upper, kb, q)
        x = jnp.concatenate([jnp.where(upper, z_f, 0.0), jnp.where(upper, 0.0, z_b)], axis=1)
        return x, jnp.concatenate([z_f, z_b], axis=1)
    xs, ys = [], []
    zero = jnp.zeros((half, w), F32)
    for i in range(c // half):
        rows = slice(i * half, (i + 1) * half)
        if i % 2 == 0:
            ref_f = bs[(i + 1) * half - 1:(i + 1) * half]
            ref_b = cs[(i + 1) * half:(i + 1) * half + 1]
            z_f = kf[rows] * jnp.exp2(ref_f - bs[rows])
            z_b = q[rows] * jnp.exp2(cs[rows] - ref_b)
            xs.append(jnp.concatenate([zero, z_b], axis=1))
        else:
            ref_f = bs[i * half - 1:i * half]
            ref_b = cs[i * half:i * half + 1]
            z_f = q[rows] * jnp.exp2(bs[rows] - ref_f)
            z_b = kb[rows] * jnp.exp2(ref_b - cs[rows])
            xs.append(jnp.concatenate([z_f, zero], axis=1))
        ys.append(jnp.concatenate([z_f, z_b], axis=1))
    return jnp.concatenate(xs, axis=0), jnp.concatenate(ys, axis=0)


def _hgrn_kernel(q_ref, ff_ref, fb_ref, v_ref, g_ref, lbl_ref, gain_ref, out_ref,
                 fb_scr, cs_scr, sb_scr, st_scr, *, layer):
    t = q_ref.shape[0]
    c = CHUNK
    n_chunks = t // c
    w = HGRN_DK

    logits = lbl_ref[...]
    mx = jnp.max(logits, axis=0)
    ex = jnp.exp(logits - mx[None])
    tot = jnp.sum(ex, axis=0)
    lb = jnp.zeros_like(tot)
    for i in range(1, layer + 1):
        lb = lb + ex[i] / tot
    lb_f = lb[0:1, :]
    lb_b = lb[1:2, :]

    pair_xor = lax.broadcasted_iota(jnp.int32, (c, c), 0) ^ lax.broadcasted_iota(jnp.int32, (c, c), 1)
    pair_level = jnp.zeros((c, c), jnp.int32)
    size = 2
    while size <= c:
        pair_level = jnp.where(pair_xor >= size // 2, size, pair_level)
        size *= 2
    trow = lax.broadcasted_iota(jnp.int32, (c, 1), 0)

    st_scr[...] = jnp.zeros_like(st_scr)

    def bwd_body(i, carry):
        n = n_chunks - 1 - i
        sl = pl.ds(pl.multiple_of(n * c, c), c)
        fb, gb = _hgrn_forget(fb_ref[sl, :], lb_b)
        pre = _prefix_sum_rows(gb)
        total = pre[c - 1:c, :]
        cs = total - pre + gb
        fb_scr[sl, :] = fb
        cs_scr[sl, :] = cs
        sb_scr[n] = st_scr[...]
        st_scr[...] = st_scr[...] * jnp.exp2(total) + _dot_tn(v_ref[sl, :], (1.0 - fb) * jnp.exp2(total - cs))
        return carry

    lax.fori_loop(0, n_chunks, bwd_body, 0, unroll=2)

    st_scr[...] = jnp.zeros_like(st_scr)

    def fwd_body(n, carry):
        sl = pl.ds(pl.multiple_of(n * c, c), c)
        q = _silu(q_ref[sl, :])
        v = v_ref[sl, :]
        f, gf = _hgrn_forget(ff_ref[sl, :], lb_f)
        kf = 1.0 - f
        bs = _prefix_sum_rows(gf)
        fb = fb_scr[sl, :]
        kb = 1.0 - fb
        cs = cs_scr[sl, :]

        att = None
        size = 2
        while size <= c:
            x, y = _hgrn_level_operands(size, q, kf, kb, f, fb, bs, cs, trow)
            att = jnp.where(pair_level == size, _dot_nt(x, y), 0.0 if att is None else att)
            size *= 2

        diag = jnp.sum(q * (kf + kb), axis=-1, keepdims=True)
        o = _dot(att, v) + diag * v
        inter = jnp.concatenate([q * jnp.exp2(bs), q * jnp.exp2(cs)], axis=1)
        states = jnp.concatenate([st_scr[...], sb_scr[n]], axis=1)
        o = o + _dot_nt(inter, states)
        last = bs[c - 1:c, :]
        st_scr[...] = st_scr[...] * jnp.exp2(last) + _dot_tn(v, kf * jnp.exp2(last - bs))
        o = o * lax.rsqrt(jnp.mean(o * o, axis=-1, keepdims=True) + EPS) * gain_ref[...]
        out_ref[sl, :] = (o * _silu(g_ref[sl, :])).astype(out_ref.dtype)
        return carry

    lax.fori_loop(0, n_chunks, fwd_body, 0, unroll=2)


def _hgrn(z3, lb_logits, gain, layer):
    b, t, _ = z3.shape
    depth = lb_logits.shape[0]
    w = HGRN_DK
    n_chunks = t // CHUNK
    return pl.pallas_call(
        functools.partial(_hgrn_kernel, layer=layer),
        grid=(b, HGRN_HEADS),
        in_specs=[
            _zspec(t, w, OFF_HQ, 0),
            _zspec(t, w, OFF_HFF, 0),
            _zspec(t, w, OFF_HFB, 0),
            _zspec(t, w, OFF_HI, 0),
            _zspec(t, w, OFF_HG, 0),
            pl.BlockSpec((depth, 2, w), lambda i, h: (0, 0, h)),
            pl.BlockSpec((1, w), lambda i, h: (0, h)),
        ],
        out_specs=pl.BlockSpec((None, t, w), lambda i, h: (i, 0, h)),
        out_shape=jax.ShapeDtypeStruct((b, t, BRANCH_W), BF16),
        scratch_shapes=[
            pltpu.VMEM((t, w), F32),
            pltpu.VMEM((t, w), F32),
            pltpu.VMEM((n_chunks, w, w), F32),
            pltpu.VMEM((w, w), F32),
        ],
        compiler_params=_params("parallel", "parallel"),
        name="hgrn",
    )(z3, z3, z3, z3, z3, lb_logits, gain.reshape(1, -1))


def _encoder(x, p, w):
    b, t, _ = x.shape
    depth = w['w_in'].shape[0]
    x2d = x.reshape(b * t, D_MODEL)
    for l in range(depth):
        z3 = _inproj(x2d, w['norm_mix'][l], w['w_in'][l]).reshape(b, t, W_IN)
        br_a = _retention(z3, w['cos'], w['sin'], w['lgq'][l], w['lgv'][l])
        br_b = _nat(z3, w['nat_bias'][l])
        br_c = _rglru(z3, w['conv_w'][l], w['conv_b'][l], w['lru_wg'][l], w['lru_bg'][l], w['lam'][l])
        br_d = _hgrn(z3, w['lb_logits'], w['hgrn_gain'][l], l)
        branches = [a.reshape(b * t, BRANCH_W) for a in (br_a, br_b, br_c, br_d)]
        x2d = _merge(x2d, branches, p[l].reshape(b * t, PLE_DIM), w['norm_mix'][l], w['w_merge'][l],
                     w['w_branch'][l], w['w_out'][l], w['ple_norm'][l], w['w_ple_gate'][l],
                     w['w_ple_proj'][l], w['final_norm'], l == depth - 1)
    return x2d.reshape(b, t, D_MODEL)


def kernel(x_prompt, x_sample, p_prompt, p_sample, norm_mix, w_in, ret_decay_logit, nat_rpb, lru_conv_w,
           lru_conv_b, lru_wa, lru_ba, lru_wx, lru_bx, lru_lambda, hgrn_lb_logits, hgrn_norm, w_branch,
           w_merge, w_out, ple_norm, w_ple_gate, w_ple_proj, final_norm):
    depth = w_in.shape[0]
    t = x_prompt.shape[1]
    rows = t // GRID_W
    cos_tab, sin_tab = _rotary_tables(t)
    gate_w = [_rglru_gate_weights(lru_wa[l], lru_ba[l], lru_wx[l], lru_bx[l]) for l in range(depth)]
    weights = {
        'norm_mix': norm_mix,
        'w_in': w_in.astype(BF16),
        'cos': cos_tab,
        'sin': sin_tab,
        'lgq': jnp.repeat(ret_decay_logit.astype(F32), RET_QK, axis=-1),
        'lgv': jnp.repeat(ret_decay_logit.astype(F32), RET_V, axis=-1),
        'nat_bias': [_nat_bias_tables(nat_rpb[l], rows) for l in range(depth)],
        'conv_w': lru_conv_w,
        'conv_b': lru_conv_b,
        'lru_wg': [g[0] for g in gate_w],
        'lru_bg': [g[1] for g in gate_w],
        'lam': lru_lambda,
        'lb_logits': hgrn_lb_logits,
        'hgrn_gain': hgrn_norm,
        'w_branch': w_branch.astype(BF16),
        'w_merge': w_merge.astype(BF16),
        'w_out': w_out.astype(BF16),
        'ple_norm': ple_norm,
        'w_ple_gate': w_ple_gate.astype(BF16),
        'w_ple_proj': w_ple_proj.astype(BF16),
        'final_norm': final_norm,
    }
    y_prompt = _encoder(x_prompt, p_prompt, weights)
    y_sample = _encoder(x_sample, p_sample, weights)
    return (y_prompt, y_sample)
```

```python
import functools

import numpy as np
import jax
import jax.numpy as jnp
from jax import lax
from jax.experimental import pallas as pl
from jax.experimental.pallas import tpu as pltpu

F32 = jnp.float32
BF16 = jnp.bfloat16

D_MODEL = 1024
PLE_DIM = 256
GRID_W = 64
N_BRANCH = 4
BRANCH_W = 512
RET_HEADS = 4
RET_QK = 64
RET_V = 128
ROPE_BASE = 10000.0
NAT_HEADS = 8
NAT_HD = 64
NAT_WIN_ROWS = 8
NAT_WIN_COLS = 16
LRU_BLOCKS = 8
LRU_BW = 64
LRU_CONV = 4
LRU_C = 8.0
HGRN_HEADS = 4
HGRN_DK = 128
EPS = 1e-6
W_IN = 7168

OFF_RQ, OFF_RK, OFF_RV, OFF_RG = 0, 256, 512, 1024
OFF_NQ, OFF_NK, OFF_NV, OFF_NG = 1536, 2048, 2560, 3072
OFF_LX, OFF_LG = 3584, 4096
OFF_HQ, OFF_HFF, OFF_HFB, OFF_HI, OFF_HG = 4608, 5120, 5632, 6144, 6656

LANES = 128
SUBLANES = 8
VMEM_LIMIT = 56 * 1024 * 1024

CHUNK = 128


def _params(*sem):
    return pltpu.CompilerParams(dimension_semantics=sem, vmem_limit_bytes=VMEM_LIMIT)


def _dot(a, b):
    return jnp.dot(a.astype(BF16), b.astype(BF16), preferred_element_type=F32)


def _dot_nt(a, b):
    return lax.dot_general(a.astype(BF16), b.astype(BF16), (((1,), (1,)), ((), ())),
                           preferred_element_type=F32)


def _dot_tn(a, b):
    return lax.dot_general(a.astype(BF16), b.astype(BF16), (((0,), (0,)), ((), ())),
                           preferred_element_type=F32)


def _rms(x, g):
    return x * lax.rsqrt(jnp.mean(x * x, axis=-1, keepdims=True) + EPS) * g


def _sigmoid(x):
    return jax.nn.sigmoid(x)


def _silu(x):
    return x * jax.nn.sigmoid(x)


def _zspec(t, width, off, grid_pos):
    base = off // width
    if grid_pos == 0:
        return pl.BlockSpec((None, t, width), lambda b, h: (b, 0, base + h))
    return pl.BlockSpec((None, t, width), lambda h, b: (b, 0, base + h))


def _inproj_kernel(x_ref, g_ref, w_ref, z_ref, h_scr):
    @pl.when(pl.program_id(1) == 0)
    def _():
        h_scr[...] = _rms(x_ref[...], g_ref[...]).astype(BF16)

    z_ref[...] = jnp.dot(h_scr[...], w_ref[...], preferred_element_type=F32)


def _inproj(x2d, g, w_bf16):
    m = x2d.shape[0]
    tm, tn = 1024, 1792
    return pl.pallas_call(
        _inproj_kernel,
        grid=(m // tm, W_IN // tn),
        in_specs=[
            pl.BlockSpec((tm, D_MODEL), lambda i, j: (i, 0)),
            pl.BlockSpec((1, D_MODEL), lambda i, j: (0, 0)),
            pl.BlockSpec((D_MODEL, tn), lambda i, j: (0, j)),
        ],
        out_specs=pl.BlockSpec((tm, tn), lambda i, j: (i, j)),
        out_shape=jax.ShapeDtypeStruct((m, W_IN), F32),
        scratch_shapes=[pltpu.VMEM((tm, D_MODEL), BF16)],
        compiler_params=_params("parallel", "arbitrary"),
        name="inproj",
    )(x2d, g.reshape(1, D_MODEL), w_bf16)


def _merge_kernel(x_ref, ba_ref, bb_ref, bc_ref, bd_ref, p_ref, gmix_ref, wmg_ref, wbr_ref, wo_ref,
                  gple_ref, wpg_ref, wpp_ref, gfin_ref, out_ref, *, final):
    x = x_ref[...]
    h = _rms(x, gmix_ref[...]).astype(BF16)
    merged = None
    for j, b_ref in enumerate((ba_ref, bb_ref, bc_ref, bd_ref)):
        gate = _sigmoid(jnp.dot(h, wmg_ref[j], preferred_element_type=F32))
        term = gate * jnp.dot(b_ref[...], wbr_ref[j], preferred_element_type=F32)
        merged = term if merged is None else merged + term
    x1 = x + _dot(merged, wo_ref[...])
    gate2 = _sigmoid(_dot(_rms(x1, gple_ref[...]), wpg_ref[...]))
    x2 = x1 + gate2 * _dot(p_ref[...], wpp_ref[...])
    if final:
        x2 = _rms(x2, gfin_ref[...])
    out_ref[...] = x2


def _merge(x2d, branches, p2d, gmix, wmg, wbr, wo, gple, wpg, wpp, gfin, final):
    m = x2d.shape[0]
    tm = 256
    row = lambda i: (i, 0)
    const2 = lambda i: (0, 0)
    const3 = lambda i: (0, 0, 0)
    vec = pl.BlockSpec((1, D_MODEL), const2)
    return pl.pallas_call(
        functools.partial(_merge_kernel, final=final),
        grid=(m // tm,),
        in_specs=[
            pl.BlockSpec((tm, D_MODEL), row),
            pl.BlockSpec((tm, BRANCH_W), row),
            pl.BlockSpec((tm, BRANCH_W), row),
            pl.BlockSpec((tm, BRANCH_W), row),
            pl.BlockSpec((tm, BRANCH_W), row),
            pl.BlockSpec((tm, PLE_DIM), row),
            vec,
            pl.BlockSpec((N_BRANCH, D_MODEL, D_MODEL), const3),
            pl.BlockSpec((N_BRANCH, BRANCH_W, D_MODEL), const3),
            pl.BlockSpec((D_MODEL, D_MODEL), const2),
            vec,
            pl.BlockSpec((D_MODEL, D_MODEL), const2),
            pl.BlockSpec((PLE_DIM, D_MODEL), const2),
            vec,
        ],
        out_specs=pl.BlockSpec((tm, D_MODEL), row),
        out_shape=jax.ShapeDtypeStruct((m, D_MODEL), F32),
        compiler_params=_params("parallel"),
        name="merge",
    )(x2d, *branches, p2d, gmix.reshape(1, -1), wmg, wbr, wo, gple.reshape(1, -1), wpg, wpp,
      gfin.reshape(1, -1))


def _retention_kernel(q_ref, k_ref, v_ref, g_ref, cos_ref, sin_ref, lgq_ref, lgv_ref, out_ref,
                      qr_scr, kr_scr, sb_scr, sf_scr):
    t = q_ref.shape[0]
    c = CHUNK
    n_chunks = t // c
    hd = RET_QK

    lane = lax.broadcasted_iota(jnp.int32, (1, 2 * hd), 1)
    first_half = (lane % hd) < (hd // 2)
    head0_q = lane < hd
    lane_v = lax.broadcasted_iota(jnp.int32, (1, 2 * RET_V), 1)
    head0_v = lane_v < RET_V

    def rotary(x, cos, sin):
        partner = jnp.where(first_half, pltpu.roll(x, 2 * hd - hd // 2, 1), pltpu.roll(x, hd // 2, 1))
        return x * cos + partner * sin

    def rot_body(n, carry):
        sl = pl.ds(pl.multiple_of(n * c, c), c)
        cos = cos_ref[sl, :]
        sin = sin_ref[sl, :]
        qr_scr[sl, :] = rotary(q_ref[sl, :], cos, sin)
        kr_scr[sl, :] = rotary(k_ref[sl, :], cos, sin) * (hd ** -0.5)
        return carry

    lax.fori_loop(0, n_chunks, rot_body, 0)

    lg_f = -jnp.log(1.0 + jnp.exp(-lgq_ref[0:1, :]))
    lg_b = -jnp.log(1.0 + jnp.exp(-lgq_ref[1:2, :]))
    lgv_f = -jnp.log(1.0 + jnp.exp(-lgv_ref[0:1, :]))
    lgv_b = -jnp.log(1.0 + jnp.exp(-lgv_ref[1:2, :]))
    tcol = lax.broadcasted_iota(jnp.int32, (c, 1), 0).astype(F32)
    head_f = jnp.exp((tcol + 1.0) * lg_f)
    head_b = jnp.exp((c - tcol) * lg_b)
    tail_f = jnp.exp((c - 1.0 - tcol) * lg_f)
    tail_b = jnp.exp(tcol * lg_b)
    dec_f = jnp.exp(c * lgv_f)
    dec_b = jnp.exp(c * lgv_b)
    rowk = lax.broadcasted_iota(jnp.int32, (2 * hd, 2 * RET_V), 0)
    colv = lax.broadcasted_iota(jnp.int32, (2 * hd, 2 * RET_V), 1)
    blockdiag = (rowk // hd) == (colv // RET_V)

    diff = (lax.broadcasted_iota(jnp.int32, (c, c), 0) - lax.broadcasted_iota(jnp.int32, (c, c), 1)).astype(F32)

    def decay_mask(lf, lb):
        fwd = jnp.exp(jnp.maximum(diff, 0.0) * lf)
        bwd = jnp.exp(jnp.maximum(-diff, 0.0) * lb)
        return jnp.where(diff > 0, fwd, jnp.where(diff < 0, bwd, 2.0))

    dmask = jnp.concatenate([decay_mask(lgv_f[:, :c], lgv_b[:, :c]),
                             decay_mask(lgv_f[:, RET_V:RET_V + c], lgv_b[:, RET_V:RET_V + c])], axis=1)

    sf_scr[...] = jnp.zeros_like(sf_scr)

    def bwd_body(i, carry):
        n = n_chunks - 1 - i
        sl = pl.ds(pl.multiple_of(n * c, c), c)
        sb_scr[n] = sf_scr[...]
        loc = _dot_tn(kr_scr[sl, :] * tail_b, v_ref[sl, :])
        sf_scr[...] = sf_scr[...] * dec_b + jnp.where(blockdiag, loc, 0.0)
        return carry

    lax.fori_loop(0, n_chunks, bwd_body, 0, unroll=2)

    sf_scr[...] = jnp.zeros_like(sf_scr)

    def fwd_body(n, carry):
        sl = pl.ds(pl.multiple_of(n * c, c), c)
        qn = qr_scr[sl, :]
        kn = kr_scr[sl, :]
        vn = v_ref[sl, :]
        kstack = jnp.concatenate([jnp.where(head0_q, kn, 0.0), jnp.where(head0_q, 0.0, kn)], axis=0)
        scores = _dot_nt(qn, kstack) * dmask
        vstack = jnp.concatenate([jnp.where(head0_v, vn, 0.0), jnp.where(head0_v, 0.0, vn)], axis=0)
        lhs = jnp.concatenate([scores, qn * head_f, qn * head_b], axis=1)
        rhs = jnp.concatenate([vstack, sf_scr[...], sb_scr[n]], axis=0)
        o = _dot(lhs, rhs)
        loc = _dot_tn(kn * tail_f, vn)
        sf_scr[...] = sf_scr[...] * dec_f + jnp.where(blockdiag, loc, 0.0)
        o0 = o[:, :RET_V]
        o1 = o[:, RET_V:]
        o0 = o0 * lax.rsqrt(jnp.mean(o0 * o0, axis=-1, keepdims=True) + EPS)
        o1 = o1 * lax.rsqrt(jnp.mean(o1 * o1, axis=-1, keepdims=True) + EPS)
        on = jnp.concatenate([o0, o1], axis=1)
        out_ref[sl, :] = (on * _silu(g_ref[sl, :])).astype(out_ref.dtype)
        return carry

    lax.fori_loop(0, n_chunks, fwd_body, 0, unroll=2)


def _retention(z3, cos_tab, sin_tab, lgq, lgv):
    b, t, _ = z3.shape
    n_chunks = t // CHUNK
    return pl.pallas_call(
        _retention_kernel,
        grid=(b, RET_HEADS // 2),
        in_specs=[
            _zspec(t, 2 * RET_QK, OFF_RQ, 0),
            _zspec(t, 2 * RET_QK, OFF_RK, 0),
            _zspec(t, 2 * RET_V, OFF_RV, 0),
            _zspec(t, 2 * RET_V, OFF_RG, 0),
            pl.BlockSpec((t, 2 * RET_QK), lambda i, h: (0, 0)),
            pl.BlockSpec((t, 2 * RET_QK), lambda i, h: (0, 0)),
            pl.BlockSpec((2, 2 * RET_QK), lambda i, h: (0, h)),
            pl.BlockSpec((2, 2 * RET_V), lambda i, h: (0, h)),
        ],
        out_specs=pl.BlockSpec((None, t, 2 * RET_V), lambda i, h: (i, 0, h)),
        out_shape=jax.ShapeDtypeStruct((b, t, BRANCH_W), BF16),
        scratch_shapes=[
            pltpu.VMEM((t, 2 * RET_QK), F32),
            pltpu.VMEM((t, 2 * RET_QK), F32),
            pltpu.VMEM((n_chunks, 2 * RET_QK, 2 * RET_V), F32),
            pltpu.VMEM((2 * RET_QK, 2 * RET_V), F32),
        ],
        compiler_params=_params("parallel", "parallel"),
        name="retention",
    )(z3, z3, z3, z3, cos_tab, sin_tab, lgq, lgv)


def _rotary_tables(t):
    half = RET_QK // 2
    inv = ROPE_BASE ** (-jnp.arange(half, dtype=F32) / half)
    ang = jnp.arange(t, dtype=F32)[:, None] * inv[None, :]
    cos = jnp.cos(ang)
    sin = jnp.sin(ang)
    cos_tab = jnp.tile(jnp.concatenate([cos, cos], axis=1), (1, 2))
    sin_tab = jnp.tile(jnp.concatenate([-sin, sin], axis=1), (1, 2))
    return cos_tab, sin_tab


NAT_QROWS = 8
NAT_KROWS = 16
NAT_KCOLS = 2 * NAT_WIN_COLS


def _nat_kstart(j):
    return int(np.clip(j * NAT_WIN_COLS - NAT_WIN_COLS // 2, 0, GRID_W - NAT_KCOLS))


def _nat_key_row_base(g, rows):
    return int(np.clip(g * NAT_QROWS - NAT_WIN_ROWS // 2, 0, rows - NAT_KROWS))


def _edge_class(i, n):
    return 0 if i == 0 else (2 if i == n - 1 else 1)


def _nat_bias_tables(rpb, rows):
    n_g = rows // NAT_QROWS
    n_cb = GRID_W // NAT_WIN_COLS

    def row_geometry(g):
        dr = np.zeros((NAT_QROWS, NAT_KROWS), np.int32)
        ok = np.zeros((NAT_QROWS, NAT_KROWS), bool)
        kb = _nat_key_row_base(g, rows)
        for rr in range(NAT_QROWS):
            r = g * NAT_QROWS + rr
            rs = int(np.clip(r - NAT_WIN_ROWS // 2, 0, rows - NAT_WIN_ROWS))
            for kr in range(NAT_KROWS):
                ka = kb + kr
                ok[rr, kr] = rs <= ka < rs + NAT_WIN_ROWS
                dr[rr, kr] = int(np.clip(ka - r + NAT_WIN_ROWS - 1, 0, 2 * NAT_WIN_ROWS - 2))
        return dr, ok

    def col_geometry(j):
        dc = np.zeros((NAT_WIN_COLS, NAT_KCOLS), np.int32)
        ok = np.zeros((NAT_WIN_COLS, NAT_KCOLS), bool)
        ks = _nat_kstart(j)
        for qq in range(NAT_WIN_COLS):
            qc = j * NAT_WIN_COLS + qq
            ws = int(np.clip(qc - NAT_WIN_COLS // 2, 0, GRID_W - NAT_WIN_COLS))
            for kc in range(NAT_KCOLS):
                ka = ks + kc
                ok[qq, kc] = ws <= ka < ws + NAT_WIN_COLS
                dc[qq, kc] = int(np.clip(ka - qc + NAT_WIN_COLS - 1, 0, 2 * NAT_WIN_COLS - 2))
        return dc, ok

    def by_class(geometry, n):
        reps = {}
        for i in range(n):
            dx, ok = geometry(i)
            cls = _edge_class(i, n)
            if cls in reps:
                assert (np.where(ok, dx, -1) == np.where(reps[cls][1], reps[cls][0], -1)).all()
            else:
                reps[cls] = (dx, ok)
        filled = [reps.get(cls, reps[0]) for cls in range(3)]
        return np.stack([f[0] for f in filled]), np.stack([f[1] for f in filled])

    dr, row_ok = by_class(row_geometry, n_g)
    dc, col_ok = by_class(col_geometry, n_cb)
    row_sel = (dr[..., None] == np.arange(2 * NAT_WIN_ROWS - 1)).astype(np.float32)
    col_sel = (dc[..., None] == np.arange(2 * NAT_WIN_COLS - 1)).astype(np.float32)
    by_row = jnp.einsum('grka,hab->hgrkb', row_sel, rpb.astype(F32), precision=lax.Precision.HIGHEST)
    bias = jnp.einsum('hgrkb,jqcb->hgjrqkc', by_row, col_sel, precision=lax.Precision.HIGHEST)
    ok6 = row_ok[:, None, :, None, :, None] & col_ok[None, :, None, :, None, :]
    bias = jnp.where(ok6[None], bias, -jnp.inf)
    nq = NAT_QROWS * NAT_WIN_COLS
    nk = NAT_KROWS * NAT_KCOLS
    return bias.reshape(NAT_HEADS // 2, 2, 3, 3, nq, nk)


def _nat_kernel(q_ref, k_ref, v_ref, g_ref, bias_ref, out_ref):
    t = q_ref.shape[0]
    rows = t // GRID_W
    n_g = rows // NAT_QROWS
    n_cb = GRID_W // NAT_WIN_COLS
    nq = NAT_QROWS * NAT_WIN_COLS
    lane = lax.broadcasted_iota(jnp.int32, (1, 2 * NAT_HD), 1)
    head0 = lane < NAT_HD

    def group_body(g, carry):
        kb = jnp.clip(g * NAT_QROWS - NAT_WIN_ROWS // 2, 0, rows - NAT_KROWS)
        g_cls = jnp.where(g == 0, 0, jnp.where(g == n_g - 1, 2, 1))
        for j in range(n_cb):
            ks = _nat_kstart(j)

            def qrows(ref, rr):
                start = pl.multiple_of((g * NAT_QROWS + rr) * GRID_W + j * NAT_WIN_COLS, NAT_WIN_COLS)
                return ref[pl.ds(start, NAT_WIN_COLS), :]

            def krows(ref, kr):
                start = pl.multiple_of((kb + kr) * GRID_W + ks, SUBLANES)
                return ref[pl.ds(start, NAT_KCOLS), :]

            qs = jnp.concatenate([qrows(q_ref, rr) for rr in range(NAT_QROWS)], axis=0) * (NAT_HD ** -0.5)
            q2 = jnp.concatenate([jnp.where(head0, qs, 0.0), jnp.where(head0, 0.0, qs)], axis=0)
            kblk = jnp.concatenate([krows(k_ref, kr) for kr in range(NAT_KROWS)], axis=0)
            vblk = jnp.concatenate([krows(v_ref, kr) for kr in range(NAT_KROWS)], axis=0)
            s = _dot_nt(q2, kblk)
            j_cls = _edge_class(j, n_cb)
            s = s + jnp.concatenate([bias_ref[0, g_cls, j_cls], bias_ref[1, g_cls, j_cls]], axis=0)
            m = jnp.max(s, axis=-1, keepdims=True)
            p = jnp.exp(s - m)
            vext = jnp.concatenate([vblk, jnp.ones_like(vblk)], axis=1)
            o_ext = _dot(p, vext)
            o2 = o_ext[:, :2 * NAT_HD] / o_ext[:, 2 * NAT_HD:]
            o = jnp.where(head0, o2[:nq], o2[nq:])
            gs = jnp.concatenate([qrows(g_ref, rr) for rr in range(NAT_QROWS)], axis=0)
            res = (o * _silu(gs)).astype(out_ref.dtype)
            for rr in range(NAT_QROWS):
                start = pl.multiple_of((g * NAT_QROWS + rr) * GRID_W + j * NAT_WIN_COLS, NAT_WIN_COLS)
                out_ref[pl.ds(start, NAT_WIN_COLS), :] = res[rr * NAT_WIN_COLS:(rr + 1) * NAT_WIN_COLS]
        return carry

    lax.fori_loop(0, n_g, group_body, 0)


def _nat(z3, bias_tab):
    b, t, _ = z3.shape
    n_hp = NAT_HEADS // 2
    w = 2 * NAT_HD
    return pl.pallas_call(
        _nat_kernel,
        grid=(n_hp, b),
        in_specs=[
            _zspec(t, w, OFF_NQ, 1),
            _zspec(t, w, OFF_NK, 1),
            _zspec(t, w, OFF_NV, 1),
            _zspec(t, w, OFF_NG, 1),
            pl.BlockSpec((None,) + bias_tab.shape[1:], lambda h, i: (h, 0, 0, 0, 0, 0)),
        ],
        out_specs=pl.BlockSpec((None, t, w), lambda h, i: (i, 0, h)),
        out_shape=jax.ShapeDtypeStruct((b, t, BRANCH_W), BF16),
        compiler_params=_params("parallel", "parallel"),
        name="nat",
    )(z3, z3, z3, z3, bias_tab)


LRU_TC = 256


def _rglru_kernel(x_ref, g_ref, cw_ref, cb_ref, wg_ref, bg_ref, lam_ref, out_ref,
                  xp_scr, af_scr, bf_scr, ab_scr, bb_scr, hf_scr, hb_scr):
    t = x_ref.shape[0]
    tc = LRU_TC
    pad = SUBLANES
    w = LANES

    xp_scr[0:pad, :] = jnp.zeros((pad, w), F32)
    xp_scr[pad + t:pad + t + pad, :] = jnp.zeros((pad, w), F32)
    xp_scr[pad:pad + t, :] = x_ref[...]

    lam = lam_ref[...]
    neg = -lam
    softplus = jnp.maximum(neg, 0.0) + jnp.log(1.0 + jnp.exp(-jnp.abs(neg)))
    cw = cw_ref[...]
    cb = cb_ref[...]
    bg = bg_ref[...]

    def gate_body(n, carry):
        t0 = pl.multiple_of(n * tc, tc)
        xx = xp_scr[pl.ds(t0, tc + 2 * pad), :]
        total = tc + 2 * pad
        xc = cb
        for j in range(LRU_CONV):
            shift = LRU_CONV // 2 - j
            xs = xx if shift == 0 else pltpu.roll(xx, shift % total, 0)
            xc = xc + xs[pad:pad + tc] * cw[j:j + 1, :]
        gates = _dot(xc, wg_ref[...]) + bg
        sl = pl.ds(t0, tc)
        for d, (a_scr, b_scr) in enumerate(((af_scr, bf_scr), (ab_scr, bb_scr))):
            r = _sigmoid(gates[:, (2 * d) * w:(2 * d + 1) * w])
            i = _sigmoid(gates[:, (2 * d + 1) * w:(2 * d + 2) * w])
            log_a = -LRU_C * r * softplus[d:d + 1, :]
            a = jnp.exp(log_a)
            a_scr[sl, :] = a
            gap = 1.0 - a * a
            b_scr[sl, :] = gap * lax.rsqrt(jnp.maximum(gap, 1e-30)) * (i * xc)
        return carry

    lax.fori_loop(0, t // tc, gate_body, 0)

    n_tiles = t // SUBLANES
    row = lax.broadcasted_iota(jnp.int32, (SUBLANES, w), 0)

    def scan_body(i, carry):
        hf_prev, hb_next = carry
        sl = pl.ds(pl.multiple_of(i * SUBLANES, SUBLANES), SUBLANES)
        a = af_scr[sl, :]
        b = bf_scr[sl, :]
        for s in (1, 2, 4):
            keep = row >= s
            b = jnp.where(keep, a * pltpu.roll(b, s, 0) + b, b)
            a = jnp.where(keep, a * pltpu.roll(a, s, 0), a)
        hf_scr[sl, :] = a * hf_prev + b
        hf_prev = (jnp.broadcast_to(a[SUBLANES - 1:SUBLANES, :], (SUBLANES, w)) * hf_prev
                   + jnp.broadcast_to(b[SUBLANES - 1:SUBLANES, :], (SUBLANES, w)))
        slb = pl.ds(pl.multiple_of((n_tiles - 1 - i) * SUBLANES, SUBLANES), SUBLANES)
        a = ab_scr[slb, :]
        b = bb_scr[slb, :]
        for s in (1, 2, 4):
            keep = row < SUBLANES - s
            b = jnp.where(keep, a * pltpu.roll(b, SUBLANES - s, 0) + b, b)
            a = jnp.where(keep, a * pltpu.roll(a, SUBLANES - s, 0), a)
        hb_scr[slb, :] = a * hb_next + b
        hb_next = (jnp.broadcast_to(a[0:1, :], (SUBLANES, w)) * hb_next
                   + jnp.broadcast_to(b[0:1, :], (SUBLANES, w)))
        return hf_prev, hb_next

    zero = jnp.zeros((SUBLANES, w), F32)
    lax.fori_loop(0, n_tiles, scan_body, (zero, zero), unroll=4)

    def out_body(n, carry):
        sl = pl.ds(pl.multiple_of(n * tc, tc), tc)
        out_ref[sl, :] = ((hf_scr[sl, :] + hb_scr[sl, :]) * _silu(g_ref[sl, :])).astype(out_ref.dtype)
        return carry

    lax.fori_loop(0, t // tc, out_body, 0)


def _rglru(z3, conv_w, conv_b, wg, bg, lam):
    b, t, _ = z3.shape
    n_cb = BRANCH_W // LANES
    return pl.pallas_call(
        _rglru_kernel,
        grid=(b, n_cb),
        in_specs=[
            _zspec(t, LANES, OFF_LX, 0),
            _zspec(t, LANES, OFF_LG, 0),
            pl.BlockSpec((LRU_CONV, LANES), lambda i, h: (0, h)),
            pl.BlockSpec((1, LANES), lambda i, h: (0, h)),
            pl.BlockSpec((None, LANES, 4 * LANES), lambda i, h: (h, 0, 0)),
            pl.BlockSpec((None, 1, 4 * LANES), lambda i, h: (h, 0, 0)),
            pl.BlockSpec((2, LANES), lambda i, h: (0, h)),
        ],
        out_specs=pl.BlockSpec((None, t, LANES), lambda i, h: (i, 0, h)),
        out_shape=jax.ShapeDtypeStruct((b, t, BRANCH_W), BF16),
        scratch_shapes=[pltpu.VMEM((t + 2 * SUBLANES, LANES), F32)] + [pltpu.VMEM((t, LANES), F32)] * 6,
        compiler_params=_params("parallel", "parallel"),
        name="rglru",
    )(z3, z3, conv_w, conv_b.reshape(1, BRANCH_W), wg, bg, lam)


def _rglru_gate_weights(wa, ba, wx, bx):
    n_cb = BRANCH_W // LANES
    per = LANES // LRU_BW

    def blockdiag(w):
        w = w.reshape(n_cb, per, LRU_BW, LRU_BW)
        eye = jnp.eye(per, dtype=w.dtype)
        return jnp.einsum('cpjk,pq->cpjqk', w, eye).reshape(n_cb, LANES, LANES)

    wg = jnp.concatenate([blockdiag(wa[0]), blockdiag(wx[0]), blockdiag(wa[1]), blockdiag(wx[1])], axis=-1)
    bg = jnp.concatenate([ba[0].reshape(n_cb, 1, LANES), bx[0].reshape(n_cb, 1, LANES),
                          ba[1].reshape(n_cb, 1, LANES), bx[1].reshape(n_cb, 1, LANES)], axis=-1)
    return wg.astype(BF16), bg.astype(F32)


def _prefix_sum_rows(x):
    c, w = x.shape
    x3 = x.reshape(c // SUBLANES, SUBLANES, w)
    row = lax.broadcasted_iota(jnp.int32, (1, SUBLANES, w), 1)
    s = 1
    while s < SUBLANES:
        x3 = x3 + jnp.where(row >= s, pltpu.roll(x3, s, 1), 0.0)
        s *= 2
    x = x3.reshape(c, w)
    tile_tot = _block_row(x, SUBLANES, SUBLANES - 1)
    while s < c:
        shifted = jnp.concatenate([jnp.zeros((s, w), x.dtype), tile_tot[:c - s]], axis=0)
        x = x + shifted
        tile_tot = tile_tot + shifted
        s *= 2
    return x


def _block_row(x, size, pos):
    c, w = x.shape
    assert size % SUBLANES == 0
    x3 = x.reshape(c // size, size, w)
    return jnp.broadcast_to(x3[:, pos:pos + 1, :], x3.shape).reshape(c, w)


def _hgrn_forget(zf, lb):
    f = lb + (1.0 - lb) * _sigmoid(zf)
    return f, jnp.log2(f)


def _hgrn_tile_decays(size, f3, fb3, bs3, cs3, pos):
    half = size // 2
    if size == 2:
        return jnp.where(pos == 1, f3, 1.0), jnp.where(pos == 0, fb3, 1.0)
    if size == 4:
        f_prev, f_next = pltpu.roll(f3, 1, 1), pltpu.roll(f3, SUBLANES - 1, 1)
        fb_prev, fb_next = pltpu.roll(fb3, 1, 1), pltpu.roll(fb3, SUBLANES - 1, 1)
        e_f = jnp.where(pos == 0, f_next, jnp.where(pos == 1, 1.0, jnp.where(pos == 2, f3, f_prev * f3)))
        e_b = jnp.where(pos == 0, fb3 * fb_next, jnp.where(pos == 1, fb3, jnp.where(pos == 2, 1.0, fb_prev)))
        return e_f, e_b
    sign = jnp.where(pos >= half, 1.0, -1.0)
    e_f = jnp.exp2((bs3 - bs3[:, half - 1:half, :]) * sign)
    e_b = jnp.exp2((cs3[:, half:half + 1, :] - cs3) * sign)
    return e_f, e_b


def _hgrn_level_operands(size, q, kf, kb, f, fb, bs, cs):
    c, w = q.shape
    half = size // 2
    if size <= SUBLANES:
        tiled = lambda a: a.reshape(c // SUBLANES, SUBLANES, w)
        pos = lax.broadcasted_iota(jnp.int32, (1, SUBLANES, w), 1) % size
        upper = pos >= half
        e_f, e_b = _hgrn_tile_decays(size, tiled(f), tiled(fb), tiled(bs), tiled(cs), pos)
        z_f = e_f * jnp.where(upper, tiled(q), tiled(kf))
        z_b = e_b * jnp.where(upper, tiled(kb), tiled(q))
        x = jnp.concatenate([jnp.where(upper, z_f, 0.0), jnp.where(upper, 0.0, z_b)], axis=2)
        y = jnp.concatenate([z_f, z_b], axis=2)
        return x.reshape(c, 2 * w), y.reshape(c, 2 * w)
    xs, ys = [], []
    zero = jnp.zeros((half, w), F32)
    for i in range(c // half):
        rows = slice(i * half, (i + 1) * half)
        if i % 2 == 0:
            ref_f = bs[(i + 1) * half - 1:(i + 1) * half]
            ref_b = cs[(i + 1) * half:(i + 1) * half + 1]
            z_f = kf[rows] * jnp.exp2(ref_f - bs[rows])
            z_b = q[rows] * jnp.exp2(cs[rows] - ref_b)
            xs.append(jnp.concatenate([zero, z_b], axis=1))
        else:
            ref_f = bs[i * half - 1:i * half]
            ref_b = cs[i * half:i * half + 1]
            z_f = q[rows] * jnp.exp2(bs[rows] - ref_f)
            z_b = kb[rows] * jnp.exp2(ref_b - cs[rows])
            xs.append(jnp.concatenate([z_f, zero], axis=1))
        ys.append(jnp.concatenate([z_f, z_b], axis=1))
    return jnp.concatenate(xs, axis=0), jnp.concatenate(ys, axis=0)


def _hgrn_kernel(q_ref, ff_ref, fb_ref, v_ref, g_ref, lbl_ref, gain_ref, out_ref,
                 fb_scr, cs_scr, sb_scr, st_scr, *, layer):
    t = q_ref.shape[0]
    c = CHUNK
    n_chunks = t // c
    w = HGRN_DK

    logits = lbl_ref[...]
    mx = jnp.max(logits, axis=0)
    ex = jnp.exp(logits - mx[None])
    tot = jnp.sum(ex, axis=0)
    lb = jnp.zeros_like(tot)
    for i in range(1, layer + 1):
        lb = lb + ex[i] / tot
    lb_f = lb[0:1, :]
    lb_b = lb[1:2, :]

    pair_xor = lax.broadcasted_iota(jnp.int32, (c, c), 0) ^ lax.broadcasted_iota(jnp.int32, (c, c), 1)
    pair_level = jnp.zeros((c, c), jnp.int32)
    size = 2
    while size <= c:
        pair_level = jnp.where(pair_xor >= size // 2, size, pair_level)
        size *= 2

    st_scr[...] = jnp.zeros_like(st_scr)

    def bwd_body(i, carry):
        n = n_chunks - 1 - i
        sl = pl.ds(pl.multiple_of(n * c, c), c)
        fb, gb = _hgrn_forget(fb_ref[sl, :], lb_b)
        pre = _prefix_sum_rows(gb)
        total = pre[c - 1:c, :]
        cs = total - pre + gb
        fb_scr[sl, :] = fb
        cs_scr[sl, :] = cs
        sb_scr[n] = st_scr[...]
        st_scr[...] = st_scr[...] * jnp.exp2(total) + _dot_tn(v_ref[sl, :], (1.0 - fb) * jnp.exp2(total - cs))
        return carry

    lax.fori_loop(0, n_chunks, bwd_body, 0, unroll=2)

    st_scr[...] = jnp.zeros_like(st_scr)

    def fwd_body(n, carry):
        sl = pl.ds(pl.multiple_of(n * c, c), c)
        q = _silu(q_ref[sl, :])
        v = v_ref[sl, :]
        f, gf = _hgrn_forget(ff_ref[sl, :], lb_f)
        kf = 1.0 - f
        bs = _prefix_sum_rows(gf)
        fb = fb_scr[sl, :]
        kb = 1.0 - fb
        cs = cs_scr[sl, :]

        att = None
        size = 2
        while size <= c:
            x, y = _hgrn_level_operands(size, q, kf, kb, f, fb, bs, cs)
            att = jnp.where(pair_level == size, _dot_nt(x, y), 0.0 if att is None else att)
            size *= 2

        diag = jnp.sum(q * (kf + kb), axis=-1, keepdims=True)
        o = _dot(att, v) + diag * v
        inter = jnp.concatenate([q * jnp.exp2(bs), q * jnp.exp2(cs)], axis=1)
        states = jnp.concatenate([st_scr[...], sb_scr[n]], axis=1)
        o = o + _dot_nt(inter, states)
        last = bs[c - 1:c, :]
        st_scr[...] = st_scr[...] * jnp.exp2(last) + _dot_tn(v, kf * jnp.exp2(last - bs))
        o = o * lax.rsqrt(jnp.mean(o * o, axis=-1, keepdims=True) + EPS) * gain_ref[...]
        out_ref[sl, :] = (o * _silu(g_ref[sl, :])).astype(out_ref.dtype)
        return carry

    lax.fori_loop(0, n_chunks, fwd_body, 0, unroll=4)


def _hgrn(z3, lb_logits, gain, layer):
    b, t, _ = z3.shape
    depth = lb_logits.shape[0]
    w = HGRN_DK
    n_chunks = t // CHUNK
    return pl.pallas_call(
        functools.partial(_hgrn_kernel, layer=layer),
        grid=(b, HGRN_HEADS),
        in_specs=[
            _zspec(t, w, OFF_HQ, 0),
            _zspec(t, w, OFF_HFF, 0),
            _zspec(t, w, OFF_HFB, 0),
            _zspec(t, w, OFF_HI, 0),
            _zspec(t, w, OFF_HG, 0),
            pl.BlockSpec((depth, 2, w), lambda i, h: (0, 0, h)),
            pl.BlockSpec((1, w), lambda i, h: (0, h)),
        ],
        out_specs=pl.BlockSpec((None, t, w), lambda i, h: (i, 0, h)),
        out_shape=jax.ShapeDtypeStruct((b, t, BRANCH_W), BF16),
        scratch_shapes=[
            pltpu.VMEM((t, w), F32),
            pltpu.VMEM((t, w), F32),
            pltpu.VMEM((n_chunks, w, w), F32),
            pltpu.VMEM((w, w), F32),
        ],
        compiler_params=_params("parallel", "parallel"),
        name="hgrn",
    )(z3, z3, z3, z3, z3, lb_logits, gain.reshape(1, -1))


def _encoder(x, p, w):
    b, t, _ = x.shape
    depth = w['w_in'].shape[0]
    x2d = x.reshape(b * t, D_MODEL)
    for l in range(depth):
        z3 = _inproj(x2d, w['norm_mix'][l], w['w_in'][l]).reshape(b, t, W_IN)
        br_a = _retention(z3, w['cos'], w['sin'], w['lgq'][l], w['lgv'][l])
        br_b = _nat(z3, w['nat_bias'][l])
        br_c = _rglru(z3, w['conv_w'][l], w['conv_b'][l], w['lru_wg'][l], w['lru_bg'][l], w['lam'][l])
        br_d = _hgrn(z3, w['lb_logits'], w['hgrn_gain'][l], l)
        branches = [a.reshape(b * t, BRANCH_W) for a in (br_a, br_b, br_c, br_d)]
        x2d = _merge(x2d, branches, p[l].reshape(b * t, PLE_DIM), w['norm_mix'][l], w['w_merge'][l],
                     w['w_branch'][l], w['w_out'][l], w['ple_norm'][l], w['w_ple_gate'][l],
                     w['w_ple_proj'][l], w['final_norm'], l == depth - 1)
    return x2d.reshape(b, t, D_MODEL)


def kernel(x_prompt, x_sample, p_prompt, p_sample, norm_mix, w_in, ret_decay_logit, nat_rpb, lru_conv_w,
           lru_conv_b, lru_wa, lru_ba, lru_wx, lru_bx, lru_lambda, hgrn_lb_logits, hgrn_norm, w_branch,
           w_merge, w_out, ple_norm, w_ple_gate, w_ple_proj, final_norm):
    depth = w_in.shape[0]
    t = x_prompt.shape[1]
    rows = t // GRID_W
    cos_tab, sin_tab = _rotary_tables(t)
    gate_w = [_rglru_gate_weights(lru_wa[l], lru_ba[l], lru_wx[l], lru_bx[l]) for l in range(depth)]
    weights = {
        'norm_mix': norm_mix,
        'w_in': w_in.astype(BF16),
        'cos': cos_tab,
        'sin': sin_tab,
        'lgq': jnp.repeat(ret_decay_logit.astype(F32), RET_QK, axis=-1),
        'lgv': jnp.repeat(ret_decay_logit.astype(F32), RET_V, axis=-1),
        'nat_bias': [_nat_bias_tables(nat_rpb[l], rows) for l in range(depth)],
        'conv_w': lru_conv_w,
        'conv_b': lru_conv_b,
        'lru_wg': [g[0] for g in gate_w],
        'lru_bg': [g[1] for g in gate_w],
        'lam': lru_lambda,
        'lb_logits': hgrn_lb_logits,
        'hgrn_gain': hgrn_norm,
        'w_branch': w_branch.astype(BF16),
        'w_merge': w_merge.astype(BF16),
        'w_out': w_out.astype(BF16),
        'ple_norm': ple_norm,
        'w_ple_gate': w_ple_gate.astype(BF16),
        'w_ple_proj': w_ple_proj.astype(BF16),
        'final_norm': final_norm,
    }
    y_prompt = _encoder(x_prompt, p_prompt, weights)
    y_sample = _encoder(x_sample, p_sample, weights)
    return (y_prompt, y_sample)
```

```python
import functools

import numpy as np
import jax
import jax.numpy as jnp
from jax import lax
from jax.experimental import pallas as pl
from jax.experimental.pallas import tpu as pltpu

F32 = jnp.float32
BF16 = jnp.bfloat16

D_MODEL = 1024
PLE_DIM = 256
GRID_W = 64
N_BRANCH = 4
BRANCH_W = 512
RET_HEADS = 4
RET_QK = 64
RET_V = 128
ROPE_BASE = 10000.0
NAT_HEADS = 8
NAT_HD = 64
NAT_WIN_ROWS = 8
NAT_WIN_COLS = 16
LRU_BLOCKS = 8
LRU_BW = 64
LRU_CONV = 4
LRU_C = 8.0
HGRN_HEADS = 4
HGRN_DK = 128
EPS = 1e-6
W_IN = 7168

OFF_RQ, OFF_RK, OFF_RV, OFF_RG = 0, 256, 512, 1024
OFF_NQ, OFF_NK, OFF_NV, OFF_NG = 1536, 2048, 2560, 3072
OFF_LX, OFF_LG = 3584, 4096
OFF_HQ, OFF_HFF, OFF_HFB, OFF_HI, OFF_HG = 4608, 5120, 5632, 6144, 6656

LANES = 128
SUBLANES = 8
VMEM_LIMIT = 56 * 1024 * 1024

CHUNK = 128


def _params(*sem):
    return pltpu.CompilerParams(dimension_semantics=sem, vmem_limit_bytes=VMEM_LIMIT)


def _dot(a, b):
    return jnp.dot(a.astype(BF16), b.astype(BF16), preferred_element_type=F32)


def _dot_nt(a, b):
    return lax.dot_general(a.astype(BF16), b.astype(BF16), (((1,), (1,)), ((), ())),
                           preferred_element_type=F32)


def _dot_tn(a, b):
    return lax.dot_general(a.astype(BF16), b.astype(BF16), (((0,), (0,)), ((), ())),
                           preferred_element_type=F32)


def _rms(x, g):
    return x * lax.rsqrt(jnp.mean(x * x, axis=-1, keepdims=True) + EPS) * g


def _sigmoid(x):
    return jax.nn.sigmoid(x)


def _silu(x):
    return x * jax.nn.sigmoid(x)


def _zspec(t, width, off, grid_pos):
    base = off // width
    if grid_pos == 0:
        return pl.BlockSpec((None, t, width), lambda b, h: (b, 0, base + h))
    return pl.BlockSpec((None, t, width), lambda h, b: (b, 0, base + h))


def _inproj_kernel(x_ref, g_ref, w_ref, z_ref, h_scr):
    @pl.when(pl.program_id(1) == 0)
    def _():
        h_scr[...] = _rms(x_ref[...], g_ref[...]).astype(BF16)

    z_ref[...] = jnp.dot(h_scr[...], w_ref[...], preferred_element_type=F32)


def _inproj(x2d, g, w_bf16):
    m = x2d.shape[0]
    tm, tn = 1024, 1792
    return pl.pallas_call(
        _inproj_kernel,
        grid=(m // tm, W_IN // tn),
        in_specs=[
            pl.BlockSpec((tm, D_MODEL), lambda i, j: (i, 0)),
            pl.BlockSpec((1, D_MODEL), lambda i, j: (0, 0)),
            pl.BlockSpec((D_MODEL, tn), lambda i, j: (0, j)),
        ],
        out_specs=pl.BlockSpec((tm, tn), lambda i, j: (i, j)),
        out_shape=jax.ShapeDtypeStruct((m, W_IN), F32),
        scratch_shapes=[pltpu.VMEM((tm, D_MODEL), BF16)],
        compiler_params=_params("parallel", "arbitrary"),
        name="inproj",
    )(x2d, g.reshape(1, D_MODEL), w_bf16)


def _merge_kernel(x_ref, ba_ref, bb_ref, bc_ref, bd_ref, p_ref, gmix_ref, wmg_ref, wbr_ref, wo_ref,
                  gple_ref, wpg_ref, wpp_ref, gfin_ref, out_ref, *, final):
    x = x_ref[...]
    h = _rms(x, gmix_ref[...]).astype(BF16)
    merged = None
    for j, b_ref in enumerate((ba_ref, bb_ref, bc_ref, bd_ref)):
        gate = _sigmoid(jnp.dot(h, wmg_ref[j], preferred_element_type=F32))
        term = gate * jnp.dot(b_ref[...], wbr_ref[j], preferred_element_type=F32)
        merged = term if merged is None else merged + term
    x1 = x + _dot(merged, wo_ref[...])
    gate2 = _sigmoid(_dot(_rms(x1, gple_ref[...]), wpg_ref[...]))
    x2 = x1 + gate2 * _dot(p_ref[...], wpp_ref[...])
    if final:
        x2 = _rms(x2, gfin_ref[...])
    out_ref[...] = x2


def _merge(x2d, branches, p2d, gmix, wmg, wbr, wo, gple, wpg, wpp, gfin, final):
    m = x2d.shape[0]
    tm = 256
    row = lambda i: (i, 0)
    const2 = lambda i: (0, 0)
    const3 = lambda i: (0, 0, 0)
    vec = pl.BlockSpec((1, D_MODEL), const2)
    return pl.pallas_call(
        functools.partial(_merge_kernel, final=final),
        grid=(m // tm,),
        in_specs=[
            pl.BlockSpec((tm, D_MODEL), row),
            pl.BlockSpec((tm, BRANCH_W), row),
            pl.BlockSpec((tm, BRANCH_W), row),
            pl.BlockSpec((tm, BRANCH_W), row),
            pl.BlockSpec((tm, BRANCH_W), row),
            pl.BlockSpec((tm, PLE_DIM), row),
            vec,
            pl.BlockSpec((N_BRANCH, D_MODEL, D_MODEL), const3),
            pl.BlockSpec((N_BRANCH, BRANCH_W, D_MODEL), const3),
            pl.BlockSpec((D_MODEL, D_MODEL), const2),
            vec,
            pl.BlockSpec((D_MODEL, D_MODEL), const2),
            pl.BlockSpec((PLE_DIM, D_MODEL), const2),
            vec,
        ],
        out_specs=pl.BlockSpec((tm, D_MODEL), row),
        out_shape=jax.ShapeDtypeStruct((m, D_MODEL), F32),
        compiler_params=_params("parallel"),
        name="merge",
    )(x2d, *branches, p2d, gmix.reshape(1, -1), wmg, wbr, wo, gple.reshape(1, -1), wpg, wpp,
      gfin.reshape(1, -1))


def _retention_kernel(q_ref, k_ref, v_ref, g_ref, cos_ref, sin_ref, lgq_ref, lgv_ref, out_ref,
                      qr_scr, kr_scr, sb_scr, sf_scr):
    t = q_ref.shape[0]
    c = CHUNK
    n_chunks = t // c
    hd = RET_QK

    lane = lax.broadcasted_iota(jnp.int32, (1, 2 * hd), 1)
    first_half = (lane % hd) < (hd // 2)
    head0_q = lane < hd
    lane_v = lax.broadcasted_iota(jnp.int32, (1, 2 * RET_V), 1)
    head0_v = lane_v < RET_V

    def rotary(x, cos, sin):
        partner = jnp.where(first_half, pltpu.roll(x, 2 * hd - hd // 2, 1), pltpu.roll(x, hd // 2, 1))
        return x * cos + partner * sin

    def rot_body(n, carry):
        sl = pl.ds(pl.multiple_of(n * c, c), c)
        cos = cos_ref[sl, :]
        sin = sin_ref[sl, :]
        qr_scr[sl, :] = rotary(q_ref[sl, :], cos, sin)
        kr_scr[sl, :] = rotary(k_ref[sl, :], cos, sin) * (hd ** -0.5)
        return carry

    lax.fori_loop(0, n_chunks, rot_body, 0)

    lg_f = -jnp.log(1.0 + jnp.exp(-lgq_ref[0:1, :]))
    lg_b = -jnp.log(1.0 + jnp.exp(-lgq_ref[1:2, :]))
    lgv_f = -jnp.log(1.0 + jnp.exp(-lgv_ref[0:1, :]))
    lgv_b = -jnp.log(1.0 + jnp.exp(-lgv_ref[1:2, :]))
    tcol = lax.broadcasted_iota(jnp.int32, (c, 1), 0).astype(F32)
    head_f = jnp.exp((tcol + 1.0) * lg_f)
    head_b = jnp.exp((c - tcol) * lg_b)
    tail_f = jnp.exp((c - 1.0 - tcol) * lg_f)
    tail_b = jnp.exp(tcol * lg_b)
    dec_f = jnp.exp(c * lgv_f)
    dec_b = jnp.exp(c * lgv_b)
    rowk = lax.broadcasted_iota(jnp.int32, (2 * hd, 2 * RET_V), 0)
    colv = lax.broadcasted_iota(jnp.int32, (2 * hd, 2 * RET_V), 1)
    blockdiag = (rowk // hd) == (colv // RET_V)

    diff = (lax.broadcasted_iota(jnp.int32, (c, c), 0) - lax.broadcasted_iota(jnp.int32, (c, c), 1)).astype(F32)

    def decay_mask(lf, lb):
        fwd = jnp.exp(jnp.maximum(diff, 0.0) * lf)
        bwd = jnp.exp(jnp.maximum(-diff, 0.0) * lb)
        return jnp.where(diff > 0, fwd, jnp.where(diff < 0, bwd, 2.0))

    dmask = jnp.concatenate([decay_mask(lgv_f[:, :c], lgv_b[:, :c]),
                             decay_mask(lgv_f[:, RET_V:RET_V + c], lgv_b[:, RET_V:RET_V + c])], axis=1)

    sf_scr[...] = jnp.zeros_like(sf_scr)

    def bwd_body(i, carry):
        n = n_chunks - 1 - i
        sl = pl.ds(pl.multiple_of(n * c, c), c)
        sb_scr[n] = sf_scr[...]
        loc = _dot_tn(kr_scr[sl, :] * tail_b, v_ref[sl, :])
        sf_scr[...] = sf_scr[...] * dec_b + jnp.where(blockdiag, loc, 0.0)
        return carry

    lax.fori_loop(0, n_chunks, bwd_body, 0, unroll=2)

    sf_scr[...] = jnp.zeros_like(sf_scr)

    def fwd_body(n, carry):
        sl = pl.ds(pl.multiple_of(n * c, c), c)
        qn = qr_scr[sl, :]
        kn = kr_scr[sl, :]
        vn = v_ref[sl, :]
        kstack = jnp.concatenate([jnp.where(head0_q, kn, 0.0), jnp.where(head0_q, 0.0, kn)], axis=0)
        scores = _dot_nt(qn, kstack) * dmask
        vstack = jnp.concatenate([jnp.where(head0_v, vn, 0.0), jnp.where(head0_v, 0.0, vn)], axis=0)
        lhs = jnp.concatenate([scores, qn * head_f, qn * head_b], axis=1)
        rhs = jnp.concatenate([vstack, sf_scr[...], sb_scr[n]], axis=0)
        o = _dot(lhs, rhs)
        loc = _dot_tn(kn * tail_f, vn)
        sf_scr[...] = sf_scr[...] * dec_f + jnp.where(blockdiag, loc, 0.0)
        o0 = o[:, :RET_V]
        o1 = o[:, RET_V:]
        o0 = o0 * lax.rsqrt(jnp.mean(o0 * o0, axis=-1, keepdims=True) + EPS)
        o1 = o1 * lax.rsqrt(jnp.mean(o1 * o1, axis=-1, keepdims=True) + EPS)
        on = jnp.concatenate([o0, o1], axis=1)
        out_ref[sl, :] = (on * _silu(g_ref[sl, :])).astype(out_ref.dtype)
        return carry

    lax.fori_loop(0, n_chunks, fwd_body, 0, unroll=4)


def _retention(z3, cos_tab, sin_tab, lgq, lgv):
    b, t, _ = z3.shape
    n_chunks = t // CHUNK
    return pl.pallas_call(
        _retention_kernel,
        grid=(b, RET_HEADS // 2),
        in_specs=[
            _zspec(t, 2 * RET_QK, OFF_RQ, 0),
            _zspec(t, 2 * RET_QK, OFF_RK, 0),
            _zspec(t, 2 * RET_V, OFF_RV, 0),
            _zspec(t, 2 * RET_V, OFF_RG, 0),
            pl.BlockSpec((t, 2 * RET_QK), lambda i, h: (0, 0)),
            pl.BlockSpec((t, 2 * RET_QK), lambda i, h: (0, 0)),
            pl.BlockSpec((2, 2 * RET_QK), lambda i, h: (0, h)),
            pl.BlockSpec((2, 2 * RET_V), lambda i, h: (0, h)),
        ],
        out_specs=pl.BlockSpec((None, t, 2 * RET_V), lambda i, h: (i, 0, h)),
        out_shape=jax.ShapeDtypeStruct((b, t, BRANCH_W), BF16),
        scratch_shapes=[
            pltpu.VMEM((t, 2 * RET_QK), F32),
            pltpu.VMEM((t, 2 * RET_QK), F32),
            pltpu.VMEM((n_chunks, 2 * RET_QK, 2 * RET_V), F32),
            pltpu.VMEM((2 * RET_QK, 2 * RET_V), F32),
        ],
        compiler_params=_params("parallel", "parallel"),
        name="retention",
    )(z3, z3, z3, z3, cos_tab, sin_tab, lgq, lgv)


def _rotary_tables(t):
    half = RET_QK // 2
    inv = ROPE_BASE ** (-jnp.arange(half, dtype=F32) / half)
    ang = jnp.arange(t, dtype=F32)[:, None] * inv[None, :]
    cos = jnp.cos(ang)
    sin = jnp.sin(ang)
    cos_tab = jnp.tile(jnp.concatenate([cos, cos], axis=1), (1, 2))
    sin_tab = jnp.tile(jnp.concatenate([-sin, sin], axis=1), (1, 2))
    return cos_tab, sin_tab


NAT_QROWS = 8
NAT_KROWS = 16
NAT_KCOLS = 2 * NAT_WIN_COLS


def _nat_kstart(j):
    return int(np.clip(j * NAT_WIN_COLS - NAT_WIN_COLS // 2, 0, GRID_W - NAT_KCOLS))


def _nat_key_row_base(g, rows):
    return int(np.clip(g * NAT_QROWS - NAT_WIN_ROWS // 2, 0, rows - NAT_KROWS))


def _edge_class(i, n):
    return 0 if i == 0 else (2 if i == n - 1 else 1)


def _nat_bias_tables(rpb, rows):
    n_g = rows // NAT_QROWS
    n_cb = GRID_W // NAT_WIN_COLS

    def row_geometry(g):
        dr = np.zeros((NAT_QROWS, NAT_KROWS), np.int32)
        ok = np.zeros((NAT_QROWS, NAT_KROWS), bool)
        kb = _nat_key_row_base(g, rows)
        for rr in range(NAT_QROWS):
            r = g * NAT_QROWS + rr
            rs = int(np.clip(r - NAT_WIN_ROWS // 2, 0, rows - NAT_WIN_ROWS))
            for kr in range(NAT_KROWS):
                ka = kb + kr
                ok[rr, kr] = rs <= ka < rs + NAT_WIN_ROWS
                dr[rr, kr] = int(np.clip(ka - r + NAT_WIN_ROWS - 1, 0, 2 * NAT_WIN_ROWS - 2))
        return dr, ok

    def col_geometry(j):
        dc = np.zeros((NAT_WIN_COLS, NAT_KCOLS), np.int32)
        ok = np.zeros((NAT_WIN_COLS, NAT_KCOLS), bool)
        ks = _nat_kstart(j)
        for qq in range(NAT_WIN_COLS):
            qc = j * NAT_WIN_COLS + qq
            ws = int(np.clip(qc - NAT_WIN_COLS // 2, 0, GRID_W - NAT_WIN_COLS))
            for kc in range(NAT_KCOLS):
                ka = ks + kc
                ok[qq, kc] = ws <= ka < ws + NAT_WIN_COLS
                dc[qq, kc] = int(np.clip(ka - qc + NAT_WIN_COLS - 1, 0, 2 * NAT_WIN_COLS - 2))
        return dc, ok

    def by_class(geometry, n):
        reps = {}
        for i in range(n):
            dx, ok = geometry(i)
            cls = _edge_class(i, n)
            if cls in reps:
                assert (np.where(ok, dx, -1) == np.where(reps[cls][1], reps[cls][0], -1)).all()
            else:
                reps[cls] = (dx, ok)
        filled = [reps.get(cls, reps[0]) for cls in range(3)]
        return np.stack([f[0] for f in filled]), np.stack([f[1] for f in filled])

    dr, row_ok = by_class(row_geometry, n_g)
    dc, col_ok = by_class(col_geometry, n_cb)
    row_sel = (dr[..., None] == np.arange(2 * NAT_WIN_ROWS - 1)).astype(np.float32)
    col_sel = (dc[..., None] == np.arange(2 * NAT_WIN_COLS - 1)).astype(np.float32)
    by_row = jnp.einsum('grka,hab->hgrkb', row_sel, rpb.astype(F32), precision=lax.Precision.HIGHEST)
    bias = jnp.einsum('hgrkb,jqcb->hgjrqkc', by_row, col_sel, precision=lax.Precision.HIGHEST)
    ok6 = row_ok[:, None, :, None, :, None] & col_ok[None, :, None, :, None, :]
    bias = jnp.where(ok6[None], bias, -jnp.inf)
    nq = NAT_QROWS * NAT_WIN_COLS
    nk = NAT_KROWS * NAT_KCOLS
    return bias.reshape(NAT_HEADS // 2, 2, 3, 3, nq, nk)


def _nat_kernel(q_ref, k_ref, v_ref, g_ref, bias_ref, out_ref):
    t = q_ref.shape[0]
    rows = t // GRID_W
    n_g = rows // NAT_QROWS
    n_cb = GRID_W // NAT_WIN_COLS
    nq = NAT_QROWS * NAT_WIN_COLS
    lane = lax.broadcasted_iota(jnp.int32, (1, 2 * NAT_HD), 1)
    head0 = lane < NAT_HD

    def group_body(g, carry):
        kb = jnp.clip(g * NAT_QROWS - NAT_WIN_ROWS // 2, 0, rows - NAT_KROWS)
        g_cls = jnp.where(g == 0, 0, jnp.where(g == n_g - 1, 2, 1))
        for j in range(n_cb):
            ks = _nat_kstart(j)

            def qrows(ref, rr):
                start = pl.multiple_of((g * NAT_QROWS + rr) * GRID_W + j * NAT_WIN_COLS, NAT_WIN_COLS)
                return ref[pl.ds(start, NAT_WIN_COLS), :]

            def krows(ref, kr):
                start = pl.multiple_of((kb + kr) * GRID_W + ks, SUBLANES)
                return ref[pl.ds(start, NAT_KCOLS), :]

            qs = jnp.concatenate([qrows(q_ref, rr) for rr in range(NAT_QROWS)], axis=0) * (NAT_HD ** -0.5)
            q2 = jnp.concatenate([jnp.where(head0, qs, 0.0), jnp.where(head0, 0.0, qs)], axis=0)
            kblk = jnp.concatenate([krows(k_ref, kr) for kr in range(NAT_KROWS)], axis=0)
            vblk = jnp.concatenate([krows(v_ref, kr) for kr in range(NAT_KROWS)], axis=0)
            s = _dot_nt(q2, kblk)
            j_cls = _edge_class(j, n_cb)
            s = s + jnp.concatenate([bias_ref[0, g_cls, j_cls], bias_ref[1, g_cls, j_cls]], axis=0)
            m = jnp.max(s, axis=-1, keepdims=True)
            p = jnp.exp(s - m)
            vext = jnp.concatenate([vblk, jnp.ones_like(vblk)], axis=1)
            o_ext = _dot(p, vext)
            o2 = o_ext[:, :2 * NAT_HD] / o_ext[:, 2 * NAT_HD:]
            o = jnp.where(head0, o2[:nq], o2[nq:])
            gs = jnp.concatenate([qrows(g_ref, rr) for rr in range(NAT_QROWS)], axis=0)
            res = (o * _silu(gs)).astype(out_ref.dtype)
            for rr in range(NAT_QROWS):
                start = pl.multiple_of((g * NAT_QROWS + rr) * GRID_W + j * NAT_WIN_COLS, NAT_WIN_COLS)
                out_ref[pl.ds(start, NAT_WIN_COLS), :] = res[rr * NAT_WIN_COLS:(rr + 1) * NAT_WIN_COLS]
        return carry

    lax.fori_loop(0, n_g, group_body, 0, unroll=2)


def _nat(z3, bias_tab):
    b, t, _ = z3.shape
    n_hp = NAT_HEADS // 2
    w = 2 * NAT_HD
    return pl.pallas_call(
        _nat_kernel,
        grid=(n_hp, b),
        in_specs=[
            _zspec(t, w, OFF_NQ, 1),
            _zspec(t, w, OFF_NK, 1),
            _zspec(t, w, OFF_NV, 1),
            _zspec(t, w, OFF_NG, 1),
            pl.BlockSpec((None,) + bias_tab.shape[1:], lambda h, i: (h, 0, 0, 0, 0, 0)),
        ],
        out_specs=pl.BlockSpec((None, t, w), lambda h, i: (i, 0, h)),
        out_shape=jax.ShapeDtypeStruct((b, t, BRANCH_W), BF16),
        compiler_params=_params("parallel", "parallel"),
        name="nat",
    )(z3, z3, z3, z3, bias_tab)


def _rglru_kernel(x_ref, g_ref, cw_ref, cb_ref, wg_ref, bg_ref, lam_ref, out_ref,
                  xp_scr, af_scr, bf_scr, ab_scr, bb_scr, hf_scr, pf_scr, hb_scr, pb_scr):
    t = x_ref.shape[0]
    n_seg = SUBLANES
    tc = t // n_seg
    pad = SUBLANES
    w = LANES

    xp_scr[0:pad, :] = jnp.zeros((pad, w), F32)
    xp_scr[pad + t:pad + t + pad, :] = jnp.zeros((pad, w), F32)
    xp_scr[pad:pad + t, :] = x_ref[...]

    lam = lam_ref[...]
    neg = -lam
    softplus = jnp.maximum(neg, 0.0) + jnp.log(1.0 + jnp.exp(-jnp.abs(neg)))
    cw = cw_ref[...]
    cb = cb_ref[...]
    bg = bg_ref[...]

    def gate_body(n, carry):
        t0 = pl.multiple_of(n * tc, tc)
        xx = xp_scr[pl.ds(t0, tc + 2 * pad), :]
        total = tc + 2 * pad
        xc = cb
        for j in range(LRU_CONV):
            shift = LRU_CONV // 2 - j
            xs = xx if shift == 0 else pltpu.roll(xx, shift % total, 0)
            xc = xc + xs[pad:pad + tc] * cw[j:j + 1, :]
        gates = _dot(xc, wg_ref[...]) + bg
        seg_rows = pl.ds(n, tc, stride=n_seg)
        for d, (a_scr, b_scr) in enumerate(((af_scr, bf_scr), (ab_scr, bb_scr))):
            r = _sigmoid(gates[:, (2 * d) * w:(2 * d + 1) * w])
            i = _sigmoid(gates[:, (2 * d + 1) * w:(2 * d + 2) * w])
            log_a = -LRU_C * r * softplus[d:d + 1, :]
            a = jnp.exp(log_a)
            a_scr[seg_rows, :] = a
            gap = 1.0 - a * a
            b_scr[seg_rows, :] = gap * lax.rsqrt(jnp.maximum(gap, 1e-30)) * (i * xc)
        return carry

    lax.fori_loop(0, n_seg, gate_body, 0)

    def scan_body(i, carry):
        hf, pf, hb, pb = carry
        sl = pl.ds(pl.multiple_of(i * n_seg, n_seg), n_seg)
        a = af_scr[sl, :]
        hf = a * hf + bf_scr[sl, :]
        pf = a * pf
        hf_scr[sl, :] = hf
        pf_scr[sl, :] = pf
        slb = pl.ds(pl.multiple_of((tc - 1 - i) * n_seg, n_seg), n_seg)
        a = ab_scr[slb, :]
        hb = a * hb + bb_scr[slb, :]
        pb = a * pb
        hb_scr[slb, :] = hb
        pb_scr[slb, :] = pb
        return hf, pf, hb, pb

    zero = jnp.zeros((n_seg, w), F32)
    one = jnp.ones((n_seg, w), F32)
    hf, pf, hb, pb = lax.fori_loop(0, tc, scan_body, (zero, one, zero, one), unroll=8)

    row = lax.broadcasted_iota(jnp.int32, (n_seg, w), 0)
    for s in (1, 2, 4):
        keep = row >= s
        hf = jnp.where(keep, pf * pltpu.roll(hf, s, 0) + hf, hf)
        pf = jnp.where(keep, pf * pltpu.roll(pf, s, 0), pf)
        keep = row < n_seg - s
        hb = jnp.where(keep, pb * pltpu.roll(hb, n_seg - s, 0) + hb, hb)
        pb = jnp.where(keep, pb * pltpu.roll(pb, n_seg - s, 0), pb)
    carry_f = jnp.where(row >= 1, pltpu.roll(hf, 1, 0), 0.0)
    carry_b = jnp.where(row < n_seg - 1, pltpu.roll(hb, n_seg - 1, 0), 0.0)

    def fix_body(i, carry):
        sl = pl.ds(pl.multiple_of(i * n_seg, n_seg), n_seg)
        af_scr[sl, :] = (hf_scr[sl, :] + pf_scr[sl, :] * carry_f) + (hb_scr[sl, :] + pb_scr[sl, :] * carry_b)
        return carry

    lax.fori_loop(0, tc, fix_body, 0, unroll=8)

    def out_body(n, carry):
        h = af_scr[pl.ds(n, tc, stride=n_seg), :]
        sl = pl.ds(pl.multiple_of(n * tc, tc), tc)
        out_ref[sl, :] = (h * _silu(g_ref[sl, :])).astype(out_ref.dtype)
        return carry

    lax.fori_loop(0, n_seg, out_body, 0)


def _rglru(z3, conv_w, conv_b, wg, bg, lam):
    b, t, _ = z3.shape
    n_cb = BRANCH_W // LANES
    return pl.pallas_call(
        _rglru_kernel,
        grid=(b, n_cb),
        in_specs=[
            _zspec(t, LANES, OFF_LX, 0),
            _zspec(t, LANES, OFF_LG, 0),
            pl.BlockSpec((LRU_CONV, LANES), lambda i, h: (0, h)),
            pl.BlockSpec((1, LANES), lambda i, h: (0, h)),
            pl.BlockSpec((None, LANES, 4 * LANES), lambda i, h: (h, 0, 0)),
            pl.BlockSpec((None, 1, 4 * LANES), lambda i, h: (h, 0, 0)),
            pl.BlockSpec((2, LANES), lambda i, h: (0, h)),
        ],
        out_specs=pl.BlockSpec((None, t, LANES), lambda i, h: (i, 0, h)),
        out_shape=jax.ShapeDtypeStruct((b, t, BRANCH_W), BF16),
        scratch_shapes=[pltpu.VMEM((t + 2 * SUBLANES, LANES), F32)] + [pltpu.VMEM((t, LANES), F32)] * 8,
        compiler_params=_params("parallel", "parallel"),
        name="rglru",
    )(z3, z3, conv_w, conv_b.reshape(1, BRANCH_W), wg, bg, lam)


def _rglru_gate_weights(wa, ba, wx, bx):
    n_cb = BRANCH_W // LANES
    per = LANES // LRU_BW

    def blockdiag(w):
        w = w.reshape(n_cb, per, LRU_BW, LRU_BW)
        eye = jnp.eye(per, dtype=w.dtype)
        return jnp.einsum('cpjk,pq->cpjqk', w, eye).reshape(n_cb, LANES, LANES)

    wg = jnp.concatenate([blockdiag(wa[0]), blockdiag(wx[0]), blockdiag(wa[1]), blockdiag(wx[1])], axis=-1)
    bg = jnp.concatenate([ba[0].reshape(n_cb, 1, LANES), bx[0].reshape(n_cb, 1, LANES),
                          ba[1].reshape(n_cb, 1, LANES), bx[1].reshape(n_cb, 1, LANES)], axis=-1)
    return wg.astype(BF16), bg.astype(F32)


def _prefix_sum_rows(x):
    c, w = x.shape
    x3 = x.reshape(c // SUBLANES, SUBLANES, w)
    row = lax.broadcasted_iota(jnp.int32, (1, SUBLANES, w), 1)
    s = 1
    while s < SUBLANES:
        x3 = x3 + jnp.where(row >= s, pltpu.roll(x3, s, 1), 0.0)
        s *= 2
    x = x3.reshape(c, w)
    tile_tot = _block_row(x, SUBLANES, SUBLANES - 1)
    while s < c:
        shifted = jnp.concatenate([jnp.zeros((s, w), x.dtype), tile_tot[:c - s]], axis=0)
        x = x + shifted
        tile_tot = tile_tot + shifted
        s *= 2
    return x


def _block_row(x, size, pos):
    c, w = x.shape
    assert size % SUBLANES == 0
    x3 = x.reshape(c // size, size, w)
    return jnp.broadcast_to(x3[:, pos:pos + 1, :], x3.shape).reshape(c, w)


def _hgrn_forget(zf, lb):
    f = lb + (1.0 - lb) * _sigmoid(zf)
    return f, jnp.log2(f)


def _hgrn_tile_decays(size, f3, fb3, bs3, cs3, pos):
    half = size // 2
    if size == 2:
        return jnp.where(pos == 1, f3, 1.0), jnp.where(pos == 0, fb3, 1.0)
    if size == 4:
        f_prev, f_next = pltpu.roll(f3, 1, 1), pltpu.roll(f3, SUBLANES - 1, 1)
        fb_prev, fb_next = pltpu.roll(fb3, 1, 1), pltpu.roll(fb3, SUBLANES - 1, 1)
        e_f = jnp.where(pos == 0, f_next, jnp.where(pos == 1, 1.0, jnp.where(pos == 2, f3, f_prev * f3)))
        e_b = jnp.where(pos == 0, fb3 * fb_next, jnp.where(pos == 1, fb3, jnp.where(pos == 2, 1.0, fb_prev)))
        return e_f, e_b
    sign = jnp.where(pos >= half, 1.0, -1.0)
    e_f = jnp.exp2((bs3 - bs3[:, half - 1:half, :]) * sign)
    e_b = jnp.exp2((cs3[:, half:half + 1, :] - cs3) * sign)
    return e_f, e_b


def _hgrn_level_operands(size, q, kf, kb, f, fb, bs, cs):
    c, w = q.shape
    half = size // 2
    if size <= SUBLANES:
        tiled = lambda a: a.reshape(c // SUBLANES, SUBLANES, w)
        pos = lax.broadcasted_iota(jnp.int32, (1, SUBLANES, w), 1) % size
        upper = pos >= half
        e_f, e_b = _hgrn_tile_decays(size, tiled(f), tiled(fb), tiled(bs), tiled(cs), pos)
        z_f = e_f * jnp.where(upper, tiled(q), tiled(kf))
        z_b = e_b * jnp.where(upper, tiled(kb), tiled(q))
        x = jnp.concatenate([jnp.where(upper, z_f, 0.0), jnp.where(upper, 0.0, z_b)], axis=2)
        y = jnp.concatenate([z_f, z_b], axis=2)
        return x.reshape(c, 2 * w), y.reshape(c, 2 * w)
    xs, ys = [], []
    zero = jnp.zeros((half, w), F32)
    for i in range(c // half):
        rows = slice(i * half, (i + 1) * half)
        if i % 2 == 0:
            ref_f = bs[(i + 1) * half - 1:(i + 1) * half]
            ref_b = cs[(i + 1) * half:(i + 1) * half + 1]
            z_f = kf[rows] * jnp.exp2(ref_f - bs[rows])
            z_b = q[rows] * jnp.exp2(cs[rows] - ref_b)
            xs.append(jnp.concatenate([zero, z_b], axis=1))
        else:
            ref_f = bs[i * half - 1:i * half]
            ref_b = cs[i * half:i * half + 1]
            z_f = q[rows] * jnp.exp2(bs[rows] - ref_f)
            z_b = kb[rows] * jnp.exp2(ref_b - cs[rows])
            xs.append(jnp.concatenate([z_f, zero], axis=1))
        ys.append(jnp.concatenate([z_f, z_b], axis=1))
    return jnp.concatenate(xs, axis=0), jnp.concatenate(ys, axis=0)


def _hgrn_kernel(q_ref, ff_ref, fb_ref, v_ref, g_ref, lbl_ref, gain_ref, out_ref,
                 fb_scr, cs_scr, sb_scr, st_scr, *, layer):
    t = q_ref.shape[0]
    c = CHUNK
    n_chunks = t // c
    w = HGRN_DK

    logits = lbl_ref[...]
    mx = jnp.max(logits, axis=0)
    ex = jnp.exp(logits - mx[None])
    tot = jnp.sum(ex, axis=0)
    lb = jnp.zeros_like(tot)
    for i in range(1, layer + 1):
        lb = lb + ex[i] / tot
    lb_f = lb[0:1, :]
    lb_b = lb[1:2, :]

    pair_xor = lax.broadcasted_iota(jnp.int32, (c, c), 0) ^ lax.broadcasted_iota(jnp.int32, (c, c), 1)
    pair_level = jnp.zeros((c, c), jnp.int32)
    size = 2
    while size <= c:
        pair_level = jnp.where(pair_xor >= size // 2, size, pair_level)
        size *= 2

    st_scr[...] = jnp.zeros_like(st_scr)

    def bwd_body(i, carry):
        n = n_chunks - 1 - i
        sl = pl.ds(pl.multiple_of(n * c, c), c)
        fb, gb = _hgrn_forget(fb_ref[sl, :], lb_b)
        pre = _prefix_sum_rows(gb)
        total = pre[c - 1:c, :]
        cs = total - pre + gb
        fb_scr[sl, :] = fb
        cs_scr[sl, :] = cs
        sb_scr[n] = st_scr[...]
        st_scr[...] = st_scr[...] * jnp.exp2(total) + _dot_tn(v_ref[sl, :], (1.0 - fb) * jnp.exp2(total - cs))
        return carry

    lax.fori_loop(0, n_chunks, bwd_body, 0, unroll=2)

    st_scr[...] = jnp.zeros_like(st_scr)

    def fwd_body(n, carry):
        sl = pl.ds(pl.multiple_of(n * c, c), c)
        q = _silu(q_ref[sl, :])
        v = v_ref[sl, :]
        f, gf = _hgrn_forget(ff_ref[sl, :], lb_f)
        kf = 1.0 - f
        bs = _prefix_sum_rows(gf)
        fb = fb_scr[sl, :]
        kb = 1.0 - fb
        cs = cs_scr[sl, :]

        att = None
        size = 2
        while size <= c:
            x, y = _hgrn_level_operands(size, q, kf, kb, f, fb, bs, cs)
            att = jnp.where(pair_level == size, _dot_nt(x, y), 0.0 if att is None else att)
            size *= 2

        diag = jnp.sum(q * (kf + kb), axis=-1, keepdims=True)
        o = _dot(att, v) + diag * v
        inter = jnp.concatenate([q * jnp.exp2(bs), q * jnp.exp2(cs)], axis=1)
        states = jnp.concatenate([st_scr[...], sb_scr[n]], axis=1)
        o = o + _dot_nt(inter, states)
        last = bs[c - 1:c, :]
        st_scr[...] = st_scr[...] * jnp.exp2(last) + _dot_tn(v, kf * jnp.exp2(last - bs))
        o = o * lax.rsqrt(jnp.mean(o * o, axis=-1, keepdims=True) + EPS) * gain_ref[...]
        out_ref[sl, :] = (o * _silu(g_ref[sl, :])).astype(out_ref.dtype)
        return carry

    lax.fori_loop(0, n_chunks, fwd_body, 0, unroll=4)


def _hgrn(z3, lb_logits, gain, layer):
    b, t, _ = z3.shape
    depth = lb_logits.shape[0]
    w = HGRN_DK
    n_chunks = t // CHUNK
    return pl.pallas_call(
        functools.partial(_hgrn_kernel, layer=layer),
        grid=(b, HGRN_HEADS),
        in_specs=[
            _zspec(t, w, OFF_HQ, 0),
            _zspec(t, w, OFF_HFF, 0),
            _zspec(t, w, OFF_HFB, 0),
            _zspec(t, w, OFF_HI, 0),
            _zspec(t, w, OFF_HG, 0),
            pl.BlockSpec((depth, 2, w), lambda i, h: (0, 0, h)),
            pl.BlockSpec((1, w), lambda i, h: (0, h)),
        ],
        out_specs=pl.BlockSpec((None, t, w), lambda i, h: (i, 0, h)),
        out_shape=jax.ShapeDtypeStruct((b, t, BRANCH_W), BF16),
        scratch_shapes=[
            pltpu.VMEM((t, w), F32),
            pltpu.VMEM((t, w), F32),
            pltpu.VMEM((n_chunks, w, w), F32),
            pltpu.VMEM((w, w), F32),
        ],
        compiler_params=_params("parallel", "parallel"),
        name="hgrn",
    )(z3, z3, z3, z3, z3, lb_logits, gain.reshape(1, -1))


def _encoder(x, p, w):
    b, t, _ = x.shape
    depth = w['w_in'].shape[0]
    x2d = x.reshape(b * t, D_MODEL)
    for l in range(depth):
        z3 = _inproj(x2d, w['norm_mix'][l], w['w_in'][l]).reshape(b, t, W_IN)
        br_a = _retention(z3, w['cos'], w['sin'], w['lgq'][l], w['lgv'][l])
        br_b = _nat(z3, w['nat_bias'][l])
        br_c = _rglru(z3, w['conv_w'][l], w['conv_b'][l], w['lru_wg'][l], w['lru_bg'][l], w['lam'][l])
        br_d = _hgrn(z3, w['lb_logits'], w['hgrn_gain'][l], l)
        branches = [a.reshape(b * t, BRANCH_W) for a in (br_a, br_b, br_c, br_d)]
        x2d = _merge(x2d, branches, p[l].reshape(b * t, PLE_DIM), w['norm_mix'][l], w['w_merge'][l],
                     w['w_branch'][l], w['w_out'][l], w['ple_norm'][l], w['w_ple_gate'][l],
                     w['w_ple_proj'][l], w['final_norm'], l == depth - 1)
    return x2d.reshape(b, t, D_MODEL)


def kernel(x_prompt, x_sample, p_prompt, p_sample, norm_mix, w_in, ret_decay_logit, nat_rpb, lru_conv_w,
           lru_conv_b, lru_wa, lru_ba, lru_wx, lru_bx, lru_lambda, hgrn_lb_logits, hgrn_norm, w_branch,
           w_merge, w_out, ple_norm, w_ple_gate, w_ple_proj, final_norm):
    depth = w_in.shape[0]
    t = x_prompt.shape[1]
    rows = t // GRID_W
    cos_tab, sin_tab = _rotary_tables(t)
    gate_w = [_rglru_gate_weights(lru_wa[l], lru_ba[l], lru_wx[l], lru_bx[l]) for l in range(depth)]
    weights = {
        'norm_mix': norm_mix,
        'w_in': w_in.astype(BF16),
        'cos': cos_tab,
        'sin': sin_tab,
        'lgq': jnp.repeat(ret_decay_logit.astype(F32), RET_QK, axis=-1),
        'lgv': jnp.repeat(ret_decay_logit.astype(F32), RET_V, axis=-1),
        'nat_bias': [_nat_bias_tables(nat_rpb[l], rows) for l in range(depth)],
        'conv_w': lru_conv_w,
        'conv_b': lru_conv_b,
        'lru_wg': [g[0] for g in gate_w],
        'lru_bg': [g[1] for g in gate_w],
        'lam': lru_lambda,
        'lb_logits': hgrn_lb_logits,
        'hgrn_gain': hgrn_norm,
        'w_branch': w_branch.astype(BF16),
        'w_merge': w_merge.astype(BF16),
        'w_out': w_out.astype(BF16),
        'ple_norm': ple_norm,
        'w_ple_gate': w_ple_gate.astype(BF16),
        'w_ple_proj': w_ple_proj.astype(BF16),
        'final_norm': final_norm,
    }
    y_prompt = _encoder(x_prompt, p_prompt, weights)
    y_sample = _encoder(x_sample, p_sample, weights)
    return (y_prompt, y_sample)
```

```python
import functools

import numpy as np
import jax
import jax.numpy as jnp
from jax import lax
from jax.experimental import pallas as pl
from jax.experimental.pallas import tpu as pltpu

F32 = jnp.float32
BF16 = jnp.bfloat16

D_MODEL = 1024
PLE_DIM = 256
GRID_W = 64
N_BRANCH = 4
BRANCH_W = 512
RET_HEADS = 4
RET_QK = 64
RET_V = 128
ROPE_BASE = 10000.0
NAT_HEADS = 8
NAT_HD = 64
NAT_WIN_ROWS = 8
NAT_WIN_COLS = 16
LRU_BLOCKS = 8
LRU_BW = 64
LRU_CONV = 4
LRU_C = 8.0
HGRN_HEADS = 4
HGRN_DK = 128
EPS = 1e-6
W_IN = 7168

OFF_RQ, OFF_RK, OFF_RV, OFF_RG = 0, 256, 512, 1024
OFF_NQ, OFF_NK, OFF_NV, OFF_NG = 1536, 2048, 2560, 3072
OFF_LX, OFF_LG = 3584, 4096
OFF_HQ, OFF_HFF, OFF_HFB, OFF_HI, OFF_HG = 4608, 5120, 5632, 6144, 6656

LANES = 128
SUBLANES = 8
VMEM_LIMIT = 56 * 1024 * 1024

CHUNK = 128


def _params(*sem):
    return pltpu.CompilerParams(dimension_semantics=sem, vmem_limit_bytes=VMEM_LIMIT)


def _dot(a, b):
    return jnp.dot(a.astype(BF16), b.astype(BF16), preferred_element_type=F32)


def _dot_nt(a, b):
    return lax.dot_general(a.astype(BF16), b.astype(BF16), (((1,), (1,)), ((), ())),
                           preferred_element_type=F32)


def _dot_tn(a, b):
    return lax.dot_general(a.astype(BF16), b.astype(BF16), (((0,), (0,)), ((), ())),
                           preferred_element_type=F32)


def _rms(x, g):
    return x * lax.rsqrt(jnp.mean(x * x, axis=-1, keepdims=True) + EPS) * g


def _sigmoid(x):
    return jax.nn.sigmoid(x)


def _silu(x):
    return x * jax.nn.sigmoid(x)


def _zspec(t, width, off, grid_pos):
    base = off // width
    if grid_pos == 0:
        return pl.BlockSpec((None, t, width), lambda b, h: (b, 0, base + h))
    return pl.BlockSpec((None, t, width), lambda h, b: (b, 0, base + h))


def _inproj_kernel(x_ref, g_ref, w_ref, z_ref, h_scr):
    @pl.when(pl.program_id(1) == 0)
    def _():
        h_scr[...] = _rms(x_ref[...], g_ref[...]).astype(BF16)

    z_ref[...] = jnp.dot(h_scr[...], w_ref[...], preferred_element_type=F32)


def _inproj(x2d, g, w_bf16):
    m = x2d.shape[0]
    tm, tn = 1024, 1792
    return pl.pallas_call(
        _inproj_kernel,
        grid=(m // tm, W_IN // tn),
        in_specs=[
            pl.BlockSpec((tm, D_MODEL), lambda i, j: (i, 0)),
            pl.BlockSpec((1, D_MODEL), lambda i, j: (0, 0)),
            pl.BlockSpec((D_MODEL, tn), lambda i, j: (0, j)),
        ],
        out_specs=pl.BlockSpec((tm, tn), lambda i, j: (i, j)),
        out_shape=jax.ShapeDtypeStruct((m, W_IN), F32),
        scratch_shapes=[pltpu.VMEM((tm, D_MODEL), BF16)],
        compiler_params=_params("parallel", "arbitrary"),
        name="inproj",
    )(x2d, g.reshape(1, D_MODEL), w_bf16)


def _merge_kernel(x_ref, ba_ref, bb_ref, bc_ref, bd_ref, p_ref, gmix_ref, wmg_ref, wbr_ref, wo_ref,
                  gple_ref, wpg_ref, wpp_ref, gfin_ref, out_ref, *, final):
    x = x_ref[...]
    h = _rms(x, gmix_ref[...]).astype(BF16)
    merged = None
    for j, b_ref in enumerate((ba_ref, bb_ref, bc_ref, bd_ref)):
        gate = _sigmoid(jnp.dot(h, wmg_ref[j], preferred_element_type=F32))
        term = gate * jnp.dot(b_ref[...], wbr_ref[j], preferred_element_type=F32)
        merged = term if merged is None else merged + term
    x1 = x + _dot(merged, wo_ref[...])
    gate2 = _sigmoid(_dot(_rms(x1, gple_ref[...]), wpg_ref[...]))
    x2 = x1 + gate2 * _dot(p_ref[...], wpp_ref[...])
    if final:
        x2 = _rms(x2, gfin_ref[...])
    out_ref[...] = x2


def _merge(x2d, branches, p2d, gmix, wmg, wbr, wo, gple, wpg, wpp, gfin, final):
    m = x2d.shape[0]
    tm = 512
    row = lambda i: (i, 0)
    const2 = lambda i: (0, 0)
    const3 = lambda i: (0, 0, 0)
    once = pl.Buffered(1)
    vec = pl.BlockSpec((1, D_MODEL), const2)
    return pl.pallas_call(
        functools.partial(_merge_kernel, final=final),
        grid=(m // tm,),
        in_specs=[
            pl.BlockSpec((tm, D_MODEL), row),
            pl.BlockSpec((tm, BRANCH_W), row),
            pl.BlockSpec((tm, BRANCH_W), row),
            pl.BlockSpec((tm, BRANCH_W), row),
            pl.BlockSpec((tm, BRANCH_W), row),
            pl.BlockSpec((tm, PLE_DIM), row),
            vec,
            pl.BlockSpec((N_BRANCH, D_MODEL, D_MODEL), const3, pipeline_mode=once),
            pl.BlockSpec((N_BRANCH, BRANCH_W, D_MODEL), const3, pipeline_mode=once),
            pl.BlockSpec((D_MODEL, D_MODEL), const2, pipeline_mode=once),
            vec,
            pl.BlockSpec((D_MODEL, D_MODEL), const2, pipeline_mode=once),
            pl.BlockSpec((PLE_DIM, D_MODEL), const2, pipeline_mode=once),
            vec,
        ],
        out_specs=pl.BlockSpec((tm, D_MODEL), row),
        out_shape=jax.ShapeDtypeStruct((m, D_MODEL), F32),
        compiler_params=_params("parallel"),
        name="merge",
    )(x2d, *branches, p2d, gmix.reshape(1, -1), wmg, wbr, wo, gple.reshape(1, -1), wpg, wpp,
      gfin.reshape(1, -1))


def _retention_kernel(q_ref, k_ref, v_ref, g_ref, cos_ref, sin_ref, lgq_ref, lgv_ref, out_ref,
                      qr_scr, kr_scr, sb_scr, sf_scr):
    t = q_ref.shape[0]
    c = CHUNK
    n_chunks = t // c
    hd = RET_QK

    lane = lax.broadcasted_iota(jnp.int32, (1, 2 * hd), 1)
    head0_q = (lane // (hd // 2)) % 2 == 0
    lane_v = lax.broadcasted_iota(jnp.int32, (1, 2 * RET_V), 1)
    head0_v = lane_v < RET_V

    def rotary(x, cos, sin):
        return x * cos + pltpu.roll(x, hd, 1) * sin

    def rot_body(n, carry):
        sl = pl.ds(pl.multiple_of(n * c, c), c)
        cos = cos_ref[sl, :]
        sin = sin_ref[sl, :]
        qr_scr[sl, :] = rotary(q_ref[sl, :], cos, sin)
        kr_scr[sl, :] = rotary(k_ref[sl, :], cos, sin) * (hd ** -0.5)
        return carry

    lax.fori_loop(0, n_chunks, rot_body, 0, unroll=2)

    lg_f = -jnp.log(1.0 + jnp.exp(-lgq_ref[0:1, :]))
    lg_b = -jnp.log(1.0 + jnp.exp(-lgq_ref[1:2, :]))
    lgv_f = -jnp.log(1.0 + jnp.exp(-lgv_ref[0:1, :]))
    lgv_b = -jnp.log(1.0 + jnp.exp(-lgv_ref[1:2, :]))
    tcol = lax.broadcasted_iota(jnp.int32, (c, 1), 0).astype(F32)
    head_f = jnp.exp((tcol + 1.0) * lg_f)
    head_b = jnp.exp((c - tcol) * lg_b)
    tail_f = jnp.exp((c - 1.0 - tcol) * lg_f)
    tail_b = jnp.exp(tcol * lg_b)
    dec_f = jnp.exp(c * lgv_f)
    dec_b = jnp.exp(c * lgv_b)
    rowk = lax.broadcasted_iota(jnp.int32, (2 * hd, 2 * RET_V), 0)
    colv = lax.broadcasted_iota(jnp.int32, (2 * hd, 2 * RET_V), 1)
    blockdiag = ((rowk // (hd // 2)) % 2) == (colv // RET_V)

    diff = (lax.broadcasted_iota(jnp.int32, (c, c), 0) - lax.broadcasted_iota(jnp.int32, (c, c), 1)).astype(F32)

    def decay_mask(lf, lb):
        fwd = jnp.exp(jnp.maximum(diff, 0.0) * lf)
        bwd = jnp.exp(jnp.maximum(-diff, 0.0) * lb)
        return jnp.where(diff > 0, fwd, jnp.where(diff < 0, bwd, 2.0))

    dmask = jnp.concatenate([decay_mask(lgv_f[:, :c], lgv_b[:, :c]),
                             decay_mask(lgv_f[:, RET_V:RET_V + c], lgv_b[:, RET_V:RET_V + c])], axis=1)

    sf_scr[...] = jnp.zeros_like(sf_scr)

    def bwd_body(i, carry):
        n = n_chunks - 1 - i
        sl = pl.ds(pl.multiple_of(n * c, c), c)
        sb_scr[n] = sf_scr[...]
        loc = _dot_tn(kr_scr[sl, :] * tail_b, v_ref[sl, :])
        sf_scr[...] = sf_scr[...] * dec_b + jnp.where(blockdiag, loc, 0.0)
        return carry

    lax.fori_loop(0, n_chunks, bwd_body, 0, unroll=4)

    sf_scr[...] = jnp.zeros_like(sf_scr)

    def fwd_body(n, carry):
        sl = pl.ds(pl.multiple_of(n * c, c), c)
        qn = qr_scr[sl, :]
        kn = kr_scr[sl, :]
        vn = v_ref[sl, :]
        kstack = jnp.concatenate([jnp.where(head0_q, kn, 0.0), jnp.where(head0_q, 0.0, kn)], axis=0)
        scores = _dot_nt(qn, kstack) * dmask
        vstack = jnp.concatenate([jnp.where(head0_v, vn, 0.0), jnp.where(head0_v, 0.0, vn)], axis=0)
        lhs = jnp.concatenate([scores, qn * head_f, qn * head_b], axis=1)
        rhs = jnp.concatenate([vstack, sf_scr[...], sb_scr[n]], axis=0)
        o = _dot(lhs, rhs)
        loc = _dot_tn(kn * tail_f, vn)
        sf_scr[...] = sf_scr[...] * dec_f + jnp.where(blockdiag, loc, 0.0)
        o0 = o[:, :RET_V]
        o1 = o[:, RET_V:]
        o0 = o0 * lax.rsqrt(jnp.mean(o0 * o0, axis=-1, keepdims=True) + EPS)
        o1 = o1 * lax.rsqrt(jnp.mean(o1 * o1, axis=-1, keepdims=True) + EPS)
        on = jnp.concatenate([o0, o1], axis=1)
        out_ref[sl, :] = (on * _silu(g_ref[sl, :])).astype(out_ref.dtype)
        return carry

    lax.fori_loop(0, n_chunks, fwd_body, 0, unroll=4)


def _retention(z3, cos_tab, sin_tab, lgq, lgv):
    b, t, _ = z3.shape
    n_chunks = t // CHUNK
    return pl.pallas_call(
        _retention_kernel,
        grid=(b, RET_HEADS // 2),
        in_specs=[
            _zspec(t, 2 * RET_QK, OFF_RQ, 0),
            _zspec(t, 2 * RET_QK, OFF_RK, 0),
            _zspec(t, 2 * RET_V, OFF_RV, 0),
            _zspec(t, 2 * RET_V, OFF_RG, 0),
            pl.BlockSpec((t, 2 * RET_QK), lambda i, h: (0, 0)),
            pl.BlockSpec((t, 2 * RET_QK), lambda i, h: (0, 0)),
            pl.BlockSpec((2, 2 * RET_QK), lambda i, h: (0, h)),
            pl.BlockSpec((2, 2 * RET_V), lambda i, h: (0, h)),
        ],
        out_specs=pl.BlockSpec((None, t, 2 * RET_V), lambda i, h: (i, 0, h)),
        out_shape=jax.ShapeDtypeStruct((b, t, BRANCH_W), BF16),
        scratch_shapes=[
            pltpu.VMEM((t, 2 * RET_QK), F32),
            pltpu.VMEM((t, 2 * RET_QK), F32),
            pltpu.VMEM((n_chunks, 2 * RET_QK, 2 * RET_V), F32),
            pltpu.VMEM((2 * RET_QK, 2 * RET_V), F32),
        ],
        compiler_params=_params("parallel", "parallel"),
        name="retention",
    )(z3, z3, z3, z3, cos_tab, sin_tab, lgq, lgv)


def _rotary_tables(t):
    half = RET_QK // 2
    inv = ROPE_BASE ** (-jnp.arange(half, dtype=F32) / half)
    ang = jnp.arange(t, dtype=F32)[:, None] * inv[None, :]
    cos = jnp.cos(ang)
    sin = jnp.sin(ang)
    cos_tab = jnp.tile(cos, (1, 4))
    sin_tab = jnp.concatenate([-sin, -sin, sin, sin], axis=1)
    return cos_tab, sin_tab


def _pair_rotary_layout(a):
    lead = a.shape[:-1]
    a = a.reshape(lead + (RET_HEADS // 2, 2, 2, RET_QK // 2))
    return jnp.swapaxes(a, -3, -2).reshape(lead + (RET_HEADS * RET_QK,))


NAT_QROWS = 8
NAT_KROWS = 16
NAT_KCOLS = 2 * NAT_WIN_COLS


def _nat_kstart(j):
    return int(np.clip(j * NAT_WIN_COLS - NAT_WIN_COLS // 2, 0, GRID_W - NAT_KCOLS))


def _nat_key_row_base(g, rows):
    return int(np.clip(g * NAT_QROWS - NAT_WIN_ROWS // 2, 0, rows - NAT_KROWS))


def _edge_class(i, n):
    return 0 if i == 0 else (2 if i == n - 1 else 1)


def _nat_bias_tables(rpb, rows):
    n_g = rows // NAT_QROWS
    n_cb = GRID_W // NAT_WIN_COLS

    def row_geometry(g):
        dr = np.zeros((NAT_QROWS, NAT_KROWS), np.int32)
        ok = np.zeros((NAT_QROWS, NAT_KROWS), bool)
        kb = _nat_key_row_base(g, rows)
        for rr in range(NAT_QROWS):
            r = g * NAT_QROWS + rr
            rs = int(np.clip(r - NAT_WIN_ROWS // 2, 0, rows - NAT_WIN_ROWS))
            for kr in range(NAT_KROWS):
                ka = kb + kr
                ok[rr, kr] = rs <= ka < rs + NAT_WIN_ROWS
                dr[rr, kr] = int(np.clip(ka - r + NAT_WIN_ROWS - 1, 0, 2 * NAT_WIN_ROWS - 2))
        return dr, ok

    def col_geometry(j):
        dc = np.zeros((NAT_WIN_COLS, NAT_KCOLS), np.int32)
        ok = np.zeros((NAT_WIN_COLS, NAT_KCOLS), bool)
        ks = _nat_kstart(j)
        for qq in range(NAT_WIN_COLS):
            qc = j * NAT_WIN_COLS + qq
            ws = int(np.clip(qc - NAT_WIN_COLS // 2, 0, GRID_W - NAT_WIN_COLS))
            for kc in range(NAT_KCOLS):
                ka = ks + kc
                ok[qq, kc] = ws <= ka < ws + NAT_WIN_COLS
                dc[qq, kc] = int(np.clip(ka - qc + NAT_WIN_COLS - 1, 0, 2 * NAT_WIN_COLS - 2))
        return dc, ok

    def by_class(geometry, n):
        reps = {}
        for i in range(n):
            dx, ok = geometry(i)
            cls = _edge_class(i, n)
            if cls in reps:
                assert (np.where(ok, dx, -1) == np.where(reps[cls][1], reps[cls][0], -1)).all()
            else:
                reps[cls] = (dx, ok)
        filled = [reps.get(cls, reps[0]) for cls in range(3)]
        return np.stack([f[0] for f in filled]), np.stack([f[1] for f in filled])

    dr, row_ok = by_class(row_geometry, n_g)
    dc, col_ok = by_class(col_geometry, n_cb)
    row_sel = (dr[..., None] == np.arange(2 * NAT_WIN_ROWS - 1)).astype(np.float32)
    col_sel = (dc[..., None] == np.arange(2 * NAT_WIN_COLS - 1)).astype(np.float32)
    by_row = jnp.einsum('grka,hab->hgrkb', row_sel, rpb.astype(F32), precision=lax.Precision.HIGHEST)
    bias = jnp.einsum('hgrkb,jqcb->hgjrqkc', by_row, col_sel, precision=lax.Precision.HIGHEST)
    ok6 = row_ok[:, None, :, None, :, None] & col_ok[None, :, None, :, None, :]
    bias = jnp.where(ok6[None], bias, -jnp.inf)
    nq = NAT_QROWS * NAT_WIN_COLS
    nk = NAT_KROWS * NAT_KCOLS
    return bias.reshape(NAT_HEADS // 2, 2, 3, 3, nq, nk)


def _nat_kernel(q_ref, k_ref, v_ref, g_ref, bias_ref, out_ref):
    t = q_ref.shape[0]
    rows = t // GRID_W
    n_g = rows // NAT_QROWS
    n_cb = GRID_W // NAT_WIN_COLS
    nq = NAT_QROWS * NAT_WIN_COLS
    lane = lax.broadcasted_iota(jnp.int32, (1, 2 * NAT_HD), 1)
    head0 = lane < NAT_HD

    def group_body(g, carry):
        kb = jnp.clip(g * NAT_QROWS - NAT_WIN_ROWS // 2, 0, rows - NAT_KROWS)
        g_cls = jnp.where(g == 0, 0, jnp.where(g == n_g - 1, 2, 1))
        for j in range(n_cb):
            ks = _nat_kstart(j)

            def qrows(ref, rr):
                start = pl.multiple_of((g * NAT_QROWS + rr) * GRID_W + j * NAT_WIN_COLS, NAT_WIN_COLS)
                return ref[pl.ds(start, NAT_WIN_COLS), :]

            def krows(ref, kr):
                start = pl.multiple_of((kb + kr) * GRID_W + ks, SUBLANES)
                return ref[pl.ds(start, NAT_KCOLS), :]

            qs = jnp.concatenate([qrows(q_ref, rr) for rr in range(NAT_QROWS)], axis=0) * (NAT_HD ** -0.5)
            q2 = jnp.concatenate([jnp.where(head0, qs, 0.0), jnp.where(head0, 0.0, qs)], axis=0)
            kblk = jnp.concatenate([krows(k_ref, kr) for kr in range(NAT_KROWS)], axis=0)
            vblk = jnp.concatenate([krows(v_ref, kr) for kr in range(NAT_KROWS)], axis=0)
            s = _dot_nt(q2, kblk)
            j_cls = _edge_class(j, n_cb)
            s = s + jnp.concatenate([bias_ref[0, g_cls, j_cls], bias_ref[1, g_cls, j_cls]], axis=0)
            m = jnp.max(s, axis=-1, keepdims=True)
            p = jnp.exp(s - m)
            vext = jnp.concatenate([vblk, jnp.ones_like(vblk)], axis=1)
            o_ext = _dot(p, vext)
            o2 = o_ext[:, :2 * NAT_HD] / o_ext[:, 2 * NAT_HD:]
            o = jnp.where(head0, o2[:nq], o2[nq:])
            gs = jnp.concatenate([qrows(g_ref, rr) for rr in range(NAT_QROWS)], axis=0)
            res = (o * _silu(gs)).astype(out_ref.dtype)
            for rr in range(NAT_QROWS):
                start = pl.multiple_of((g * NAT_QROWS + rr) * GRID_W + j * NAT_WIN_COLS, NAT_WIN_COLS)
                out_ref[pl.ds(start, NAT_WIN_COLS), :] = res[rr * NAT_WIN_COLS:(rr + 1) * NAT_WIN_COLS]
        return carry

    lax.fori_loop(0, n_g, group_body, 0, unroll=2)


def _nat(z3, bias_tab):
    b, t, _ = z3.shape
    n_hp = NAT_HEADS // 2
    w = 2 * NAT_HD
    return pl.pallas_call(
        _nat_kernel,
        grid=(n_hp, b),
        in_specs=[
            _zspec(t, w, OFF_NQ, 1),
            _zspec(t, w, OFF_NK, 1),
            _zspec(t, w, OFF_NV, 1),
            _zspec(t, w, OFF_NG, 1),
            pl.BlockSpec((None,) + bias_tab.shape[1:], lambda h, i: (h, 0, 0, 0, 0, 0)),
        ],
        out_specs=pl.BlockSpec((None, t, w), lambda h, i: (i, 0, h)),
        out_shape=jax.ShapeDtypeStruct((b, t, BRANCH_W), BF16),
        compiler_params=_params("parallel", "parallel"),
        name="nat",
    )(z3, z3, z3, z3, bias_tab)


def _rglru_kernel(x_ref, g_ref, cw_ref, cb_ref, wg_ref, bg_ref, lam_ref, out_ref,
                  xp_scr, af_scr, bf_scr, ab_scr, bb_scr, hf_scr, pf_scr, hb_scr, pb_scr):
    t = x_ref.shape[0]
    n_seg = SUBLANES
    tc = t // n_seg
    pad = SUBLANES
    w = LANES

    xp_scr[0:pad, :] = jnp.zeros((pad, w), F32)
    xp_scr[pad + t:pad + t + pad, :] = jnp.zeros((pad, w), F32)
    xp_scr[pad:pad + t, :] = x_ref[...]

    lam = lam_ref[...]
    neg = -lam
    softplus = jnp.maximum(neg, 0.0) + jnp.log(1.0 + jnp.exp(-jnp.abs(neg)))
    cw = cw_ref[...]
    cb = cb_ref[...]
    bg = bg_ref[...]

    def gate_body(n, carry):
        t0 = pl.multiple_of(n * tc, tc)
        xx = xp_scr[pl.ds(t0, tc + 2 * pad), :]
        total = tc + 2 * pad
        xc = cb
        for j in range(LRU_CONV):
            shift = LRU_CONV // 2 - j
            xs = xx if shift == 0 else pltpu.roll(xx, shift % total, 0)
            xc = xc + xs[pad:pad + tc] * cw[j:j + 1, :]
        gates = _dot(xc, wg_ref[...]) + bg
        seg_rows = pl.ds(n, tc, stride=n_seg)
        for d, (a_scr, b_scr) in enumerate(((af_scr, bf_scr), (ab_scr, bb_scr))):
            r = _sigmoid(gates[:, (2 * d) * w:(2 * d + 1) * w])
            i = _sigmoid(gates[:, (2 * d + 1) * w:(2 * d + 2) * w])
            log_a = -LRU_C * r * softplus[d:d + 1, :]
            a = jnp.exp(log_a)
            a_scr[seg_rows, :] = a
            gap = 1.0 - a * a
            b_scr[seg_rows, :] = gap * lax.rsqrt(jnp.maximum(gap, 1e-30)) * (i * xc)
        return carry

    lax.fori_loop(0, n_seg, gate_body, 0)

    def scan_body(i, carry):
        hf, pf, hb, pb = carry
        sl = pl.ds(pl.multiple_of(i * n_seg, n_seg), n_seg)
        a = af_scr[sl, :]
        hf = a * hf + bf_scr[sl, :]
        pf = a * pf
        hf_scr[sl, :] = hf
        pf_scr[sl, :] = pf
        slb = pl.ds(pl.multiple_of((tc - 1 - i) * n_seg, n_seg), n_seg)
        a = ab_scr[slb, :]
        hb = a * hb + bb_scr[slb, :]
        pb = a * pb
        hb_scr[slb, :] = hb
        pb_scr[slb, :] = pb
        return hf, pf, hb, pb

    zero = jnp.zeros((n_seg, w), F32)
    one = jnp.ones((n_seg, w), F32)
    hf, pf, hb, pb = lax.fori_loop(0, tc, scan_body, (zero, one, zero, one), unroll=8)

    row = lax.broadcasted_iota(jnp.int32, (n_seg, w), 0)
    for s in (1, 2, 4):
        keep = row >= s
        hf = jnp.where(keep, pf * pltpu.roll(hf, s, 0) + hf, hf)
        pf = jnp.where(keep, pf * pltpu.roll(pf, s, 0), pf)
        keep = row < n_seg - s
        hb = jnp.where(keep, pb * pltpu.roll(hb, n_seg - s, 0) + hb, hb)
        pb = jnp.where(keep, pb * pltpu.roll(pb, n_seg - s, 0), pb)
    carry_f = jnp.where(row >= 1, pltpu.roll(hf, 1, 0), 0.0)
    carry_b = jnp.where(row < n_seg - 1, pltpu.roll(hb, n_seg - 1, 0), 0.0)

    def fix_body(i, carry):
        sl = pl.ds(pl.multiple_of(i * n_seg, n_seg), n_seg)
        af_scr[sl, :] = (hf_scr[sl, :] + pf_scr[sl, :] * carry_f) + (hb_scr[sl, :] + pb_scr[sl, :] * carry_b)
        return carry

    lax.fori_loop(0, tc, fix_body, 0, unroll=8)

    def out_body(n, carry):
        h = af_scr[pl.ds(n, tc, stride=n_seg), :]
        sl = pl.ds(pl.multiple_of(n * tc, tc), tc)
        out_ref[sl, :] = (h * _silu(g_ref[sl, :])).astype(out_ref.dtype)
        return carry

    lax.fori_loop(0, n_seg, out_body, 0)


def _rglru(z3, conv_w, conv_b, wg, bg, lam):
    b, t, _ = z3.shape
    n_cb = BRANCH_W // LANES
    return pl.pallas_call(
        _rglru_kernel,
        grid=(b, n_cb),
        in_specs=[
            _zspec(t, LANES, OFF_LX, 0),
            _zspec(t, LANES, OFF_LG, 0),
            pl.BlockSpec((LRU_CONV, LANES), lambda i, h: (0, h)),
            pl.BlockSpec((1, LANES), lambda i, h: (0, h)),
            pl.BlockSpec((None, LANES, 4 * LANES), lambda i, h: (h, 0, 0)),
            pl.BlockSpec((None, 1, 4 * LANES), lambda i, h: (h, 0, 0)),
            pl.BlockSpec((2, LANES), lambda i, h: (0, h)),
        ],
        out_specs=pl.BlockSpec((None, t, LANES), lambda i, h: (i, 0, h)),
        out_shape=jax.ShapeDtypeStruct((b, t, BRANCH_W), BF16),
        scratch_shapes=[pltpu.VMEM((t + 2 * SUBLANES, LANES), F32)] + [pltpu.VMEM((t, LANES), F32)] * 8,
        compiler_params=_params("parallel", "parallel"),
        name="rglru",
    )(z3, z3, conv_w, conv_b.reshape(1, BRANCH_W), wg, bg, lam)


def _rglru_gate_weights(wa, ba, wx, bx):
    n_cb = BRANCH_W // LANES
    per = LANES // LRU_BW

    def blockdiag(w):
        w = w.reshape(n_cb, per, LRU_BW, LRU_BW)
        eye = jnp.eye(per, dtype=w.dtype)
        return jnp.einsum('cpjk,pq->cpjqk', w, eye).reshape(n_cb, LANES, LANES)

    wg = jnp.concatenate([blockdiag(wa[0]), blockdiag(wx[0]), blockdiag(wa[1]), blockdiag(wx[1])], axis=-1)
    bg = jnp.concatenate([ba[0].reshape(n_cb, 1, LANES), bx[0].reshape(n_cb, 1, LANES),
                          ba[1].reshape(n_cb, 1, LANES), bx[1].reshape(n_cb, 1, LANES)], axis=-1)
    return wg.astype(BF16), bg.astype(F32)


def _prefix_sum_rows(x):
    c, w = x.shape
    x3 = x.reshape(c // SUBLANES, SUBLANES, w)
    row = lax.broadcasted_iota(jnp.int32, (1, SUBLANES, w), 1)
    s = 1
    while s < SUBLANES:
        x3 = x3 + jnp.where(row >= s, pltpu.roll(x3, s, 1), 0.0)
        s *= 2
    x = x3.reshape(c, w)
    tile_tot = _block_row(x, SUBLANES, SUBLANES - 1)
    while s < c:
        shifted = jnp.concatenate([jnp.zeros((s, w), x.dtype), tile_tot[:c - s]], axis=0)
        x = x + shifted
        tile_tot = tile_tot + shifted
        s *= 2
    return x


def _block_row(x, size, pos):
    c, w = x.shape
    assert size % SUBLANES == 0
    x3 = x.reshape(c // size, size, w)
    return jnp.broadcast_to(x3[:, pos:pos + 1, :], x3.shape).reshape(c, w)


def _hgrn_forget(zf, lb):
    f = lb + (1.0 - lb) * _sigmoid(zf)
    return f, jnp.log2(f)


def _hgrn_tile_decays(size, f3, fb3, bs3, cs3, pos):
    half = size // 2
    if size == 2:
        return jnp.where(pos == 1, f3, 1.0), jnp.where(pos == 0, fb3, 1.0)
    if size == 4:
        f_prev, f_next = pltpu.roll(f3, 1, 1), pltpu.roll(f3, SUBLANES - 1, 1)
        fb_prev, fb_next = pltpu.roll(fb3, 1, 1), pltpu.roll(fb3, SUBLANES - 1, 1)
        e_f = jnp.where(pos == 0, f_next, jnp.where(pos == 1, 1.0, jnp.where(pos == 2, f3, f_prev * f3)))
        e_b = jnp.where(pos == 0, fb3 * fb_next, jnp.where(pos == 1, fb3, jnp.where(pos == 2, 1.0, fb_prev)))
        return e_f, e_b
    sign = jnp.where(pos >= half, 1.0, -1.0)
    e_f = jnp.exp2((bs3 - bs3[:, half - 1:half, :]) * sign)
    e_b = jnp.exp2((cs3[:, half:half + 1, :] - cs3) * sign)
    return e_f, e_b


def _hgrn_level_operands(size, q, kf, kb, f, fb, bs, cs):
    c, w = q.shape
    half = size // 2
    if size <= SUBLANES:
        tiled = lambda a: a.reshape(c // SUBLANES, SUBLANES, w)
        pos = lax.broadcasted_iota(jnp.int32, (1, SUBLANES, w), 1) % size
        upper = pos >= half
        e_f, e_b = _hgrn_tile_decays(size, tiled(f), tiled(fb), tiled(bs), tiled(cs), pos)
        z_f = e_f * jnp.where(upper, tiled(q), tiled(kf))
        z_b = e_b * jnp.where(upper, tiled(kb), tiled(q))
        x = jnp.concatenate([jnp.where(upper, z_f, 0.0), jnp.where(upper, 0.0, z_b)], axis=2)
        y = jnp.concatenate([z_f, z_b], axis=2)
        return x.reshape(c, 2 * w), y.reshape(c, 2 * w)
    xs, ys = [], []
    zero = jnp.zeros((half, w), F32)
    for i in range(c // half):
        rows = slice(i * half, (i + 1) * half)
        if i % 2 == 0:
            ref_f = bs[(i + 1) * half - 1:(i + 1) * half]
            ref_b = cs[(i + 1) * half:(i + 1) * half + 1]
            z_f = kf[rows] * jnp.exp2(ref_f - bs[rows])
            z_b = q[rows] * jnp.exp2(cs[rows] - ref_b)
            xs.append(jnp.concatenate([zero, z_b], axis=1))
        else:
            ref_f = bs[i * half - 1:i * half]
            ref_b = cs[i * half:i * half + 1]
            z_f = q[rows] * jnp.exp2(bs[rows] - ref_f)
            z_b = kb[rows] * jnp.exp2(ref_b - cs[rows])
            xs.append(jnp.concatenate([z_f, zero], axis=1))
        ys.append(jnp.concatenate([z_f, z_b], axis=1))
    return jnp.concatenate(xs, axis=0), jnp.concatenate(ys, axis=0)


def _hgrn_kernel(q_ref, ff_ref, fb_ref, v_ref, g_ref, lbl_ref, gain_ref, out_ref,
                 fb_scr, cs_scr, sb_scr, st_scr, *, layer):
    t = q_ref.shape[0]
    c = CHUNK
    n_chunks = t // c
    w = HGRN_DK

    logits = lbl_ref[...]
    mx = jnp.max(logits, axis=0)
    ex = jnp.exp(logits - mx[None])
    tot = jnp.sum(ex, axis=0)
    lb = jnp.zeros_like(tot)
    for i in range(1, layer + 1):
        lb = lb + ex[i] / tot
    lb_f = lb[0:1, :]
    lb_b = lb[1:2, :]

    pair_xor = lax.broadcasted_iota(jnp.int32, (c, c), 0) ^ lax.broadcasted_iota(jnp.int32, (c, c), 1)
    pair_level = jnp.zeros((c, c), jnp.int32)
    size = 2
    while size <= c:
        pair_level = jnp.where(pair_xor >= size // 2, size, pair_level)
        size *= 2

    st_scr[...] = jnp.zeros_like(st_scr)

    def bwd_body(i, carry):
        n = n_chunks - 1 - i
        sl = pl.ds(pl.multiple_of(n * c, c), c)
        fb, gb = _hgrn_forget(fb_ref[sl, :], lb_b)
        pre = _prefix_sum_rows(gb)
        total = pre[c - 1:c, :]
        cs = total - pre + gb
        fb_scr[sl, :] = fb
        cs_scr[sl, :] = cs
        sb_scr[n] = st_scr[...]
        st_scr[...] = st_scr[...] * jnp.exp2(total) + _dot_tn(v_ref[sl, :], (1.0 - fb) * jnp.exp2(total - cs))
        return carry

    lax.fori_loop(0, n_chunks, bwd_body, 0, unroll=4)

    st_scr[...] = jnp.zeros_like(st_scr)

    def fwd_body(n, carry):
        sl = pl.ds(pl.multiple_of(n * c, c), c)
        q = _silu(q_ref[sl, :])
        v = v_ref[sl, :]
        f, gf = _hgrn_forget(ff_ref[sl, :], lb_f)
        kf = 1.0 - f
        bs = _prefix_sum_rows(gf)
        fb = fb_scr[sl, :]
        kb = 1.0 - fb
        cs = cs_scr[sl, :]

        att = None
        size = 2
        while size <= c:
            x, y = _hgrn_level_operands(size, q, kf, kb, f, fb, bs, cs)
            att = jnp.where(pair_level == size, _dot_nt(x, y), 0.0 if att is None else att)
            size *= 2

        diag = jnp.sum(q * (kf + kb), axis=-1, keepdims=True)
        o = _dot(att, v) + diag * v
        inter = jnp.concatenate([q * jnp.exp2(bs), q * jnp.exp2(cs)], axis=1)
        states = jnp.concatenate([st_scr[...], sb_scr[n]], axis=1)
        o = o + _dot_nt(inter, states)
        last = bs[c - 1:c, :]
        st_scr[...] = st_scr[...] * jnp.exp2(last) + _dot_tn(v, kf * jnp.exp2(last - bs))
        o = o * lax.rsqrt(jnp.mean(o * o, axis=-1, keepdims=True) + EPS) * gain_ref[...]
        out_ref[sl, :] = (o * _silu(g_ref[sl, :])).astype(out_ref.dtype)
        return carry

    lax.fori_loop(0, n_chunks, fwd_body, 0, unroll=4)


def _hgrn(z3, lb_logits, gain, layer):
    b, t, _ = z3.shape
    depth = lb_logits.shape[0]
    w = HGRN_DK
    n_chunks = t // CHUNK
    return pl.pallas_call(
        functools.partial(_hgrn_kernel, layer=layer),
        grid=(b, HGRN_HEADS),
        in_specs=[
            _zspec(t, w, OFF_HQ, 0),
            _zspec(t, w, OFF_HFF, 0),
            _zspec(t, w, OFF_HFB, 0),
            _zspec(t, w, OFF_HI, 0),
            _zspec(t, w, OFF_HG, 0),
            pl.BlockSpec((depth, 2, w), lambda i, h: (0, 0, h)),
            pl.BlockSpec((1, w), lambda i, h: (0, h)),
        ],
        out_specs=pl.BlockSpec((None, t, w), lambda i, h: (i, 0, h)),
        out_shape=jax.ShapeDtypeStruct((b, t, BRANCH_W), BF16),
        scratch_shapes=[
            pltpu.VMEM((t, w), F32),
            pltpu.VMEM((t, w), F32),
            pltpu.VMEM((n_chunks, w, w), F32),
            pltpu.VMEM((w, w), F32),
        ],
        compiler_params=_params("parallel", "parallel"),
        name="hgrn",
    )(z3, z3, z3, z3, z3, lb_logits, gain.reshape(1, -1))


def _encoder(x, p, w):
    b, t, _ = x.shape
    depth = w['w_in'].shape[0]
    x2d = x.reshape(b * t, D_MODEL)
    for l in range(depth):
        z3 = _inproj(x2d, w['norm_mix'][l], w['w_in'][l]).reshape(b, t, W_IN)
        br_a = _retention(z3, w['cos'], w['sin'], w['lgq'][l], w['lgv'][l])
        br_b = _nat(z3, w['nat_bias'][l])
        br_c = _rglru(z3, w['conv_w'][l], w['conv_b'][l], w['lru_wg'][l], w['lru_bg'][l], w['lam'][l])
        br_d = _hgrn(z3, w['lb_logits'], w['hgrn_gain'][l], l)
        branches = [a.reshape(b * t, BRANCH_W) for a in (br_a, br_b, br_c, br_d)]
        x2d = _merge(x2d, branches, p[l].reshape(b * t, PLE_DIM), w['norm_mix'][l], w['w_merge'][l],
                     w['w_branch'][l], w['w_out'][l], w['ple_norm'][l], w['w_ple_gate'][l],
                     w['w_ple_proj'][l], w['final_norm'], l == depth - 1)
    return x2d.reshape(b, t, D_MODEL)


def kernel(x_prompt, x_sample, p_prompt, p_sample, norm_mix, w_in, ret_decay_logit, nat_rpb, lru_conv_w,
           lru_conv_b, lru_wa, lru_ba, lru_wx, lru_bx, lru_lambda, hgrn_lb_logits, hgrn_norm, w_branch,
           w_merge, w_out, ple_norm, w_ple_gate, w_ple_proj, final_norm):
    depth = w_in.shape[0]
    t = x_prompt.shape[1]
    rows = t // GRID_W
    cos_tab, sin_tab = _rotary_tables(t)
    gate_w = [_rglru_gate_weights(lru_wa[l], lru_ba[l], lru_wx[l], lru_bx[l]) for l in range(depth)]
    weights = {
        'norm_mix': norm_mix,
        'w_in': jnp.concatenate([_pair_rotary_layout(w_in[..., OFF_RQ:OFF_RK]),
                                 _pair_rotary_layout(w_in[..., OFF_RK:OFF_RV]),
                                 w_in[..., OFF_RV:]], axis=-1).astype(BF16),
        'cos': cos_tab,
        'sin': sin_tab,
        'lgq': _pair_rotary_layout(jnp.repeat(ret_decay_logit.astype(F32), RET_QK, axis=-1)),
        'lgv': jnp.repeat(ret_decay_logit.astype(F32), RET_V, axis=-1),
        'nat_bias': [_nat_bias_tables(nat_rpb[l], rows) for l in range(depth)],
        'conv_w': lru_conv_w,
        'conv_b': lru_conv_b,
        'lru_wg': [g[0] for g in gate_w],
        'lru_bg': [g[1] for g in gate_w],
        'lam': lru_lambda,
        'lb_logits': hgrn_lb_logits,
        'hgrn_gain': hgrn_norm,
        'w_branch': w_branch.astype(BF16),
        'w_merge': w_merge.astype(BF16),
        'w_out': w_out.astype(BF16),
        'ple_norm': ple_norm,
        'w_ple_gate': w_ple_gate.astype(BF16),
        'w_ple_proj': w_ple_proj.astype(BF16),
        'final_norm': final_norm,
    }
    y_prompt = _encoder(x_prompt, p_prompt, weights)
    y_sample = _encoder(x_sample, p_sample, weights)
    return (y_prompt, y_sample)
```

```python
import functools

import numpy as np
import jax
import jax.numpy as jnp
from jax import lax
from jax.experimental import pallas as pl
from jax.experimental.pallas import tpu as pltpu

F32 = jnp.float32
BF16 = jnp.bfloat16

D_MODEL = 1024
PLE_DIM = 256
GRID_W = 64
N_BRANCH = 4
BRANCH_W = 512
RET_HEADS = 4
RET_QK = 64
RET_V = 128
ROPE_BASE = 10000.0
NAT_HEADS = 8
NAT_HD = 64
NAT_WIN_ROWS = 8
NAT_WIN_COLS = 16
LRU_BLOCKS = 8
LRU_BW = 64
LRU_CONV = 4
LRU_C = 8.0
HGRN_HEADS = 4
HGRN_DK = 128
EPS = 1e-6
W_IN = 7168

OFF_RQ, OFF_RK, OFF_RV, OFF_RG = 0, 256, 512, 1024
OFF_NQ, OFF_NK, OFF_NV, OFF_NG = 1536, 2048, 2560, 3072
OFF_LX, OFF_LG = 3584, 4096
OFF_HQ, OFF_HFF, OFF_HFB, OFF_HI, OFF_HG = 4608, 5120, 5632, 6144, 6656

LANES = 128
SUBLANES = 8
VMEM_LIMIT = 56 * 1024 * 1024

CHUNK = 128


def _params(*sem):
    return pltpu.CompilerParams(dimension_semantics=sem, vmem_limit_bytes=VMEM_LIMIT)


def _dot(a, b):
    return jnp.dot(a.astype(BF16), b.astype(BF16), preferred_element_type=F32)


def _dot_nt(a, b):
    return lax.dot_general(a.astype(BF16), b.astype(BF16), (((1,), (1,)), ((), ())),
                           preferred_element_type=F32)


def _dot_tn(a, b):
    return lax.dot_general(a.astype(BF16), b.astype(BF16), (((0,), (0,)), ((), ())),
                           preferred_element_type=F32)


def _rms(x, g):
    return x * lax.rsqrt(jnp.mean(x * x, axis=-1, keepdims=True) + EPS) * g


def _sigmoid(x):
    return jax.nn.sigmoid(x)


def _silu(x):
    return x * jax.nn.sigmoid(x)


def _zspec(t, width, off, grid_pos):
    base = off // width
    if grid_pos == 0:
        return pl.BlockSpec((None, t, width), lambda b, h: (b, 0, base + h))
    return pl.BlockSpec((None, t, width), lambda h, b: (b, 0, base + h))


def _inproj_kernel(x_ref, g_ref, w_ref, z_ref, h_scr):
    @pl.when(pl.program_id(1) == 0)
    def _():
        h_scr[...] = _rms(x_ref[...], g_ref[...]).astype(BF16)

    z_ref[...] = jnp.dot(h_scr[...], w_ref[...], preferred_element_type=F32)


def _inproj(x2d, g, w_bf16):
    m = x2d.shape[0]
    tm, tn = 1024, 1792
    return pl.pallas_call(
        _inproj_kernel,
        grid=(m // tm, W_IN // tn),
        in_specs=[
            pl.BlockSpec((tm, D_MODEL), lambda i, j: (i, 0)),
            pl.BlockSpec((1, D_MODEL), lambda i, j: (0, 0)),
            pl.BlockSpec((D_MODEL, tn), lambda i, j: (0, j)),
        ],
        out_specs=pl.BlockSpec((tm, tn), lambda i, j: (i, j)),
        out_shape=jax.ShapeDtypeStruct((m, W_IN), F32),
        scratch_shapes=[pltpu.VMEM((tm, D_MODEL), BF16)],
        compiler_params=_params("parallel", "arbitrary"),
        name="inproj",
    )(x2d, g.reshape(1, D_MODEL), w_bf16)


def _merge_kernel(x_ref, ba_ref, bb_ref, bc_ref, bd_ref, p_ref, gmix_ref, wmg_ref, wbr_ref, wo_ref,
                  gple_ref, wpg_ref, wpp_ref, gfin_ref, out_ref, *, final):
    x = x_ref[...]
    h = _rms(x, gmix_ref[...]).astype(BF16)
    merged = None
    for j, b_ref in enumerate((ba_ref, bb_ref, bc_ref, bd_ref)):
        gate = _sigmoid(jnp.dot(h, wmg_ref[j], preferred_element_type=F32))
        term = gate * jnp.dot(b_ref[...], wbr_ref[j], preferred_element_type=F32)
        merged = term if merged is None else merged + term
    x1 = x + _dot(merged, wo_ref[...])
    gate2 = _sigmoid(_dot(_rms(x1, gple_ref[...]), wpg_ref[...]))
    x2 = x1 + gate2 * _dot(p_ref[...], wpp_ref[...])
    if final:
        x2 = _rms(x2, gfin_ref[...])
    out_ref[...] = x2


def _merge(x2d, branches, p2d, gmix, wmg, wbr, wo, gple, wpg, wpp, gfin, final):
    m = x2d.shape[0]
    tm = 512
    row = lambda i: (i, 0)
    const2 = lambda i: (0, 0)
    const3 = lambda i: (0, 0, 0)
    once = pl.Buffered(1)
    vec = pl.BlockSpec((1, D_MODEL), const2)
    return pl.pallas_call(
        functools.partial(_merge_kernel, final=final),
        grid=(m // tm,),
        in_specs=[
            pl.BlockSpec((tm, D_MODEL), row),
            pl.BlockSpec((tm, BRANCH_W), row),
            pl.BlockSpec((tm, BRANCH_W), row),
            pl.BlockSpec((tm, BRANCH_W), row),
            pl.BlockSpec((tm, BRANCH_W), row),
            pl.BlockSpec((tm, PLE_DIM), row),
            vec,
            pl.BlockSpec((N_BRANCH, D_MODEL, D_MODEL), const3, pipeline_mode=once),
            pl.BlockSpec((N_BRANCH, BRANCH_W, D_MODEL), const3, pipeline_mode=once),
            pl.BlockSpec((D_MODEL, D_MODEL), const2, pipeline_mode=once),
            vec,
            pl.BlockSpec((D_MODEL, D_MODEL), const2, pipeline_mode=once),
            pl.BlockSpec((PLE_DIM, D_MODEL), const2, pipeline_mode=once),
            vec,
        ],
        out_specs=pl.BlockSpec((tm, D_MODEL), row),
        out_shape=jax.ShapeDtypeStruct((m, D_MODEL), F32),
        compiler_params=_params("parallel"),
        name="merge",
    )(x2d, *branches, p2d, gmix.reshape(1, -1), wmg, wbr, wo, gple.reshape(1, -1), wpg, wpp,
      gfin.reshape(1, -1))


def _retention_kernel(q_ref, k_ref, v_ref, g_ref, cos_ref, sin_ref, lgq_ref, lgv_ref, out_ref,
                      qr_scr, kr_scr, sb_scr, sf_scr):
    t = q_ref.shape[0]
    c = CHUNK
    n_chunks = t // c
    hd = RET_QK

    lane = lax.broadcasted_iota(jnp.int32, (1, 2 * hd), 1)
    head0_q = (lane // (hd // 2)) % 2 == 0
    lane_v = lax.broadcasted_iota(jnp.int32, (1, 2 * RET_V), 1)
    head0_v = lane_v < RET_V

    def rotary(x, cos, sin):
        return x * cos + pltpu.roll(x, hd, 1) * sin

    def rot_body(n, carry):
        sl = pl.ds(pl.multiple_of(n * c, c), c)
        cos = cos_ref[sl, :]
        sin = sin_ref[sl, :]
        qr_scr[sl, :] = rotary(q_ref[sl, :], cos, sin)
        kr_scr[sl, :] = rotary(k_ref[sl, :], cos, sin) * (hd ** -0.5)
        return carry

    lax.fori_loop(0, n_chunks, rot_body, 0, unroll=2)

    lg_f = -jnp.log(1.0 + jnp.exp(-lgq_ref[0:1, :]))
    lg_b = -jnp.log(1.0 + jnp.exp(-lgq_ref[1:2, :]))
    lgv_f = -jnp.log(1.0 + jnp.exp(-lgv_ref[0:1, :]))
    lgv_b = -jnp.log(1.0 + jnp.exp(-lgv_ref[1:2, :]))
    tcol = lax.broadcasted_iota(jnp.int32, (c, 1), 0).astype(F32)
    head_f = jnp.exp((tcol + 1.0) * lg_f)
    head_b = jnp.exp((c - tcol) * lg_b)
    tail_f = jnp.exp((c - 1.0 - tcol) * lg_f)
    tail_b = jnp.exp(tcol * lg_b)
    dec_f = jnp.exp(c * lgv_f)
    dec_b = jnp.exp(c * lgv_b)
    rowk = lax.broadcasted_iota(jnp.int32, (2 * hd, 2 * RET_V), 0)
    colv = lax.broadcasted_iota(jnp.int32, (2 * hd, 2 * RET_V), 1)
    blockdiag = ((rowk // (hd // 2)) % 2) == (colv // RET_V)

    diff = (lax.broadcasted_iota(jnp.int32, (c, c), 0) - lax.broadcasted_iota(jnp.int32, (c, c), 1)).astype(F32)

    def decay_mask(lf, lb):
        fwd = jnp.exp(jnp.maximum(diff, 0.0) * lf)
        bwd = jnp.exp(jnp.maximum(-diff, 0.0) * lb)
        return jnp.where(diff > 0, fwd, jnp.where(diff < 0, bwd, 2.0))

    dmask = jnp.concatenate([decay_mask(lgv_f[:, :c], lgv_b[:, :c]),
                             decay_mask(lgv_f[:, RET_V:RET_V + c], lgv_b[:, RET_V:RET_V + c])], axis=1)

    sf_scr[...] = jnp.zeros_like(sf_scr)

    def bwd_body(i, carry):
        n = n_chunks - 1 - i
        sl = pl.ds(pl.multiple_of(n * c, c), c)
        sb_scr[n] = sf_scr[...]
        loc = _dot_tn(kr_scr[sl, :] * tail_b, v_ref[sl, :])
        sf_scr[...] = sf_scr[...] * dec_b + jnp.where(blockdiag, loc, 0.0)
        return carry

    lax.fori_loop(0, n_chunks, bwd_body, 0, unroll=8)

    sf_scr[...] = jnp.zeros_like(sf_scr)

    def fwd_body(n, carry):
        sl = pl.ds(pl.multiple_of(n * c, c), c)
        qn = qr_scr[sl, :]
        kn = kr_scr[sl, :]
        vn = v_ref[sl, :]
        kstack = jnp.concatenate([jnp.where(head0_q, kn, 0.0), jnp.where(head0_q, 0.0, kn)], axis=0)
        scores = _dot_nt(qn, kstack) * dmask
        vstack = jnp.concatenate([jnp.where(head0_v, vn, 0.0), jnp.where(head0_v, 0.0, vn)], axis=0)
        lhs = jnp.concatenate([scores, qn * head_f, qn * head_b], axis=1)
        rhs = jnp.concatenate([vstack, sf_scr[...], sb_scr[n]], axis=0)
        o = _dot(lhs, rhs)
        loc = _dot_tn(kn * tail_f, vn)
        sf_scr[...] = sf_scr[...] * dec_f + jnp.where(blockdiag, loc, 0.0)
        o0 = o[:, :RET_V]
        o1 = o[:, RET_V:]
        o0 = o0 * lax.rsqrt(jnp.mean(o0 * o0, axis=-1, keepdims=True) + EPS)
        o1 = o1 * lax.rsqrt(jnp.mean(o1 * o1, axis=-1, keepdims=True) + EPS)
        on = jnp.concatenate([o0, o1], axis=1)
        out_ref[sl, :] = (on * _silu(g_ref[sl, :])).astype(out_ref.dtype)
        return carry

    lax.fori_loop(0, n_chunks, fwd_body, 0, unroll=8)


def _retention(z3, cos_tab, sin_tab, lgq, lgv):
    b, t, _ = z3.shape
    n_chunks = t // CHUNK
    return pl.pallas_call(
        _retention_kernel,
        grid=(b, RET_HEADS // 2),
        in_specs=[
            _zspec(t, 2 * RET_QK, OFF_RQ, 0),
            _zspec(t, 2 * RET_QK, OFF_RK, 0),
            _zspec(t, 2 * RET_V, OFF_RV, 0),
            _zspec(t, 2 * RET_V, OFF_RG, 0),
            pl.BlockSpec((t, 2 * RET_QK), lambda i, h: (0, 0)),
            pl.BlockSpec((t, 2 * RET_QK), lambda i, h: (0, 0)),
            pl.BlockSpec((2, 2 * RET_QK), lambda i, h: (0, h)),
            pl.BlockSpec((2, 2 * RET_V), lambda i, h: (0, h)),
        ],
        out_specs=pl.BlockSpec((None, t, 2 * RET_V), lambda i, h: (i, 0, h)),
        out_shape=jax.ShapeDtypeStruct((b, t, BRANCH_W), BF16),
        scratch_shapes=[
            pltpu.VMEM((t, 2 * RET_QK), F32),
            pltpu.VMEM((t, 2 * RET_QK), F32),
            pltpu.VMEM((n_chunks, 2 * RET_QK, 2 * RET_V), F32),
            pltpu.VMEM((2 * RET_QK, 2 * RET_V), F32),
        ],
        compiler_params=_params("parallel", "parallel"),
        name="retention",
    )(z3, z3, z3, z3, cos_tab, sin_tab, lgq, lgv)


def _rotary_tables(t):
    half = RET_QK // 2
    inv = ROPE_BASE ** (-jnp.arange(half, dtype=F32) / half)
    ang = jnp.arange(t, dtype=F32)[:, None] * inv[None, :]
    cos = jnp.cos(ang)
    sin = jnp.sin(ang)
    cos_tab = jnp.tile(cos, (1, 4))
    sin_tab = jnp.concatenate([-sin, -sin, sin, sin], axis=1)
    return cos_tab, sin_tab


def _pair_rotary_layout(a):
    lead = a.shape[:-1]
    a = a.reshape(lead + (RET_HEADS // 2, 2, 2, RET_QK // 2))
    return jnp.swapaxes(a, -3, -2).reshape(lead + (RET_HEADS * RET_QK,))


NAT_QROWS = 8
NAT_KROWS = 16
NAT_KCOLS = 2 * NAT_WIN_COLS


def _nat_kstart(j):
    return int(np.clip(j * NAT_WIN_COLS - NAT_WIN_COLS // 2, 0, GRID_W - NAT_KCOLS))


def _nat_key_row_base(g, rows):
    return int(np.clip(g * NAT_QROWS - NAT_WIN_ROWS // 2, 0, rows - NAT_KROWS))


def _edge_class(i, n):
    return 0 if i == 0 else (2 if i == n - 1 else 1)


def _nat_bias_tables(rpb, rows):
    n_g = rows // NAT_QROWS
    n_cb = GRID_W // NAT_WIN_COLS

    def row_geometry(g):
        dr = np.zeros((NAT_QROWS, NAT_KROWS), np.int32)
        ok = np.zeros((NAT_QROWS, NAT_KROWS), bool)
        kb = _nat_key_row_base(g, rows)
        for rr in range(NAT_QROWS):
            r = g * NAT_QROWS + rr
            rs = int(np.clip(r - NAT_WIN_ROWS // 2, 0, rows - NAT_WIN_ROWS))
            for kr in range(NAT_KROWS):
                ka = kb + kr
                ok[rr, kr] = rs <= ka < rs + NAT_WIN_ROWS
                dr[rr, kr] = int(np.clip(ka - r + NAT_WIN_ROWS - 1, 0, 2 * NAT_WIN_ROWS - 2))
        return dr, ok

    def col_geometry(j):
        dc = np.zeros((NAT_WIN_COLS, NAT_KCOLS), np.int32)
        ok = np.zeros((NAT_WIN_COLS, NAT_KCOLS), bool)
        ks = _nat_kstart(j)
        for qq in range(NAT_WIN_COLS):
            qc = j * NAT_WIN_COLS + qq
            ws = int(np.clip(qc - NAT_WIN_COLS // 2, 0, GRID_W - NAT_WIN_COLS))
            for kc in range(NAT_KCOLS):
                ka = ks + kc
                ok[qq, kc] = ws <= ka < ws + NAT_WIN_COLS
                dc[qq, kc] = int(np.clip(ka - qc + NAT_WIN_COLS - 1, 0, 2 * NAT_WIN_COLS - 2))
        return dc, ok

    def by_class(geometry, n):
        reps = {}
        for i in range(n):
            dx, ok = geometry(i)
            cls = _edge_class(i, n)
            if cls in reps:
                assert (np.where(ok, dx, -1) == np.where(reps[cls][1], reps[cls][0], -1)).all()
            else:
                reps[cls] = (dx, ok)
        filled = [reps.get(cls, reps[0]) for cls in range(3)]
        return np.stack([f[0] for f in filled]), np.stack([f[1] for f in filled])

    dr, row_ok = by_class(row_geometry, n_g)
    dc, col_ok = by_class(col_geometry, n_cb)
    row_sel = (dr[..., None] == np.arange(2 * NAT_WIN_ROWS - 1)).astype(np.float32)
    col_sel = (dc[..., None] == np.arange(2 * NAT_WIN_COLS - 1)).astype(np.float32)
    by_row = jnp.einsum('grka,hab->hgrkb', row_sel, rpb.astype(F32), precision=lax.Precision.HIGHEST)
    bias = jnp.einsum('hgrkb,jqcb->hgjrqkc', by_row, col_sel, precision=lax.Precision.HIGHEST)
    ok6 = row_ok[:, None, :, None, :, None] & col_ok[None, :, None, :, None, :]
    bias = jnp.where(ok6[None], bias, -jnp.inf)
    nq = NAT_QROWS * NAT_WIN_COLS
    nk = NAT_KROWS * NAT_KCOLS
    return bias.reshape(NAT_HEADS // 2, 2, 3, 3, nq, nk)


def _nat_kernel(q_ref, k_ref, v_ref, g_ref, bias_ref, out_ref):
    t = q_ref.shape[0]
    rows = t // GRID_W
    n_g = rows // NAT_QROWS
    n_cb = GRID_W // NAT_WIN_COLS
    nq = NAT_QROWS * NAT_WIN_COLS
    lane = lax.broadcasted_iota(jnp.int32, (1, 2 * NAT_HD), 1)
    head0 = lane < NAT_HD

    def group_body(g, carry):
        kb = jnp.clip(g * NAT_QROWS - NAT_WIN_ROWS // 2, 0, rows - NAT_KROWS)
        g_cls = jnp.where(g == 0, 0, jnp.where(g == n_g - 1, 2, 1))
        for j in range(n_cb):
            ks = _nat_kstart(j)

            def qrows(ref, rr):
                start = pl.multiple_of((g * NAT_QROWS + rr) * GRID_W + j * NAT_WIN_COLS, NAT_WIN_COLS)
                return ref[pl.ds(start, NAT_WIN_COLS), :]

            def krows(ref, kr):
                start = pl.multiple_of((kb + kr) * GRID_W + ks, SUBLANES)
                return ref[pl.ds(start, NAT_KCOLS), :]

            qs = jnp.concatenate([qrows(q_ref, rr) for rr in range(NAT_QROWS)], axis=0) * (NAT_HD ** -0.5)
            q2 = jnp.concatenate([jnp.where(head0, qs, 0.0), jnp.where(head0, 0.0, qs)], axis=0)
            kblk = jnp.concatenate([krows(k_ref, kr) for kr in range(NAT_KROWS)], axis=0)
            vblk = jnp.concatenate([krows(v_ref, kr) for kr in range(NAT_KROWS)], axis=0)
            s = _dot_nt(q2, kblk)
            j_cls = _edge_class(j, n_cb)
            s = s + jnp.concatenate([bias_ref[0, g_cls, j_cls], bias_ref[1, g_cls, j_cls]], axis=0)
            m = jnp.max(s, axis=-1, keepdims=True)
            p = jnp.exp(s - m)
            vext = jnp.concatenate([vblk, jnp.ones_like(vblk)], axis=1)
            o_ext = _dot(p, vext)
            o2 = o_ext[:, :2 * NAT_HD] / o_ext[:, 2 * NAT_HD:]
            o = jnp.where(head0, o2[:nq], o2[nq:])
            gs = jnp.concatenate([qrows(g_ref, rr) for rr in range(NAT_QROWS)], axis=0)
            res = (o * _silu(gs)).astype(out_ref.dtype)
            for rr in range(NAT_QROWS):
                start = pl.multiple_of((g * NAT_QROWS + rr) * GRID_W + j * NAT_WIN_COLS, NAT_WIN_COLS)
                out_ref[pl.ds(start, NAT_WIN_COLS), :] = res[rr * NAT_WIN_COLS:(rr + 1) * NAT_WIN_COLS]
        return carry

    lax.fori_loop(0, n_g, group_body, 0, unroll=4)


def _nat(z3, bias_tab):
    b, t, _ = z3.shape
    n_hp = NAT_HEADS // 2
    w = 2 * NAT_HD
    return pl.pallas_call(
        _nat_kernel,
        grid=(n_hp, b),
        in_specs=[
            _zspec(t, w, OFF_NQ, 1),
            _zspec(t, w, OFF_NK, 1),
            _zspec(t, w, OFF_NV, 1),
            _zspec(t, w, OFF_NG, 1),
            pl.BlockSpec((None,) + bias_tab.shape[1:], lambda h, i: (h, 0, 0, 0, 0, 0)),
        ],
        out_specs=pl.BlockSpec((None, t, w), lambda h, i: (i, 0, h)),
        out_shape=jax.ShapeDtypeStruct((b, t, BRANCH_W), BF16),
        compiler_params=_params("parallel", "parallel"),
        name="nat",
    )(z3, z3, z3, z3, bias_tab)


def _rglru_kernel(x_ref, g_ref, cw_ref, cb_ref, wg_ref, bg_ref, lam_ref, out_ref,
                  xp_scr, af_scr, bf_scr, ab_scr, bb_scr, hf_scr, pf_scr, hb_scr, pb_scr):
    t = x_ref.shape[0]
    n_seg = SUBLANES
    tc = t // n_seg
    pad = SUBLANES
    w = LANES

    xp_scr[0:pad, :] = jnp.zeros((pad, w), F32)
    xp_scr[pad + t:pad + t + pad, :] = jnp.zeros((pad, w), F32)
    xp_scr[pad:pad + t, :] = x_ref[...]

    lam = lam_ref[...]
    neg = -lam
    softplus = jnp.maximum(neg, 0.0) + jnp.log(1.0 + jnp.exp(-jnp.abs(neg)))
    log2_a_per_r = softplus * float(-LRU_C * np.log2(np.e))
    cw = cw_ref[...]
    cb = cb_ref[...]
    bg = bg_ref[...]

    def gate_body(n, carry):
        t0 = pl.multiple_of(n * tc, tc)
        xx = xp_scr[pl.ds(t0, tc + 2 * pad), :]
        total = tc + 2 * pad
        xc = cb
        for j in range(LRU_CONV):
            shift = LRU_CONV // 2 - j
            xs = xx if shift == 0 else pltpu.roll(xx, shift % total, 0)
            xc = xc + xs[pad:pad + tc] * cw[j:j + 1, :]
        gates = _dot(xc, wg_ref[...]) + bg
        seg_rows = pl.ds(n, tc, stride=n_seg)
        for d, (a_scr, b_scr) in enumerate(((af_scr, bf_scr), (ab_scr, bb_scr))):
            r = _sigmoid(gates[:, (2 * d) * w:(2 * d + 1) * w])
            i = _sigmoid(gates[:, (2 * d + 1) * w:(2 * d + 2) * w])
            a = jnp.exp2(r * log2_a_per_r[d:d + 1, :])
            a_scr[seg_rows, :] = a
            gap = 1.0 - a * a
            b_scr[seg_rows, :] = gap * lax.rsqrt(jnp.maximum(gap, 1e-30)) * (i * xc)
        return carry

    lax.fori_loop(0, n_seg, gate_body, 0, unroll=2)

    def scan_body(i, carry):
        hf, pf, hb, pb = carry
        sl = pl.ds(pl.multiple_of(i * n_seg, n_seg), n_seg)
        a = af_scr[sl, :]
        hf = a * hf + bf_scr[sl, :]
        pf = a * pf
        hf_scr[sl, :] = hf
        pf_scr[sl, :] = pf
        slb = pl.ds(pl.multiple_of((tc - 1 - i) * n_seg, n_seg), n_seg)
        a = ab_scr[slb, :]
        hb = a * hb + bb_scr[slb, :]
        pb = a * pb
        hb_scr[slb, :] = hb
        pb_scr[slb, :] = pb
        return hf, pf, hb, pb

    zero = jnp.zeros((n_seg, w), F32)
    one = jnp.ones((n_seg, w), F32)
    hf, pf, hb, pb = lax.fori_loop(0, tc, scan_body, (zero, one, zero, one), unroll=8)

    row = lax.broadcasted_iota(jnp.int32, (n_seg, w), 0)
    for s in (1, 2, 4):
        keep = row >= s
        hf = jnp.where(keep, pf * pltpu.roll(hf, s, 0) + hf, hf)
        pf = jnp.where(keep, pf * pltpu.roll(pf, s, 0), pf)
        keep = row < n_seg - s
        hb = jnp.where(keep, pb * pltpu.roll(hb, n_seg - s, 0) + hb, hb)
        pb = jnp.where(keep, pb * pltpu.roll(pb, n_seg - s, 0), pb)
    carry_f = jnp.where(row >= 1, pltpu.roll(hf, 1, 0), 0.0)
    carry_b = jnp.where(row < n_seg - 1, pltpu.roll(hb, n_seg - 1, 0), 0.0)

    def fix_body(i, carry):
        sl = pl.ds(pl.multiple_of(i * n_seg, n_seg), n_seg)
        af_scr[sl, :] = (hf_scr[sl, :] + pf_scr[sl, :] * carry_f) + (hb_scr[sl, :] + pb_scr[sl, :] * carry_b)
        return carry

    lax.fori_loop(0, tc, fix_body, 0, unroll=8)

    def out_body(n, carry):
        h = af_scr[pl.ds(n, tc, stride=n_seg), :]
        sl = pl.ds(pl.multiple_of(n * tc, tc), tc)
        out_ref[sl, :] = (h * _silu(g_ref[sl, :])).astype(out_ref.dtype)
        return carry

    lax.fori_loop(0, n_seg, out_body, 0)


def _rglru(z3, conv_w, conv_b, wg, bg, lam):
    b, t, _ = z3.shape
    n_cb = BRANCH_W // LANES
    return pl.pallas_call(
        _rglru_kernel,
        grid=(b, n_cb),
        in_specs=[
            _zspec(t, LANES, OFF_LX, 0),
            _zspec(t, LANES, OFF_LG, 0),
            pl.BlockSpec((LRU_CONV, LANES), lambda i, h: (0, h)),
            pl.BlockSpec((1, LANES), lambda i, h: (0, h)),
            pl.BlockSpec((None, LANES, 4 * LANES), lambda i, h: (h, 0, 0)),
            pl.BlockSpec((None, 1, 4 * LANES), lambda i, h: (h, 0, 0)),
            pl.BlockSpec((2, LANES), lambda i, h: (0, h)),
        ],
        out_specs=pl.BlockSpec((None, t, LANES), lambda i, h: (i, 0, h)),
        out_shape=jax.ShapeDtypeStruct((b, t, BRANCH_W), BF16),
        scratch_shapes=[pltpu.VMEM((t + 2 * SUBLANES, LANES), F32)] + [pltpu.VMEM((t, LANES), F32)] * 8,
        compiler_params=_params("parallel", "parallel"),
        name="rglru",
    )(z3, z3, conv_w, conv_b.reshape(1, BRANCH_W), wg, bg, lam)


def _rglru_gate_weights(wa, ba, wx, bx):
    n_cb = BRANCH_W // LANES
    per = LANES // LRU_BW

    def blockdiag(w):
        w = w.reshape(n_cb, per, LRU_BW, LRU_BW)
        eye = jnp.eye(per, dtype=w.dtype)
        return jnp.einsum('cpjk,pq->cpjqk', w, eye).reshape(n_cb, LANES, LANES)

    wg = jnp.concatenate([blockdiag(wa[0]), blockdiag(wx[0]), blockdiag(wa[1]), blockdiag(wx[1])], axis=-1)
    bg = jnp.concatenate([ba[0].reshape(n_cb, 1, LANES), bx[0].reshape(n_cb, 1, LANES),
                          ba[1].reshape(n_cb, 1, LANES), bx[1].reshape(n_cb, 1, LANES)], axis=-1)
    return wg.astype(BF16), bg.astype(F32)


def _prefix_sum_rows(x):
    c, w = x.shape
    x3 = x.reshape(c // SUBLANES, SUBLANES, w)
    row = lax.broadcasted_iota(jnp.int32, (1, SUBLANES, w), 1)
    s = 1
    while s < SUBLANES:
        x3 = x3 + jnp.where(row >= s, pltpu.roll(x3, s, 1), 0.0)
        s *= 2
    tile_tot = jnp.broadcast_to(x3[:, SUBLANES - 1:, :], x3.shape).reshape(c, w)
    x = x3.reshape(c, w)
    while s < c:
        shifted = jnp.concatenate([jnp.zeros((s, w), x.dtype), tile_tot[:c - s]], axis=0)
        x = x + shifted
        tile_tot = tile_tot + shifted
        s *= 2
    return x


def _hgrn_forget(zf, lb):
    f = lb + (1.0 - lb) * _sigmoid(zf)
    return f, jnp.log2(f)


def _hgrn_tile_decays(size, f3, fb3, bs3, cs3, pos):
    half = size // 2
    if size == 2:
        return jnp.where(pos == 1, f3, 1.0), jnp.where(pos == 0, fb3, 1.0)
    if size == 4:
        f_prev, f_next = pltpu.roll(f3, 1, 1), pltpu.roll(f3, SUBLANES - 1, 1)
        fb_prev, fb_next = pltpu.roll(fb3, 1, 1), pltpu.roll(fb3, SUBLANES - 1, 1)
        e_f = jnp.where(pos == 0, f_next, jnp.where(pos == 1, 1.0, jnp.where(pos == 2, f3, f_prev * f3)))
        e_b = jnp.where(pos == 0, fb3 * fb_next, jnp.where(pos == 1, fb3, jnp.where(pos == 2, 1.0, fb_prev)))
        return e_f, e_b
    sign = jnp.where(pos >= half, 1.0, -1.0)
    e_f = jnp.exp2((bs3 - bs3[:, half - 1:half, :]) * sign)
    e_b = jnp.exp2((cs3[:, half:half + 1, :] - cs3) * sign)
    return e_f, e_b


def _hgrn_level_operands(size, q, kf, kb, f, fb, bs, cs):
    c, w = q.shape
    half = size // 2
    if size <= SUBLANES:
        tiled = lambda a: a.reshape(c // SUBLANES, SUBLANES, w)
        pos = lax.broadcasted_iota(jnp.int32, (1, SUBLANES, w), 1) % size
        upper = pos >= half
        e_f, e_b = _hgrn_tile_decays(size, tiled(f), tiled(fb), tiled(bs), tiled(cs), pos)
        z_f = e_f * jnp.where(upper, tiled(q), tiled(kf))
        z_b = e_b * jnp.where(upper, tiled(kb), tiled(q))
        x = jnp.concatenate([jnp.where(upper, z_f, 0.0), jnp.where(upper, 0.0, z_b)], axis=2)
        y = jnp.concatenate([z_f, z_b], axis=2)
        return x.reshape(c, 2 * w), y.reshape(c, 2 * w)
    xs, ys = [], []
    zero = jnp.zeros((half, w), F32)
    for i in range(c // half):
        rows = slice(i * half, (i + 1) * half)
        if i % 2 == 0:
            ref_f = bs[(i + 1) * half - 1:(i + 1) * half]
            ref_b = cs[(i + 1) * half:(i + 1) * half + 1]
            z_f = kf[rows] * jnp.exp2(ref_f - bs[rows])
            z_b = q[rows] * jnp.exp2(cs[rows] - ref_b)
            xs.append(jnp.concatenate([zero, z_b], axis=1))
        else:
            ref_f = bs[i * half - 1:i * half]
            ref_b = cs[i * half:i * half + 1]
            z_f = q[rows] * jnp.exp2(bs[rows] - ref_f)
            z_b = kb[rows] * jnp.exp2(ref_b - cs[rows])
            xs.append(jnp.concatenate([z_f, zero], axis=1))
        ys.append(jnp.concatenate([z_f, z_b], axis=1))
    return jnp.concatenate(xs, axis=0), jnp.concatenate(ys, axis=0)


def _hgrn_kernel(q_ref, ff_ref, fb_ref, v_ref, g_ref, lbl_ref, gain_ref, out_ref,
                 fb_scr, cs_scr, sb_scr, st_scr, *, layer):
    t = q_ref.shape[0]
    c = CHUNK
    n_chunks = t // c
    w = HGRN_DK

    logits = lbl_ref[...]
    mx = jnp.max(logits, axis=0)
    ex = jnp.exp(logits - mx[None])
    tot = jnp.sum(ex, axis=0)
    lb = jnp.zeros_like(tot)
    for i in range(1, layer + 1):
        lb = lb + ex[i] / tot
    lb_f = lb[0:1, :]
    lb_b = lb[1:2, :]

    pair_xor = lax.broadcasted_iota(jnp.int32, (c, c), 0) ^ lax.broadcasted_iota(jnp.int32, (c, c), 1)
    pair_level = jnp.zeros((c, c), jnp.int32)
    size = 2
    while size <= c:
        pair_level = jnp.where(pair_xor >= size // 2, size, pair_level)
        size *= 2

    st_scr[...] = jnp.zeros_like(st_scr)

    def bwd_body(i, carry):
        n = n_chunks - 1 - i
        sl = pl.ds(pl.multiple_of(n * c, c), c)
        fb, gb = _hgrn_forget(fb_ref[sl, :], lb_b)
        pre = _prefix_sum_rows(gb)
        total = pre[c - 1:c, :]
        cs = total - pre + gb
        fb_scr[sl, :] = fb
        cs_scr[sl, :] = cs
        sb_scr[n] = st_scr[...]
        st_scr[...] = st_scr[...] * jnp.exp2(total) + _dot_tn(v_ref[sl, :], (1.0 - fb) * jnp.exp2(total - cs))
        return carry

    lax.fori_loop(0, n_chunks, bwd_body, 0, unroll=8)

    st_scr[...] = jnp.zeros_like(st_scr)

    def fwd_body(n, carry):
        sl = pl.ds(pl.multiple_of(n * c, c), c)
        q = _silu(q_ref[sl, :])
        v = v_ref[sl, :]
        f, gf = _hgrn_forget(ff_ref[sl, :], lb_f)
        kf = 1.0 - f
        bs = _prefix_sum_rows(gf)
        fb = fb_scr[sl, :]
        kb = 1.0 - fb
        cs = cs_scr[sl, :]

        att = None
        size = 2
        while size <= c:
            x, y = _hgrn_level_operands(size, q, kf, kb, f, fb, bs, cs)
            att = jnp.where(pair_level == size, _dot_nt(x, y), 0.0 if att is None else att)
            size *= 2

        diag = jnp.sum(q * (kf + kb), axis=-1, keepdims=True)
        o = _dot(att, v) + diag * v
        inter = jnp.concatenate([q * jnp.exp2(bs), q * jnp.exp2(cs)], axis=1)
        states = jnp.concatenate([st_scr[...], sb_scr[n]], axis=1)
        o = o + _dot_nt(inter, states)
        last = bs[c - 1:c, :]
        st_scr[...] = st_scr[...] * jnp.exp2(last) + _dot_tn(v, kf * jnp.exp2(last - bs))
        o = o * lax.rsqrt(jnp.mean(o * o, axis=-1, keepdims=True) + EPS) * gain_ref[...]
        out_ref[sl, :] = (o * _silu(g_ref[sl, :])).astype(out_ref.dtype)
        return carry

    lax.fori_loop(0, n_chunks, fwd_body, 0, unroll=8)


def _hgrn(z3, lb_logits, gain, layer):
    b, t, _ = z3.shape
    depth = lb_logits.shape[0]
    w = HGRN_DK
    n_chunks = t // CHUNK
    return pl.pallas_call(
        functools.partial(_hgrn_kernel, layer=layer),
        grid=(b, HGRN_HEADS),
        in_specs=[
            _zspec(t, w, OFF_HQ, 0),
            _zspec(t, w, OFF_HFF, 0),
            _zspec(t, w, OFF_HFB, 0),
            _zspec(t, w, OFF_HI, 0),
            _zspec(t, w, OFF_HG, 0),
            pl.BlockSpec((depth, 2, w), lambda i, h: (0, 0, h)),
            pl.BlockSpec((1, w), lambda i, h: (0, h)),
        ],
        out_specs=pl.BlockSpec((None, t, w), lambda i, h: (i, 0, h)),
        out_shape=jax.ShapeDtypeStruct((b, t, BRANCH_W), BF16),
        scratch_shapes=[
            pltpu.VMEM((t, w), F32),
            pltpu.VMEM((t, w), F32),
            pltpu.VMEM((n_chunks, w, w), F32),
            pltpu.VMEM((w, w), F32),
        ],
        compiler_params=_params("parallel", "parallel"),
        name="hgrn",
    )(z3, z3, z3, z3, z3, lb_logits, gain.reshape(1, -1))


def _encoder(x, p, w):
    b, t, _ = x.shape
    depth = w['w_in'].shape[0]
    x2d = x.reshape(b * t, D_MODEL)
    for l in range(depth):
        z3 = _inproj(x2d, w['norm_mix'][l], w['w_in'][l]).reshape(b, t, W_IN)
        br_a = _retention(z3, w['cos'], w['sin'], w['lgq'][l], w['lgv'][l])
        br_b = _nat(z3, w['nat_bias'][l])
        br_c = _rglru(z3, w['conv_w'][l], w['conv_b'][l], w['lru_wg'][l], w['lru_bg'][l], w['lam'][l])
        br_d = _hgrn(z3, w['lb_logits'], w['hgrn_gain'][l], l)
        branches = [a.reshape(b * t, BRANCH_W) for a in (br_a, br_b, br_c, br_d)]
        x2d = _merge(x2d, branches, p[l].reshape(b * t, PLE_DIM), w['norm_mix'][l], w['w_merge'][l],
                     w['w_branch'][l], w['w_out'][l], w['ple_norm'][l], w['w_ple_gate'][l],
                     w['w_ple_proj'][l], w['final_norm'], l == depth - 1)
    return x2d.reshape(b, t, D_MODEL)


def kernel(x_prompt, x_sample, p_prompt, p_sample, norm_mix, w_in, ret_decay_logit, nat_rpb, lru_conv_w,
           lru_conv_b, lru_wa, lru_ba, lru_wx, lru_bx, lru_lambda, hgrn_lb_logits, hgrn_norm, w_branch,
           w_merge, w_out, ple_norm, w_ple_gate, w_ple_proj, final_norm):
    depth = w_in.shape[0]
    t = x_prompt.shape[1]
    rows = t // GRID_W
    cos_tab, sin_tab = _rotary_tables(t)
    gate_w = [_rglru_gate_weights(lru_wa[l], lru_ba[l], lru_wx[l], lru_bx[l]) for l in range(depth)]
    weights = {
        'norm_mix': norm_mix,
        'w_in': jnp.concatenate([_pair_rotary_layout(w_in[..., OFF_RQ:OFF_RK]),
                                 _pair_rotary_layout(w_in[..., OFF_RK:OFF_RV]),
                                 w_in[..., OFF_RV:]], axis=-1).astype(BF16),
        'cos': cos_tab,
        'sin': sin_tab,
        'lgq': _pair_rotary_layout(jnp.repeat(ret_decay_logit.astype(F32), RET_QK, axis=-1)),
        'lgv': jnp.repeat(ret_decay_logit.astype(F32), RET_V, axis=-1),
        'nat_bias': [_nat_bias_tables(nat_rpb[l], rows) for l in range(depth)],
        'conv_w': lru_conv_w,
        'conv_b': lru_conv_b,
        'lru_wg': [g[0] for g in gate_w],
        'lru_bg': [g[1] for g in gate_w],
        'lam': lru_lambda,
        'lb_logits': hgrn_lb_logits,
        'hgrn_gain': hgrn_norm,
        'w_branch': w_branch.astype(BF16),
        'w_merge': w_merge.astype(BF16),
        'w_out': w_out.astype(BF16),
        'ple_norm': ple_norm,
        'w_ple_gate': w_ple_gate.astype(BF16),
        'w_ple_proj': w_ple_proj.astype(BF16),
        'final_norm': final_norm,
    }
    y_prompt = _encoder(x_prompt, p_prompt, weights)
    y_sample = _encoder(x_sample, p_sample, weights)
    return (y_prompt, y_sample)
```

```python
import functools

import numpy as np
import jax
import jax.numpy as jnp
from jax import lax
from jax.experimental import pallas as pl
from jax.experimental.pallas import tpu as pltpu

F32 = jnp.float32
BF16 = jnp.bfloat16

D_MODEL = 1024
PLE_DIM = 256
GRID_W = 64
N_BRANCH = 4
BRANCH_W = 512
RET_HEADS = 4
RET_QK = 64
RET_V = 128
ROPE_BASE = 10000.0
NAT_HEADS = 8
NAT_HD = 64
NAT_WIN_ROWS = 8
NAT_WIN_COLS = 16
LRU_BLOCKS = 8
LRU_BW = 64
LRU_CONV = 4
LRU_C = 8.0
HGRN_HEADS = 4
HGRN_DK = 128
EPS = 1e-6
W_IN = 7168

OFF_RQ, OFF_RK, OFF_RV, OFF_RG = 0, 256, 512, 1024
OFF_NQ, OFF_NK, OFF_NV, OFF_NG = 1536, 2048, 2560, 3072
OFF_LX, OFF_LG = 3584, 4096
OFF_HQ, OFF_HFF, OFF_HFB, OFF_HI, OFF_HG = 4608, 5120, 5632, 6144, 6656

LANES = 128
SUBLANES = 8
VMEM_LIMIT = 56 * 1024 * 1024

CHUNK = 128
HGRN_CHUNK = 128


def _params(*sem):
    return pltpu.CompilerParams(dimension_semantics=sem, vmem_limit_bytes=VMEM_LIMIT)


def _dot(a, b):
    return jnp.dot(a.astype(BF16), b.astype(BF16), preferred_element_type=F32)


def _dot_nt(a, b):
    return lax.dot_general(a.astype(BF16), b.astype(BF16), (((1,), (1,)), ((), ())),
                           preferred_element_type=F32)


def _dot_tn(a, b):
    return lax.dot_general(a.astype(BF16), b.astype(BF16), (((0,), (0,)), ((), ())),
                           preferred_element_type=F32)


def _rms(x, g):
    return x * lax.rsqrt(jnp.mean(x * x, axis=-1, keepdims=True) + EPS) * g


def _sigmoid(x):
    return jax.nn.sigmoid(x)


def _silu(x):
    return x * jax.nn.sigmoid(x)


def _zspec(t, width, off, grid_pos):
    base = off // width
    if grid_pos == 0:
        return pl.BlockSpec((None, t, width), lambda b, h: (b, 0, base + h))
    return pl.BlockSpec((None, t, width), lambda h, b: (b, 0, base + h))


def _inproj_kernel(x_ref, g_ref, w_ref, z_ref, h_scr):
    @pl.when(pl.program_id(1) == 0)
    def _():
        h_scr[...] = _rms(x_ref[...], g_ref[...]).astype(BF16)

    z_ref[...] = jnp.dot(h_scr[...], w_ref[...], preferred_element_type=F32)


def _inproj(x2d, g, w_bf16):
    m = x2d.shape[0]
    tm, tn = 1024, 1792
    return pl.pallas_call(
        _inproj_kernel,
        grid=(m // tm, W_IN // tn),
        in_specs=[
            pl.BlockSpec((tm, D_MODEL), lambda i, j: (i, 0)),
            pl.BlockSpec((1, D_MODEL), lambda i, j: (0, 0)),
            pl.BlockSpec((D_MODEL, tn), lambda i, j: (0, j)),
        ],
        out_specs=pl.BlockSpec((tm, tn), lambda i, j: (i, j)),
        out_shape=jax.ShapeDtypeStruct((m, W_IN), F32),
        scratch_shapes=[pltpu.VMEM((tm, D_MODEL), BF16)],
        compiler_params=_params("parallel", "arbitrary"),
        name="inproj",
    )(x2d, g.reshape(1, D_MODEL), w_bf16)


def _merge_kernel(x_ref, ba_ref, bb_ref, bc_ref, bd_ref, p_ref, gmix_ref, wmg_ref, wbr_ref, wo_ref,
                  gple_ref, wpg_ref, wpp_ref, gfin_ref, out_ref, *, final):
    x = x_ref[...]
    h = _rms(x, gmix_ref[...]).astype(BF16)
    merged = None
    for j, b_ref in enumerate((ba_ref, bb_ref, bc_ref, bd_ref)):
        gate = _sigmoid(jnp.dot(h, wmg_ref[j], preferred_element_type=F32))
        term = gate * jnp.dot(b_ref[...], wbr_ref[j], preferred_element_type=F32)
        merged = term if merged is None else merged + term
    x1 = x + _dot(merged, wo_ref[...])
    gate2 = _sigmoid(_dot(_rms(x1, gple_ref[...]), wpg_ref[...]))
    x2 = x1 + gate2 * _dot(p_ref[...], wpp_ref[...])
    if final:
        x2 = _rms(x2, gfin_ref[...])
    out_ref[...] = x2


def _merge(x2d, branches, p2d, gmix, wmg, wbr, wo, gple, wpg, wpp, gfin, final):
    m = x2d.shape[0]
    tm = 512
    row = lambda i: (i, 0)
    const2 = lambda i: (0, 0)
    const3 = lambda i: (0, 0, 0)
    once = pl.Buffered(1)
    vec = pl.BlockSpec((1, D_MODEL), const2)
    return pl.pallas_call(
        functools.partial(_merge_kernel, final=final),
        grid=(m // tm,),
        in_specs=[
            pl.BlockSpec((tm, D_MODEL), row),
            pl.BlockSpec((tm, BRANCH_W), row),
            pl.BlockSpec((tm, BRANCH_W), row),
            pl.BlockSpec((tm, BRANCH_W), row),
            pl.BlockSpec((tm, BRANCH_W), row),
            pl.BlockSpec((tm, PLE_DIM), row),
            vec,
            pl.BlockSpec((N_BRANCH, D_MODEL, D_MODEL), const3, pipeline_mode=once),
            pl.BlockSpec((N_BRANCH, BRANCH_W, D_MODEL), const3, pipeline_mode=once),
            pl.BlockSpec((D_MODEL, D_MODEL), const2, pipeline_mode=once),
            vec,
            pl.BlockSpec((D_MODEL, D_MODEL), const2, pipeline_mode=once),
            pl.BlockSpec((PLE_DIM, D_MODEL), const2, pipeline_mode=once),
            vec,
        ],
        out_specs=pl.BlockSpec((tm, D_MODEL), row),
        out_shape=jax.ShapeDtypeStruct((m, D_MODEL), F32),
        compiler_params=_params("parallel"),
        name="merge",
    )(x2d, *branches, p2d, gmix.reshape(1, -1), wmg, wbr, wo, gple.reshape(1, -1), wpg, wpp,
      gfin.reshape(1, -1))


def _retention_kernel(q_ref, k_ref, v_ref, g_ref, cos_ref, sin_ref, lgq_ref, lgv_ref, out_ref,
                      kr_scr, sb_scr, sf_scr):
    t = q_ref.shape[0]
    c = CHUNK
    n_chunks = t // c
    hd = RET_QK

    lane = lax.broadcasted_iota(jnp.int32, (1, 2 * hd), 1)
    head0_q = (lane // (hd // 2)) % 2 == 0
    lane_v = lax.broadcasted_iota(jnp.int32, (1, 2 * RET_V), 1)
    head0_v = lane_v < RET_V

    def rotary(x, cos, sin):
        return x * cos + pltpu.roll(x, hd, 1) * sin

    lg_f = -jnp.log(1.0 + jnp.exp(-lgq_ref[0:1, :]))
    lg_b = -jnp.log(1.0 + jnp.exp(-lgq_ref[1:2, :]))
    lgv_f = -jnp.log(1.0 + jnp.exp(-lgv_ref[0:1, :]))
    lgv_b = -jnp.log(1.0 + jnp.exp(-lgv_ref[1:2, :]))
    tcol = lax.broadcasted_iota(jnp.int32, (c, 1), 0).astype(F32)
    head_f = jnp.exp((tcol + 1.0) * lg_f)
    head_b = jnp.exp((c - tcol) * lg_b)
    tail_f = jnp.exp((c - 1.0 - tcol) * lg_f)
    tail_b = jnp.exp(tcol * lg_b)
    dec_f = jnp.exp(c * lgv_f)
    dec_b = jnp.exp(c * lgv_b)
    rowk = lax.broadcasted_iota(jnp.int32, (2 * hd, 2 * RET_V), 0)
    colv = lax.broadcasted_iota(jnp.int32, (2 * hd, 2 * RET_V), 1)
    blockdiag = ((rowk // (hd // 2)) % 2) == (colv // RET_V)

    diff = (lax.broadcasted_iota(jnp.int32, (c, c), 0) - lax.broadcasted_iota(jnp.int32, (c, c), 1)).astype(F32)

    def decay_mask(lf, lb):
        fwd = jnp.exp(jnp.maximum(diff, 0.0) * lf)
        bwd = jnp.exp(jnp.maximum(-diff, 0.0) * lb)
        return jnp.where(diff > 0, fwd, jnp.where(diff < 0, bwd, 2.0))

    dmask = jnp.concatenate([decay_mask(lgv_f[:, :c], lgv_b[:, :c]),
                             decay_mask(lgv_f[:, RET_V:RET_V + c], lgv_b[:, RET_V:RET_V + c])], axis=1)

    sf_scr[...] = jnp.zeros_like(sf_scr)

    def bwd_body(i, carry):
        n = n_chunks - 1 - i
        sl = pl.ds(pl.multiple_of(n * c, c), c)
        sb_scr[n] = sf_scr[...]
        kn = rotary(k_ref[sl, :], cos_ref[sl, :], sin_ref[sl, :]) * (hd ** -0.5)
        kr_scr[sl, :] = kn
        loc = _dot_tn(kn * tail_b, v_ref[sl, :])
        sf_scr[...] = sf_scr[...] * dec_b + jnp.where(blockdiag, loc, 0.0)
        return carry

    lax.fori_loop(0, n_chunks, bwd_body, 0, unroll=8)

    sf_scr[...] = jnp.zeros_like(sf_scr)

    def fwd_body(n, carry):
        sl = pl.ds(pl.multiple_of(n * c, c), c)
        qn = rotary(q_ref[sl, :], cos_ref[sl, :], sin_ref[sl, :])
        kn = kr_scr[sl, :]
        vn = v_ref[sl, :]
        kstack = jnp.concatenate([jnp.where(head0_q, kn, 0.0), jnp.where(head0_q, 0.0, kn)], axis=0)
        scores = _dot_nt(qn, kstack) * dmask
        vstack = jnp.concatenate([jnp.where(head0_v, vn, 0.0), jnp.where(head0_v, 0.0, vn)], axis=0)
        lhs = jnp.concatenate([scores, qn * head_f, qn * head_b], axis=1)
        rhs = jnp.concatenate([vstack, sf_scr[...], sb_scr[n]], axis=0)
        o = _dot(lhs, rhs)
        loc = _dot_tn(kn * tail_f, vn)
        sf_scr[...] = sf_scr[...] * dec_f + jnp.where(blockdiag, loc, 0.0)
        o0 = o[:, :RET_V]
        o1 = o[:, RET_V:]
        o0 = o0 * lax.rsqrt(jnp.mean(o0 * o0, axis=-1, keepdims=True) + EPS)
        o1 = o1 * lax.rsqrt(jnp.mean(o1 * o1, axis=-1, keepdims=True) + EPS)
        on = jnp.concatenate([o0, o1], axis=1)
        out_ref[sl, :] = (on * _silu(g_ref[sl, :])).astype(out_ref.dtype)
        return carry

    lax.fori_loop(0, n_chunks, fwd_body, 0, unroll=8)


def _retention(z3, cos_tab, sin_tab, lgq, lgv):
    b, t, _ = z3.shape
    n_chunks = t // CHUNK
    return pl.pallas_call(
        _retention_kernel,
        grid=(b, RET_HEADS // 2),
        in_specs=[
            _zspec(t, 2 * RET_QK, OFF_RQ, 0),
            _zspec(t, 2 * RET_QK, OFF_RK, 0),
            _zspec(t, 2 * RET_V, OFF_RV, 0),
            _zspec(t, 2 * RET_V, OFF_RG, 0),
            pl.BlockSpec((t, 2 * RET_QK), lambda i, h: (0, 0)),
            pl.BlockSpec((t, 2 * RET_QK), lambda i, h: (0, 0)),
            pl.BlockSpec((2, 2 * RET_QK), lambda i, h: (0, h)),
            pl.BlockSpec((2, 2 * RET_V), lambda i, h: (0, h)),
        ],
        out_specs=pl.BlockSpec((None, t, 2 * RET_V), lambda i, h: (i, 0, h)),
        out_shape=jax.ShapeDtypeStruct((b, t, BRANCH_W), BF16),
        scratch_shapes=[
            pltpu.VMEM((t, 2 * RET_QK), F32),
            pltpu.VMEM((n_chunks, 2 * RET_QK, 2 * RET_V), F32),
            pltpu.VMEM((2 * RET_QK, 2 * RET_V), F32),
        ],
        compiler_params=_params("parallel", "parallel"),
        name="retention",
    )(z3, z3, z3, z3, cos_tab, sin_tab, lgq, lgv)


def _rotary_tables(t):
    half = RET_QK // 2
    inv = ROPE_BASE ** (-jnp.arange(half, dtype=F32) / half)
    ang = jnp.arange(t, dtype=F32)[:, None] * inv[None, :]
    cos = jnp.cos(ang)
    sin = jnp.sin(ang)
    cos_tab = jnp.tile(cos, (1, 4))
    sin_tab = jnp.concatenate([-sin, -sin, sin, sin], axis=1)
    return cos_tab, sin_tab


def _pair_rotary_layout(a):
    lead = a.shape[:-1]
    a = a.reshape(lead + (RET_HEADS // 2, 2, 2, RET_QK // 2))
    return jnp.swapaxes(a, -3, -2).reshape(lead + (RET_HEADS * RET_QK,))


NAT_QROWS = 8
NAT_KROWS = 16
NAT_KCOLS = 2 * NAT_WIN_COLS


def _nat_kstart(j):
    return int(np.clip(j * NAT_WIN_COLS - NAT_WIN_COLS // 2, 0, GRID_W - NAT_KCOLS))


def _nat_key_row_base(g, rows):
    return int(np.clip(g * NAT_QROWS - NAT_WIN_ROWS // 2, 0, rows - NAT_KROWS))


def _edge_class(i, n):
    return 0 if i == 0 else (2 if i == n - 1 else 1)


def _nat_bias_tables(rpb, rows):
    n_g = rows // NAT_QROWS
    n_cb = GRID_W // NAT_WIN_COLS

    def row_geometry(g):
        dr = np.zeros((NAT_QROWS, NAT_KROWS), np.int32)
        ok = np.zeros((NAT_QROWS, NAT_KROWS), bool)
        kb = _nat_key_row_base(g, rows)
        for rr in range(NAT_QROWS):
            r = g * NAT_QROWS + rr
            rs = int(np.clip(r - NAT_WIN_ROWS // 2, 0, rows - NAT_WIN_ROWS))
            for kr in range(NAT_KROWS):
                ka = kb + kr
                ok[rr, kr] = rs <= ka < rs + NAT_WIN_ROWS
                dr[rr, kr] = int(np.clip(ka - r + NAT_WIN_ROWS - 1, 0, 2 * NAT_WIN_ROWS - 2))
        return dr, ok

    def col_geometry(j):
        dc = np.zeros((NAT_WIN_COLS, NAT_KCOLS), np.int32)
        ok = np.zeros((NAT_WIN_COLS, NAT_KCOLS), bool)
        ks = _nat_kstart(j)
        for qq in range(NAT_WIN_COLS):
            qc = j * NAT_WIN_COLS + qq
            ws = int(np.clip(qc - NAT_WIN_COLS // 2, 0, GRID_W - NAT_WIN_COLS))
            for kc in range(NAT_KCOLS):
                ka = ks + kc
                ok[qq, kc] = ws <= ka < ws + NAT_WIN_COLS
                dc[qq, kc] = int(np.clip(ka - qc + NAT_WIN_COLS - 1, 0, 2 * NAT_WIN_COLS - 2))
        return dc, ok

    def by_class(geometry, n):
        reps = {}
        for i in range(n):
            dx, ok = geometry(i)
            cls = _edge_class(i, n)
            if cls in reps:
                assert (np.where(ok, dx, -1) == np.where(reps[cls][1], reps[cls][0], -1)).all()
            else:
                reps[cls] = (dx, ok)
        filled = [reps.get(cls, reps[0]) for cls in range(3)]
        return np.stack([f[0] for f in filled]), np.stack([f[1] for f in filled])

    dr, row_ok = by_class(row_geometry, n_g)
    dc, col_ok = by_class(col_geometry, n_cb)
    row_sel = (dr[..., None] == np.arange(2 * NAT_WIN_ROWS - 1)).astype(np.float32)
    col_sel = (dc[..., None] == np.arange(2 * NAT_WIN_COLS - 1)).astype(np.float32)
    by_row = jnp.einsum('grka,hab->hgrkb', row_sel, rpb.astype(F32), precision=lax.Precision.HIGHEST)
    bias = jnp.einsum('hgrkb,jqcb->hgjrqkc', by_row, col_sel, precision=lax.Precision.HIGHEST)
    nq = NAT_QROWS * NAT_WIN_COLS
    nk = NAT_KROWS * NAT_KCOLS
    ok = (row_ok[:, None, :, None, :, None] & col_ok[None, :, None, :, None, :]).reshape(3, 3, nq, nk)
    bias = bias.reshape(NAT_HEADS // 2, 2, 3, 3, nq, nk)
    return jnp.where(ok, bias, -jnp.inf)


def _nat_kernel(q_ref, k_ref, v_ref, g_ref, bias_ref, out_ref):
    t = q_ref.shape[0]
    rows = t // GRID_W
    n_g = rows // NAT_QROWS
    n_cb = GRID_W // NAT_WIN_COLS
    nq = NAT_QROWS * NAT_WIN_COLS
    lane = lax.broadcasted_iota(jnp.int32, (1, 2 * NAT_HD), 1)
    head0 = lane < NAT_HD

    def group_body(g, carry):
        kb = jnp.clip(g * NAT_QROWS - NAT_WIN_ROWS // 2, 0, rows - NAT_KROWS)
        g_cls = jnp.where(g == 0, 0, jnp.where(g == n_g - 1, 2, 1))
        for j in range(n_cb):
            ks = _nat_kstart(j)

            def qrows(ref, rr):
                start = pl.multiple_of((g * NAT_QROWS + rr) * GRID_W + j * NAT_WIN_COLS, NAT_WIN_COLS)
                return ref[pl.ds(start, NAT_WIN_COLS), :]

            def krows(ref, kr):
                start = pl.multiple_of((kb + kr) * GRID_W + ks, SUBLANES)
                return ref[pl.ds(start, NAT_KCOLS), :]

            qs = jnp.concatenate([qrows(q_ref, rr) for rr in range(NAT_QROWS)], axis=0) * (NAT_HD ** -0.5)
            q2 = jnp.concatenate([jnp.where(head0, qs, 0.0), jnp.where(head0, 0.0, qs)], axis=0)
            kblk = jnp.concatenate([krows(k_ref, kr) for kr in range(NAT_KROWS)], axis=0)
            vblk = jnp.concatenate([krows(v_ref, kr) for kr in range(NAT_KROWS)], axis=0)
            s = _dot_nt(q2, kblk)
            j_cls = _edge_class(j, n_cb)
            s = s + jnp.concatenate([bias_ref[0, g_cls, j_cls], bias_ref[1, g_cls, j_cls]], axis=0)
            m = jnp.max(s, axis=-1, keepdims=True)
            p = jnp.exp(s - m)
            vext = jnp.concatenate([vblk, jnp.ones_like(vblk)], axis=1)
            o_ext = _dot(p, vext)
            o2 = o_ext[:, :2 * NAT_HD] / o_ext[:, 2 * NAT_HD:]
            o = jnp.where(head0, o2[:nq], o2[nq:])
            gs = jnp.concatenate([qrows(g_ref, rr) for rr in range(NAT_QROWS)], axis=0)
            res = (o * _silu(gs)).astype(out_ref.dtype)
            for rr in range(NAT_QROWS):
                start = pl.multiple_of((g * NAT_QROWS + rr) * GRID_W + j * NAT_WIN_COLS, NAT_WIN_COLS)
                out_ref[pl.ds(start, NAT_WIN_COLS), :] = res[rr * NAT_WIN_COLS:(rr + 1) * NAT_WIN_COLS]
        return carry

    lax.fori_loop(0, n_g, group_body, 0, unroll=4)


def _nat(z3, bias_tab):
    b, t, _ = z3.shape
    n_hp = NAT_HEADS // 2
    w = 2 * NAT_HD
    return pl.pallas_call(
        _nat_kernel,
        grid=(n_hp, b),
        in_specs=[
            _zspec(t, w, OFF_NQ, 1),
            _zspec(t, w, OFF_NK, 1),
            _zspec(t, w, OFF_NV, 1),
            _zspec(t, w, OFF_NG, 1),
            pl.BlockSpec((None,) + bias_tab.shape[1:], lambda h, i: (h, 0, 0, 0, 0, 0)),
        ],
        out_specs=pl.BlockSpec((None, t, w), lambda h, i: (i, 0, h)),
        out_shape=jax.ShapeDtypeStruct((b, t, BRANCH_W), BF16),
        compiler_params=_params("parallel", "parallel"),
        name="nat",
    )(z3, z3, z3, z3, bias_tab)


def _rglru_kernel(x_ref, g_ref, cw_ref, cb_ref, wg_ref, bg_ref, lam_ref, out_ref,
                  xp_scr, af_scr, bf_scr, ab_scr, bb_scr, hf_scr, pf_scr, hb_scr, pb_scr):
    t = x_ref.shape[0]
    n_seg = SUBLANES
    tc = t // n_seg
    pad = SUBLANES
    w = LANES

    xp_scr[0:pad, :] = jnp.zeros((pad, w), F32)
    xp_scr[pad + t:pad + t + pad, :] = jnp.zeros((pad, w), F32)
    xp_scr[pad:pad + t, :] = x_ref[...]

    lam = lam_ref[...]
    neg = -lam
    softplus = jnp.maximum(neg, 0.0) + jnp.log(1.0 + jnp.exp(-jnp.abs(neg)))
    log2_a_per_r = softplus * float(-LRU_C * np.log2(np.e))
    cw = cw_ref[...]
    cb = cb_ref[...]
    bg = bg_ref[...]

    def gate_body(n, carry):
        t0 = pl.multiple_of(n * tc, tc)
        xx = xp_scr[pl.ds(t0, tc + 2 * pad), :]
        total = tc + 2 * pad
        xc = cb
        for j in range(LRU_CONV):
            shift = LRU_CONV // 2 - j
            xs = xx if shift == 0 else pltpu.roll(xx, shift % total, 0)
            xc = xc + xs[pad:pad + tc] * cw[j:j + 1, :]
        gates = _dot(xc, wg_ref[...]) + bg
        seg_rows = pl.ds(n, tc, stride=n_seg)
        for d, (a_scr, b_scr) in enumerate(((af_scr, bf_scr), (ab_scr, bb_scr))):
            r = _sigmoid(gates[:, (2 * d) * w:(2 * d + 1) * w])
            i = _sigmoid(gates[:, (2 * d + 1) * w:(2 * d + 2) * w])
            a = jnp.exp2(r * log2_a_per_r[d:d + 1, :])
            a_scr[seg_rows, :] = a
            gap = 1.0 - a * a
            b_scr[seg_rows, :] = gap * lax.rsqrt(jnp.maximum(gap, 1e-30)) * (i * xc)
        return carry

    lax.fori_loop(0, n_seg, gate_body, 0, unroll=2)

    def scan_body(i, carry):
        hf, pf, hb, pb = carry
        sl = pl.ds(pl.multiple_of(i * n_seg, n_seg), n_seg)
        a = af_scr[sl, :]
        hf = a * hf + bf_scr[sl, :]
        pf = a * pf
        hf_scr[sl, :] = hf
        pf_scr[sl, :] = pf
        slb = pl.ds(pl.multiple_of((tc - 1 - i) * n_seg, n_seg), n_seg)
        a = ab_scr[slb, :]
        hb = a * hb + bb_scr[slb, :]
        pb = a * pb
        hb_scr[slb, :] = hb
        pb_scr[slb, :] = pb
        return hf, pf, hb, pb

    zero = jnp.zeros((n_seg, w), F32)
    one = jnp.ones((n_seg, w), F32)
    hf, pf, hb, pb = lax.fori_loop(0, tc, scan_body, (zero, one, zero, one), unroll=8)

    row = lax.broadcasted_iota(jnp.int32, (n_seg, w), 0)
    for s in (1, 2, 4):
        keep = row >= s
        hf = jnp.where(keep, pf * pltpu.roll(hf, s, 0) + hf, hf)
        pf = jnp.where(keep, pf * pltpu.roll(pf, s, 0), pf)
        keep = row < n_seg - s
        hb = jnp.where(keep, pb * pltpu.roll(hb, n_seg - s, 0) + hb, hb)
        pb = jnp.where(keep, pb * pltpu.roll(pb, n_seg - s, 0), pb)
    carry_f = jnp.where(row >= 1, pltpu.roll(hf, 1, 0), 0.0)
    carry_b = jnp.where(row < n_seg - 1, pltpu.roll(hb, n_seg - 1, 0), 0.0)

    def fix_body(i, carry):
        sl = pl.ds(pl.multiple_of(i * n_seg, n_seg), n_seg)
        af_scr[sl, :] = (hf_scr[sl, :] + pf_scr[sl, :] * carry_f) + (hb_scr[sl, :] + pb_scr[sl, :] * carry_b)
        return carry

    lax.fori_loop(0, tc, fix_body, 0, unroll=8)

    def out_body(n, carry):
        h = af_scr[pl.ds(n, tc, stride=n_seg), :]
        sl = pl.ds(pl.multiple_of(n * tc, tc), tc)
        out_ref[sl, :] = (h * _silu(g_ref[sl, :])).astype(out_ref.dtype)
        return carry

    lax.fori_loop(0, n_seg, out_body, 0)


def _rglru(z3, conv_w, conv_b, wg, bg, lam):
    b, t, _ = z3.shape
    n_cb = BRANCH_W // LANES
    return pl.pallas_call(
        _rglru_kernel,
        grid=(b, n_cb),
        in_specs=[
            _zspec(t, LANES, OFF_LX, 0),
            _zspec(t, LANES, OFF_LG, 0),
            pl.BlockSpec((LRU_CONV, LANES), lambda i, h: (0, h)),
            pl.BlockSpec((1, LANES), lambda i, h: (0, h)),
            pl.BlockSpec((None, LANES, 4 * LANES), lambda i, h: (h, 0, 0)),
            pl.BlockSpec((None, 1, 4 * LANES), lambda i, h: (h, 0, 0)),
            pl.BlockSpec((2, LANES), lambda i, h: (0, h)),
        ],
        out_specs=pl.BlockSpec((None, t, LANES), lambda i, h: (i, 0, h)),
        out_shape=jax.ShapeDtypeStruct((b, t, BRANCH_W), BF16),
        scratch_shapes=[pltpu.VMEM((t + 2 * SUBLANES, LANES), F32)] + [pltpu.VMEM((t, LANES), F32)] * 8,
        compiler_params=_params("parallel", "parallel"),
        name="rglru",
    )(z3, z3, conv_w, conv_b.reshape(1, BRANCH_W), wg, bg, lam)


def _rglru_gate_weights(wa, ba, wx, bx):
    n_cb = BRANCH_W // LANES
    per = LANES // LRU_BW

    def blockdiag(w):
        w = w.reshape(n_cb, per, LRU_BW, LRU_BW)
        eye = jnp.eye(per, dtype=w.dtype)
        return jnp.einsum('cpjk,pq->cpjqk', w, eye).reshape(n_cb, LANES, LANES)

    wg = jnp.concatenate([blockdiag(wa[0]), blockdiag(wx[0]), blockdiag(wa[1]), blockdiag(wx[1])], axis=-1)
    bg = jnp.concatenate([ba[0].reshape(n_cb, 1, LANES), bx[0].reshape(n_cb, 1, LANES),
                          ba[1].reshape(n_cb, 1, LANES), bx[1].reshape(n_cb, 1, LANES)], axis=-1)
    return wg.astype(BF16), bg.astype(F32)


def _prefix_sum_rows(x):
    c, w = x.shape
    x3 = x.reshape(c // SUBLANES, SUBLANES, w)
    row = lax.broadcasted_iota(jnp.int32, (1, SUBLANES, w), 1)
    s = 1
    while s < SUBLANES:
        x3 = x3 + jnp.where(row >= s, pltpu.roll(x3, s, 1), 0.0)
        s *= 2
    tile_tot = jnp.broadcast_to(x3[:, SUBLANES - 1:, :], x3.shape).reshape(c, w)
    x = x3.reshape(c, w)
    while s < c:
        shifted = jnp.concatenate([jnp.zeros((s, w), x.dtype), tile_tot[:c - s]], axis=0)
        x = x + shifted
        tile_tot = tile_tot + shifted
        s *= 2
    return x


def _hgrn_forget(zf, lb):
    f = lb + (1.0 - lb) * _sigmoid(zf)
    return f, jnp.log2(f)


def _hgrn_tile_decays(size, f3, fb3, bs3, cs3, pos):
    half = size // 2
    if size == 2:
        return jnp.where(pos == 1, f3, 1.0), jnp.where(pos == 0, fb3, 1.0)
    if size == 4:
        f_prev, f_next = pltpu.roll(f3, 1, 1), pltpu.roll(f3, SUBLANES - 1, 1)
        fb_prev, fb_next = pltpu.roll(fb3, 1, 1), pltpu.roll(fb3, SUBLANES - 1, 1)
        e_f = jnp.where(pos == 0, f_next, jnp.where(pos == 1, 1.0, jnp.where(pos == 2, f3, f_prev * f3)))
        e_b = jnp.where(pos == 0, fb3 * fb_next, jnp.where(pos == 1, fb3, jnp.where(pos == 2, 1.0, fb_prev)))
        return e_f, e_b
    sign = jnp.where(pos >= half, 1.0, -1.0)
    e_f = jnp.exp2((bs3 - bs3[:, half - 1:half, :]) * sign)
    e_b = jnp.exp2((cs3[:, half:half + 1, :] - cs3) * sign)
    return e_f, e_b


def _hgrn_level_operands(size, q, kf, kb, f, fb, bs, cs):
    c, w = q.shape
    half = size // 2
    if size <= SUBLANES:
        tiled = lambda a: a.reshape(c // SUBLANES, SUBLANES, w)
        pos = lax.broadcasted_iota(jnp.int32, (1, SUBLANES, w), 1) % size
        upper = pos >= half
        e_f, e_b = _hgrn_tile_decays(size, tiled(f), tiled(fb), tiled(bs), tiled(cs), pos)
        z_f = e_f * jnp.where(upper, tiled(q), tiled(kf))
        z_b = e_b * jnp.where(upper, tiled(kb), tiled(q))
        x = jnp.concatenate([jnp.where(upper, z_f, 0.0), jnp.where(upper, 0.0, z_b)], axis=2)
        y = jnp.concatenate([z_f, z_b], axis=2)
        return x.reshape(c, 2 * w), y.reshape(c, 2 * w)
    xs, ys = [], []
    zero = jnp.zeros((half, w), F32)
    for i in range(c // half):
        rows = slice(i * half, (i + 1) * half)
        if i % 2 == 0:
            ref_f = bs[(i + 1) * half - 1:(i + 1) * half]
            ref_b = cs[(i + 1) * half:(i + 1) * half + 1]
            z_f = kf[rows] * jnp.exp2(ref_f - bs[rows])
            z_b = q[rows] * jnp.exp2(cs[rows] - ref_b)
            xs.append(jnp.concatenate([zero, z_b], axis=1))
        else:
            ref_f = bs[i * half - 1:i * half]
            ref_b = cs[i * half:i * half + 1]
            z_f = q[rows] * jnp.exp2(bs[rows] - ref_f)
            z_b = kb[rows] * jnp.exp2(ref_b - cs[rows])
            xs.append(jnp.concatenate([z_f, zero], axis=1))
        ys.append(jnp.concatenate([z_f, z_b], axis=1))
    return jnp.concatenate(xs, axis=0), jnp.concatenate(ys, axis=0)


def _hgrn_kernel(q_ref, ff_ref, fb_ref, v_ref, g_ref, lbl_ref, gain_ref, out_ref,
                 fb_scr, cs_scr, sb_scr, st_scr, *, layer):
    t = q_ref.shape[0]
    c = HGRN_CHUNK
    n_chunks = t // c
    w = HGRN_DK

    logits = lbl_ref[...]
    mx = jnp.max(logits, axis=0)
    ex = jnp.exp(logits - mx[None])
    tot = jnp.sum(ex, axis=0)
    lb = jnp.zeros_like(tot)
    for i in range(1, layer + 1):
        lb = lb + ex[i] / tot
    lb_f = lb[0:1, :]
    lb_b = lb[1:2, :]

    pair_xor = lax.broadcasted_iota(jnp.int32, (c, c), 0) ^ lax.broadcasted_iota(jnp.int32, (c, c), 1)
    pair_level = jnp.zeros((c, c), jnp.int32)
    size = 2
    while size <= c:
        pair_level = jnp.where(pair_xor >= size // 2, size, pair_level)
        size *= 2

    st_scr[...] = jnp.zeros_like(st_scr)

    def bwd_body(i, carry):
        n = n_chunks - 1 - i
        sl = pl.ds(pl.multiple_of(n * c, c), c)
        fb, gb = _hgrn_forget(fb_ref[sl, :], lb_b)
        pre = _prefix_sum_rows(gb)
        total = pre[c - 1:c, :]
        cs = total - pre + gb
        fb_scr[sl, :] = fb
        cs_scr[sl, :] = cs
        sb_scr[n] = st_scr[...]
        st_scr[...] = st_scr[...] * jnp.exp2(total) + _dot_tn(v_ref[sl, :], (1.0 - fb) * jnp.exp2(total - cs))
        return carry

    lax.fori_loop(0, n_chunks, bwd_body, 0, unroll=8)

    st_scr[...] = jnp.zeros_like(st_scr)

    def fwd_body(n, carry):
        sl = pl.ds(pl.multiple_of(n * c, c), c)
        q = _silu(q_ref[sl, :])
        v = v_ref[sl, :]
        f, gf = _hgrn_forget(ff_ref[sl, :], lb_f)
        kf = 1.0 - f
        bs = _prefix_sum_rows(gf)
        fb = fb_scr[sl, :]
        kb = 1.0 - fb
        cs = cs_scr[sl, :]

        att = None
        size = 2
        while size <= c:
            x, y = _hgrn_level_operands(size, q, kf, kb, f, fb, bs, cs)
            att = jnp.where(pair_level == size, _dot_nt(x, y), 0.0 if att is None else att)
            size *= 2

        diag = jnp.sum(q * (kf + kb), axis=-1, keepdims=True)
        o = _dot(att, v) + diag * v
        inter = jnp.concatenate([q * jnp.exp2(bs), q * jnp.exp2(cs)], axis=1)
        states = jnp.concatenate([st_scr[...], sb_scr[n]], axis=1)
        o = o + _dot_nt(inter, states)
        last = bs[c - 1:c, :]
        st_scr[...] = st_scr[...] * jnp.exp2(last) + _dot_tn(v, kf * jnp.exp2(last - bs))
        o = o * lax.rsqrt(jnp.mean(o * o, axis=-1, keepdims=True) + EPS) * gain_ref[...]
        out_ref[sl, :] = (o * _silu(g_ref[sl, :])).astype(out_ref.dtype)
        return carry

    lax.fori_loop(0, n_chunks, fwd_body, 0, unroll=8)


def _hgrn(z3, lb_logits, gain, layer):
    b, t, _ = z3.shape
    depth = lb_logits.shape[0]
    w = HGRN_DK
    n_chunks = t // HGRN_CHUNK
    return pl.pallas_call(
        functools.partial(_hgrn_kernel, layer=layer),
        grid=(b, HGRN_HEADS),
        in_specs=[
            _zspec(t, w, OFF_HQ, 0),
            _zspec(t, w, OFF_HFF, 0),
            _zspec(t, w, OFF_HFB, 0),
            _zspec(t, w, OFF_HI, 0),
            _zspec(t, w, OFF_HG, 0),
            pl.BlockSpec((depth, 2, w), lambda i, h: (0, 0, h)),
            pl.BlockSpec((1, w), lambda i, h: (0, h)),
        ],
        out_specs=pl.BlockSpec((None, t, w), lambda i, h: (i, 0, h)),
        out_shape=jax.ShapeDtypeStruct((b, t, BRANCH_W), BF16),
        scratch_shapes=[
            pltpu.VMEM((t, w), F32),
            pltpu.VMEM((t, w), F32),
            pltpu.VMEM((n_chunks, w, w), F32),
            pltpu.VMEM((w, w), F32),
        ],
        compiler_params=_params("parallel", "parallel"),
        name="hgrn",
    )(z3, z3, z3, z3, z3, lb_logits, gain.reshape(1, -1))


def _encoder(x, p, w):
    b, t, _ = x.shape
    depth = w['w_in'].shape[0]
    x2d = x.reshape(b * t, D_MODEL)
    for l in range(depth):
        z3 = _inproj(x2d, w['norm_mix'][l], w['w_in'][l]).reshape(b, t, W_IN)
        br_a = _retention(z3, w['cos'], w['sin'], w['lgq'][l], w['lgv'][l])
        br_b = _nat(z3, w['nat_bias'][l])
        br_c = _rglru(z3, w['conv_w'][l], w['conv_b'][l], w['lru_wg'][l], w['lru_bg'][l], w['lam'][l])
        br_d = _hgrn(z3, w['lb_logits'], w['hgrn_gain'][l], l)
        branches = [a.reshape(b * t, BRANCH_W) for a in (br_a, br_b, br_c, br_d)]
        x2d = _merge(x2d, branches, p[l].reshape(b * t, PLE_DIM), w['norm_mix'][l], w['w_merge'][l],
                     w['w_branch'][l], w['w_out'][l], w['ple_norm'][l], w['w_ple_gate'][l],
                     w['w_ple_proj'][l], w['final_norm'], l == depth - 1)
    return x2d.reshape(b, t, D_MODEL)


def kernel(x_prompt, x_sample, p_prompt, p_sample, norm_mix, w_in, ret_decay_logit, nat_rpb, lru_conv_w,
           lru_conv_b, lru_wa, lru_ba, lru_wx, lru_bx, lru_lambda, hgrn_lb_logits, hgrn_norm, w_branch,
           w_merge, w_out, ple_norm, w_ple_gate, w_ple_proj, final_norm):
    depth = w_in.shape[0]
    t = x_prompt.shape[1]
    rows = t // GRID_W
    cos_tab, sin_tab = _rotary_tables(t)
    gate_w = [_rglru_gate_weights(lru_wa[l], lru_ba[l], lru_wx[l], lru_bx[l]) for l in range(depth)]
    weights = {
        'norm_mix': norm_mix,
        'w_in': jnp.concatenate([_pair_rotary_layout(w_in[..., OFF_RQ:OFF_RK]),
                                 _pair_rotary_layout(w_in[..., OFF_RK:OFF_RV]),
                                 w_in[..., OFF_RV:]], axis=-1).astype(BF16),
        'cos': cos_tab,
        'sin': sin_tab,
        'lgq': _pair_rotary_layout(jnp.repeat(ret_decay_logit.astype(F32), RET_QK, axis=-1)),
        'lgv': jnp.repeat(ret_decay_logit.astype(F32), RET_V, axis=-1),
        'nat_bias': [_nat_bias_tables(nat_rpb[l], rows) for l in range(depth)],
        'conv_w': lru_conv_w,
        'conv_b': lru_conv_b,
        'lru_wg': [g[0] for g in gate_w],
        'lru_bg': [g[1] for g in gate_w],
        'lam': lru_lambda,
        'lb_logits': hgrn_lb_logits,
        'hgrn_gain': hgrn_norm,
        'w_branch': w_branch.astype(BF16),
        'w_merge': w_merge.astype(BF16),
        'w_out': w_out.astype(BF16),
        'ple_norm': ple_norm,
        'w_ple_gate': w_ple_gate.astype(BF16),
        'w_ple_proj': w_ple_proj.astype(BF16),
        'final_norm': final_norm,
    }
    y_prompt = _encoder(x_prompt, p_prompt, weights)
    y_sample = _encoder(x_sample, p_sample, weights)
    return (y_prompt, y_sample)
```

```python
import functools

import numpy as np
import jax
import jax.numpy as jnp
from jax import lax
from jax.experimental import pallas as pl
from jax.experimental.pallas import tpu as pltpu

F32 = jnp.float32
BF16 = jnp.bfloat16

D_MODEL = 1024
PLE_DIM = 256
GRID_W = 64
N_BRANCH = 4
BRANCH_W = 512
RET_HEADS = 4
RET_QK = 64
RET_V = 128
ROPE_BASE = 10000.0
NAT_HEADS = 8
NAT_HD = 64
NAT_WIN_ROWS = 8
NAT_WIN_COLS = 16
LRU_BLOCKS = 8
LRU_BW = 64
LRU_CONV = 4
LRU_C = 8.0
HGRN_HEADS = 4
HGRN_DK = 128
EPS = 1e-6
W_IN = 7168

OFF_RQ, OFF_RK, OFF_RV, OFF_RG = 0, 256, 512, 1024
OFF_NQ, OFF_NK, OFF_NV, OFF_NG = 1536, 2048, 2560, 3072
OFF_LX, OFF_LG = 3584, 4096
OFF_HQ, OFF_HFF, OFF_HFB, OFF_HI, OFF_HG = 4608, 5120, 5632, 6144, 6656

LANES = 128
SUBLANES = 8
VMEM_LIMIT = 56 * 1024 * 1024

CHUNK = 128
HGRN_CHUNK = 128
HGRN_GATE_CHUNKS = 8
GATE_PIECE_COLS = 256


def _params(*sem):
    return pltpu.CompilerParams(dimension_semantics=sem, vmem_limit_bytes=VMEM_LIMIT)


def _dot(a, b):
    return jnp.dot(a.astype(BF16), b.astype(BF16), preferred_element_type=F32)


def _dot_nt(a, b):
    return lax.dot_general(a.astype(BF16), b.astype(BF16), (((1,), (1,)), ((), ())),
                           preferred_element_type=F32)


def _dot_tn(a, b):
    return lax.dot_general(a.astype(BF16), b.astype(BF16), (((0,), (0,)), ((), ())),
                           preferred_element_type=F32)


def _rms(x, g):
    return x * lax.rsqrt(jnp.mean(x * x, axis=-1, keepdims=True) + EPS) * g


def _sigmoid(x):
    return jax.nn.sigmoid(x)


def _silu(x):
    return x * jax.nn.sigmoid(x)


def _zspec(t, width, off, grid_pos):
    base = off // width
    if grid_pos == 0:
        return pl.BlockSpec((None, t, width), lambda b, h: (b, 0, base + h))
    return pl.BlockSpec((None, t, width), lambda h, b: (b, 0, base + h))


def _inproj_kernel(x_ref, g_ref, w_ref, z_ref, h_ref):
    @pl.when(pl.program_id(1) == 0)
    def _():
        h_ref[...] = _rms(x_ref[...], g_ref[...]).astype(h_ref.dtype)

    z_ref[...] = jnp.dot(h_ref[...], w_ref[...], preferred_element_type=F32)


def _inproj(x2d, g, w_bf16):
    m = x2d.shape[0]
    tm, tn = 1024, 1792
    return pl.pallas_call(
        _inproj_kernel,
        grid=(m // tm, W_IN // tn),
        in_specs=[
            pl.BlockSpec((tm, D_MODEL), lambda i, j: (i, 0)),
            pl.BlockSpec((1, D_MODEL), lambda i, j: (0, 0)),
            pl.BlockSpec((D_MODEL, tn), lambda i, j: (0, j)),
        ],
        out_specs=[
            pl.BlockSpec((tm, tn), lambda i, j: (i, j)),
            pl.BlockSpec((tm, D_MODEL), lambda i, j: (i, 0)),
        ],
        out_shape=[
            jax.ShapeDtypeStruct((m, W_IN), F32),
            jax.ShapeDtypeStruct((m, D_MODEL), BF16),
        ],
        compiler_params=_params("parallel", "arbitrary"),
        name="inproj",
    )(x2d, g.reshape(1, D_MODEL), w_bf16)


def _merge_kernel(x_ref, ba_ref, bb_ref, bc_ref, bd_ref, gates_ref, p_ref, wbr_ref, wo_ref,
                  gple_ref, wpg_ref, wpp_ref, gfin_ref, out_ref, *, final):
    x = x_ref[...]
    merged = None
    for j, b_ref in enumerate((ba_ref, bb_ref, bc_ref, bd_ref)):
        gate = _sigmoid(gates_ref[:, j * D_MODEL:(j + 1) * D_MODEL].astype(F32))
        term = gate * jnp.dot(b_ref[...], wbr_ref[j], preferred_element_type=F32)
        merged = term if merged is None else merged + term
    x1 = x + _dot(merged, wo_ref[...])
    gate2 = _sigmoid(_dot(_rms(x1, gple_ref[...]), wpg_ref[...]))
    x2 = x1 + gate2 * _dot(p_ref[...], wpp_ref[...])
    if final:
        x2 = _rms(x2, gfin_ref[...])
    out_ref[...] = x2


def _merge(x2d, branches, gates, p2d, wbr, wo, gple, wpg, wpp, gfin, final):
    m = x2d.shape[0]
    tm = 512
    row = lambda i: (i, 0)
    const2 = lambda i: (0, 0)
    const3 = lambda i: (0, 0, 0)
    once = pl.Buffered(1)
    vec = pl.BlockSpec((1, D_MODEL), const2)
    return pl.pallas_call(
        functools.partial(_merge_kernel, final=final),
        grid=(m // tm,),
        in_specs=[
            pl.BlockSpec((tm, D_MODEL), row),
            pl.BlockSpec((tm, BRANCH_W), row),
            pl.BlockSpec((tm, BRANCH_W), row),
            pl.BlockSpec((tm, BRANCH_W), row),
            pl.BlockSpec((tm, BRANCH_W), row),
            pl.BlockSpec((tm, N_BRANCH * D_MODEL), row),
            pl.BlockSpec((tm, PLE_DIM), row),
            pl.BlockSpec((N_BRANCH, BRANCH_W, D_MODEL), const3, pipeline_mode=once),
            pl.BlockSpec((D_MODEL, D_MODEL), const2, pipeline_mode=once),
            vec,
            pl.BlockSpec((D_MODEL, D_MODEL), const2, pipeline_mode=once),
            pl.BlockSpec((PLE_DIM, D_MODEL), const2, pipeline_mode=once),
            vec,
        ],
        out_specs=pl.BlockSpec((tm, D_MODEL), row),
        out_shape=jax.ShapeDtypeStruct((m, D_MODEL), F32),
        compiler_params=_params("parallel"),
        name="merge",
    )(x2d, *branches, gates, p2d, wbr, wo, gple.reshape(1, -1), wpg, wpp, gfin.reshape(1, -1))


def _retention_kernel(q_ref, k_ref, v_ref, g_ref, cos_ref, sin_ref, lgq_ref, lgv_ref, out_ref,
                      kr_scr, sb_scr, sf_scr):
    t = q_ref.shape[0]
    c = CHUNK
    n_chunks = t // c
    hd = RET_QK

    lane = lax.broadcasted_iota(jnp.int32, (1, 2 * hd), 1)
    head0_q = (lane // (hd // 2)) % 2 == 0
    lane_v = lax.broadcasted_iota(jnp.int32, (1, 2 * RET_V), 1)
    head0_v = lane_v < RET_V

    def rotary(x, cos, sin):
        return x * cos + pltpu.roll(x, hd, 1) * sin

    lg_f = -jnp.log(1.0 + jnp.exp(-lgq_ref[0:1, :]))
    lg_b = -jnp.log(1.0 + jnp.exp(-lgq_ref[1:2, :]))
    lgv_f = -jnp.log(1.0 + jnp.exp(-lgv_ref[0:1, :]))
    lgv_b = -jnp.log(1.0 + jnp.exp(-lgv_ref[1:2, :]))
    tcol = lax.broadcasted_iota(jnp.int32, (c, 1), 0).astype(F32)
    head_f = jnp.exp((tcol + 1.0) * lg_f)
    head_b = jnp.exp((c - tcol) * lg_b)
    tail_f = jnp.exp((c - 1.0 - tcol) * lg_f)
    tail_b = jnp.exp(tcol * lg_b)
    dec_f = jnp.exp(c * lgv_f)
    dec_b = jnp.exp(c * lgv_b)
    rowk = lax.broadcasted_iota(jnp.int32, (2 * hd, 2 * RET_V), 0)
    colv = lax.broadcasted_iota(jnp.int32, (2 * hd, 2 * RET_V), 1)
    blockdiag = ((rowk // (hd // 2)) % 2) == (colv // RET_V)

    diff = (lax.broadcasted_iota(jnp.int32, (c, c), 0) - lax.broadcasted_iota(jnp.int32, (c, c), 1)).astype(F32)

    def decay_mask(lf, lb):
        fwd = jnp.exp(jnp.maximum(diff, 0.0) * lf)
        bwd = jnp.exp(jnp.maximum(-diff, 0.0) * lb)
        return jnp.where(diff > 0, fwd, jnp.where(diff < 0, bwd, 2.0))

    dmask = jnp.concatenate([decay_mask(lgv_f[:, :c], lgv_b[:, :c]),
                             decay_mask(lgv_f[:, RET_V:RET_V + c], lgv_b[:, RET_V:RET_V + c])], axis=1)

    sf_scr[...] = jnp.zeros_like(sf_scr)

    def bwd_body(i, carry):
        n = n_chunks - 1 - i
        sl = pl.ds(pl.multiple_of(n * c, c), c)
        sb_scr[n] = sf_scr[...]
        kn = rotary(k_ref[sl, :], cos_ref[sl, :], sin_ref[sl, :]) * (hd ** -0.5)
        kr_scr[sl, :] = kn
        loc = _dot_tn(kn * tail_b, v_ref[sl, :])
        sf_scr[...] = sf_scr[...] * dec_b + jnp.where(blockdiag, loc, 0.0)
        return carry

    lax.fori_loop(0, n_chunks, bwd_body, 0, unroll=8)

    sf_scr[...] = jnp.zeros_like(sf_scr)

    def fwd_body(n, carry):
        sl = pl.ds(pl.multiple_of(n * c, c), c)
        qn = rotary(q_ref[sl, :], cos_ref[sl, :], sin_ref[sl, :])
        kn = kr_scr[sl, :]
        vn = v_ref[sl, :]
        kstack = jnp.concatenate([jnp.where(head0_q, kn, 0.0), jnp.where(head0_q, 0.0, kn)], axis=0)
        scores = _dot_nt(qn, kstack) * dmask
        vstack = jnp.concatenate([jnp.where(head0_v, vn, 0.0), jnp.where(head0_v, 0.0, vn)], axis=0)
        lhs = jnp.concatenate([scores, qn * head_f, qn * head_b], axis=1)
        rhs = jnp.concatenate([vstack, sf_scr[...], sb_scr[n]], axis=0)
        o = _dot(lhs, rhs)
        loc = _dot_tn(kn * tail_f, vn)
        sf_scr[...] = sf_scr[...] * dec_f + jnp.where(blockdiag, loc, 0.0)
        o0 = o[:, :RET_V]
        o1 = o[:, RET_V:]
        o0 = o0 * lax.rsqrt(jnp.mean(o0 * o0, axis=-1, keepdims=True) + EPS)
        o1 = o1 * lax.rsqrt(jnp.mean(o1 * o1, axis=-1, keepdims=True) + EPS)
        on = jnp.concatenate([o0, o1], axis=1)
        out_ref[sl, :] = (on * _silu(g_ref[sl, :])).astype(out_ref.dtype)
        return carry

    lax.fori_loop(0, n_chunks, fwd_body, 0, unroll=8)


def _retention(z3, cos_tab, sin_tab, lgq, lgv):
    b, t, _ = z3.shape
    n_chunks = t // CHUNK
    return pl.pallas_call(
        _retention_kernel,
        grid=(b, RET_HEADS // 2),
        in_specs=[
            _zspec(t, 2 * RET_QK, OFF_RQ, 0),
            _zspec(t, 2 * RET_QK, OFF_RK, 0),
            _zspec(t, 2 * RET_V, OFF_RV, 0),
            _zspec(t, 2 * RET_V, OFF_RG, 0),
            pl.BlockSpec((t, 2 * RET_QK), lambda i, h: (0, 0)),
            pl.BlockSpec((t, 2 * RET_QK), lambda i, h: (0, 0)),
            pl.BlockSpec((2, 2 * RET_QK), lambda i, h: (0, h)),
            pl.BlockSpec((2, 2 * RET_V), lambda i, h: (0, h)),
        ],
        out_specs=pl.BlockSpec((None, t, 2 * RET_V), lambda i, h: (i, 0, h)),
        out_shape=jax.ShapeDtypeStruct((b, t, BRANCH_W), BF16),
        scratch_shapes=[
            pltpu.VMEM((t, 2 * RET_QK), F32),
            pltpu.VMEM((n_chunks, 2 * RET_QK, 2 * RET_V), F32),
            pltpu.VMEM((2 * RET_QK, 2 * RET_V), F32),
        ],
        compiler_params=_params("parallel", "parallel"),
        name="retention",
    )(z3, z3, z3, z3, cos_tab, sin_tab, lgq, lgv)


def _rotary_tables(t):
    half = RET_QK // 2
    inv = ROPE_BASE ** (-jnp.arange(half, dtype=F32) / half)
    ang = jnp.arange(t, dtype=F32)[:, None] * inv[None, :]
    cos = jnp.cos(ang)
    sin = jnp.sin(ang)
    cos_tab = jnp.tile(cos, (1, 4))
    sin_tab = jnp.concatenate([-sin, -sin, sin, sin], axis=1)
    return cos_tab, sin_tab


def _pair_rotary_layout(a):
    lead = a.shape[:-1]
    a = a.reshape(lead + (RET_HEADS // 2, 2, 2, RET_QK // 2))
    return jnp.swapaxes(a, -3, -2).reshape(lead + (RET_HEADS * RET_QK,))


NAT_QROWS = 8
NAT_KROWS = 16
NAT_KCOLS = 2 * NAT_WIN_COLS


def _nat_kstart(j):
    return int(np.clip(j * NAT_WIN_COLS - NAT_WIN_COLS // 2, 0, GRID_W - NAT_KCOLS))


def _nat_key_row_base(g, rows):
    return int(np.clip(g * NAT_QROWS - NAT_WIN_ROWS // 2, 0, rows - NAT_KROWS))


def _edge_class(i, n):
    return 0 if i == 0 else (2 if i == n - 1 else 1)


def _nat_bias_tables(rpb, rows):
    n_g = rows // NAT_QROWS
    n_cb = GRID_W // NAT_WIN_COLS

    def row_geometry(g):
        dr = np.zeros((NAT_QROWS, NAT_KROWS), np.int32)
        ok = np.zeros((NAT_QROWS, NAT_KROWS), bool)
        kb = _nat_key_row_base(g, rows)
        for rr in range(NAT_QROWS):
            r = g * NAT_QROWS + rr
            rs = int(np.clip(r - NAT_WIN_ROWS // 2, 0, rows - NAT_WIN_ROWS))
            for kr in range(NAT_KROWS):
                ka = kb + kr
                ok[rr, kr] = rs <= ka < rs + NAT_WIN_ROWS
                dr[rr, kr] = int(np.clip(ka - r + NAT_WIN_ROWS - 1, 0, 2 * NAT_WIN_ROWS - 2))
        return dr, ok

    def col_geometry(j):
        dc = np.zeros((NAT_WIN_COLS, NAT_KCOLS), np.int32)
        ok = np.zeros((NAT_WIN_COLS, NAT_KCOLS), bool)
        ks = _nat_kstart(j)
        for qq in range(NAT_WIN_COLS):
            qc = j * NAT_WIN_COLS + qq
            ws = int(np.clip(qc - NAT_WIN_COLS // 2, 0, GRID_W - NAT_WIN_COLS))
            for kc in range(NAT_KCOLS):
                ka = ks + kc
                ok[qq, kc] = ws <= ka < ws + NAT_WIN_COLS
                dc[qq, kc] = int(np.clip(ka - qc + NAT_WIN_COLS - 1, 0, 2 * NAT_WIN_COLS - 2))
        return dc, ok

    def by_class(geometry, n):
        reps = {}
        for i in range(n):
            dx, ok = geometry(i)
            cls = _edge_class(i, n)
            if cls in reps:
                assert (np.where(ok, dx, -1) == np.where(reps[cls][1], reps[cls][0], -1)).all()
            else:
                reps[cls] = (dx, ok)
        filled = [reps.get(cls, reps[0]) for cls in range(3)]
        return np.stack([f[0] for f in filled]), np.stack([f[1] for f in filled])

    dr, row_ok = by_class(row_geometry, n_g)
    dc, col_ok = by_class(col_geometry, n_cb)
    row_sel = (dr[..., None] == np.arange(2 * NAT_WIN_ROWS - 1)).astype(np.float32)
    col_sel = (dc[..., None] == np.arange(2 * NAT_WIN_COLS - 1)).astype(np.float32)
    by_row = jnp.einsum('grka,hab->hgrkb', row_sel, rpb.astype(F32), precision=lax.Precision.HIGHEST)
    bias = jnp.einsum('hgrkb,jqcb->hgjrqkc', by_row, col_sel, precision=lax.Precision.HIGHEST)
    nq = NAT_QROWS * NAT_WIN_COLS
    nk = NAT_KROWS * NAT_KCOLS
    ok = (row_ok[:, None, :, None, :, None] & col_ok[None, :, None, :, None, :]).reshape(3, 3, nq, nk)
    bias = bias.reshape(NAT_HEADS // 2, 2, 3, 3, nq, nk)
    return jnp.where(ok, bias, -jnp.inf)


def _nat_kernel(q_ref, k_ref, v_ref, g_ref, bias_ref, out_ref):
    t = q_ref.shape[0]
    rows = t // GRID_W
    n_g = rows // NAT_QROWS
    n_cb = GRID_W // NAT_WIN_COLS
    nq = NAT_QROWS * NAT_WIN_COLS
    lane = lax.broadcasted_iota(jnp.int32, (1, 2 * NAT_HD), 1)
    head0 = lane < NAT_HD

    def group_body(g, carry):
        kb = jnp.clip(g * NAT_QROWS - NAT_WIN_ROWS // 2, 0, rows - NAT_KROWS)
        g_cls = jnp.where(g == 0, 0, jnp.where(g == n_g - 1, 2, 1))
        for j in range(n_cb):
            ks = _nat_kstart(j)

            def qrows(ref, rr):
                start = pl.multiple_of((g * NAT_QROWS + rr) * GRID_W + j * NAT_WIN_COLS, NAT_WIN_COLS)
                return ref[pl.ds(start, NAT_WIN_COLS), :]

            def krows(ref, kr):
                start = pl.multiple_of((kb + kr) * GRID_W + ks, SUBLANES)
                return ref[pl.ds(start, NAT_KCOLS), :]

            qs = jnp.concatenate([qrows(q_ref, rr) for rr in range(NAT_QROWS)], axis=0) * (NAT_HD ** -0.5)
            q2 = jnp.concatenate([jnp.where(head0, qs, 0.0), jnp.where(head0, 0.0, qs)], axis=0)
            kblk = jnp.concatenate([krows(k_ref, kr) for kr in range(NAT_KROWS)], axis=0)
            vblk = jnp.concatenate([krows(v_ref, kr) for kr in range(NAT_KROWS)], axis=0)
            s = _dot_nt(q2, kblk)
            j_cls = _edge_class(j, n_cb)
            s = s + jnp.concatenate([bias_ref[0, g_cls, j_cls], bias_ref[1, g_cls, j_cls]], axis=0)
            m = jnp.max(s, axis=-1, keepdims=True)
            p = jnp.exp(s - m)
            vext = jnp.concatenate([vblk, jnp.ones_like(vblk)], axis=1)
            o_ext = _dot(p, vext)
            o2 = o_ext[:, :2 * NAT_HD] / o_ext[:, 2 * NAT_HD:]
            o = jnp.where(head0, o2[:nq], o2[nq:])
            gs = jnp.concatenate([qrows(g_ref, rr) for rr in range(NAT_QROWS)], axis=0)
            res = (o * _silu(gs)).astype(out_ref.dtype)
            for rr in range(NAT_QROWS):
                start = pl.multiple_of((g * NAT_QROWS + rr) * GRID_W + j * NAT_WIN_COLS, NAT_WIN_COLS)
                out_ref[pl.ds(start, NAT_WIN_COLS), :] = res[rr * NAT_WIN_COLS:(rr + 1) * NAT_WIN_COLS]
        return carry

    lax.fori_loop(0, n_g, group_body, 0, unroll=4)


def _nat(z3, bias_tab):
    b, t, _ = z3.shape
    n_hp = NAT_HEADS // 2
    w = 2 * NAT_HD
    return pl.pallas_call(
        _nat_kernel,
        grid=(n_hp, b),
        in_specs=[
            _zspec(t, w, OFF_NQ, 1),
            _zspec(t, w, OFF_NK, 1),
            _zspec(t, w, OFF_NV, 1),
            _zspec(t, w, OFF_NG, 1),
            pl.BlockSpec((None,) + bias_tab.shape[1:], lambda h, i: (h, 0, 0, 0, 0, 0)),
        ],
        out_specs=pl.BlockSpec((None, t, w), lambda h, i: (i, 0, h)),
        out_shape=jax.ShapeDtypeStruct((b, t, BRANCH_W), BF16),
        compiler_params=_params("parallel", "parallel"),
        name="nat",
    )(z3, z3, z3, z3, bias_tab)


def _rglru_kernel(x_ref, g_ref, cw_ref, cb_ref, wg_ref, bg_ref, lam_ref, out_ref,
                  xp_scr, af_scr, bf_scr, ab_scr, bb_scr, hf_scr, pf_scr, hb_scr, pb_scr):
    t = x_ref.shape[0]
    n_seg = SUBLANES
    tc = t // n_seg
    pad = SUBLANES
    w = LANES

    xp_scr[0:pad, :] = jnp.zeros((pad, w), F32)
    xp_scr[pad + t:pad + t + pad, :] = jnp.zeros((pad, w), F32)
    xp_scr[pad:pad + t, :] = x_ref[...]

    lam = lam_ref[...]
    neg = -lam
    softplus = jnp.maximum(neg, 0.0) + jnp.log(1.0 + jnp.exp(-jnp.abs(neg)))
    log2_a_per_r = softplus * float(-LRU_C * np.log2(np.e))
    cw = cw_ref[...]
    cb = cb_ref[...]
    bg = bg_ref[...]

    def gate_body(n, carry):
        t0 = pl.multiple_of(n * tc, tc)
        xx = xp_scr[pl.ds(t0, tc + 2 * pad), :]
        total = tc + 2 * pad
        xc = cb
        for j in range(LRU_CONV):
            shift = LRU_CONV // 2 - j
            xs = xx if shift == 0 else pltpu.roll(xx, shift % total, 0)
            xc = xc + xs[pad:pad + tc] * cw[j:j + 1, :]
        gates = _dot(xc, wg_ref[...]) + bg
        seg_rows = pl.ds(n, tc, stride=n_seg)
        for d, (a_scr, b_scr) in enumerate(((af_scr, bf_scr), (ab_scr, bb_scr))):
            r = _sigmoid(gates[:, (2 * d) * w:(2 * d + 1) * w])
            i = _sigmoid(gates[:, (2 * d + 1) * w:(2 * d + 2) * w])
            a = jnp.exp2(r * log2_a_per_r[d:d + 1, :])
            a_scr[seg_rows, :] = a
            gap = 1.0 - a * a
            b_scr[seg_rows, :] = gap * lax.rsqrt(jnp.maximum(gap, 1e-30)) * (i * xc)
        return carry

    lax.fori_loop(0, n_seg, gate_body, 0, unroll=2)

    def scan_body(i, carry):
        hf, pf, hb, pb = carry
        sl = pl.ds(pl.multiple_of(i * n_seg, n_seg), n_seg)
        a = af_scr[sl, :]
        hf = a * hf + bf_scr[sl, :]
        pf = a * pf
        hf_scr[sl, :] = hf
        pf_scr[sl, :] = pf
        slb = pl.ds(pl.multiple_of((tc - 1 - i) * n_seg, n_seg), n_seg)
        a = ab_scr[slb, :]
        hb = a * hb + bb_scr[slb, :]
        pb = a * pb
        hb_scr[slb, :] = hb
        pb_scr[slb, :] = pb
        return hf, pf, hb, pb

    zero = jnp.zeros((n_seg, w), F32)
    one = jnp.ones((n_seg, w), F32)
    hf, pf, hb, pb = lax.fori_loop(0, tc, scan_body, (zero, one, zero, one), unroll=8)

    row = lax.broadcasted_iota(jnp.int32, (n_seg, w), 0)
    for s in (1, 2, 4):
        keep = row >= s
        hf = jnp.where(keep, pf * pltpu.roll(hf, s, 0) + hf, hf)
        pf = jnp.where(keep, pf * pltpu.roll(pf, s, 0), pf)
        keep = row < n_seg - s
        hb = jnp.where(keep, pb * pltpu.roll(hb, n_seg - s, 0) + hb, hb)
        pb = jnp.where(keep, pb * pltpu.roll(pb, n_seg - s, 0), pb)
    carry_f = jnp.where(row >= 1, pltpu.roll(hf, 1, 0), 0.0)
    carry_b = jnp.where(row < n_seg - 1, pltpu.roll(hb, n_seg - 1, 0), 0.0)

    def fix_body(i, carry):
        sl = pl.ds(pl.multiple_of(i * n_seg, n_seg), n_seg)
        af_scr[sl, :] = (hf_scr[sl, :] + pf_scr[sl, :] * carry_f) + (hb_scr[sl, :] + pb_scr[sl, :] * carry_b)
        return carry

    lax.fori_loop(0, tc, fix_body, 0, unroll=8)

    def out_body(n, carry):
        h = af_scr[pl.ds(n, tc, stride=n_seg), :]
        sl = pl.ds(pl.multiple_of(n * tc, tc), tc)
        out_ref[sl, :] = (h * _silu(g_ref[sl, :])).astype(out_ref.dtype)
        return carry

    lax.fori_loop(0, n_seg, out_body, 0)


def _rglru(z3, conv_w, conv_b, wg, bg, lam):
    b, t, _ = z3.shape
    n_cb = BRANCH_W // LANES
    return pl.pallas_call(
        _rglru_kernel,
        grid=(b, n_cb),
        in_specs=[
            _zspec(t, LANES, OFF_LX, 0),
            _zspec(t, LANES, OFF_LG, 0),
            pl.BlockSpec((LRU_CONV, LANES), lambda i, h: (0, h)),
            pl.BlockSpec((1, LANES), lambda i, h: (0, h)),
            pl.BlockSpec((None, LANES, 4 * LANES), lambda i, h: (h, 0, 0)),
            pl.BlockSpec((None, 1, 4 * LANES), lambda i, h: (h, 0, 0)),
            pl.BlockSpec((2, LANES), lambda i, h: (0, h)),
        ],
        out_specs=pl.BlockSpec((None, t, LANES), lambda i, h: (i, 0, h)),
        out_shape=jax.ShapeDtypeStruct((b, t, BRANCH_W), BF16),
        scratch_shapes=[pltpu.VMEM((t + 2 * SUBLANES, LANES), F32)] + [pltpu.VMEM((t, LANES), F32)] * 8,
        compiler_params=_params("parallel", "parallel"),
        name="rglru",
    )(z3, z3, conv_w, conv_b.reshape(1, BRANCH_W), wg, bg, lam)


def _rglru_gate_weights(wa, ba, wx, bx):
    n_cb = BRANCH_W // LANES
    per = LANES // LRU_BW

    def blockdiag(w):
        w = w.reshape(n_cb, per, LRU_BW, LRU_BW)
        eye = jnp.eye(per, dtype=w.dtype)
        return jnp.einsum('cpjk,pq->cpjqk', w, eye).reshape(n_cb, LANES, LANES)

    wg = jnp.concatenate([blockdiag(wa[0]), blockdiag(wx[0]), blockdiag(wa[1]), blockdiag(wx[1])], axis=-1)
    bg = jnp.concatenate([ba[0].reshape(n_cb, 1, LANES), bx[0].reshape(n_cb, 1, LANES),
                          ba[1].reshape(n_cb, 1, LANES), bx[1].reshape(n_cb, 1, LANES)], axis=-1)
    return wg.astype(BF16), bg.astype(F32)


def _prefix_sum_rows(x):
    c, w = x.shape
    x3 = x.reshape(c // SUBLANES, SUBLANES, w)
    row = lax.broadcasted_iota(jnp.int32, (1, SUBLANES, w), 1)
    s = 1
    while s < SUBLANES:
        x3 = x3 + jnp.where(row >= s, pltpu.roll(x3, s, 1), 0.0)
        s *= 2
    tile_tot = jnp.broadcast_to(x3[:, SUBLANES - 1:, :], x3.shape).reshape(c, w)
    x = x3.reshape(c, w)
    while s < c:
        shifted = jnp.concatenate([jnp.zeros((s, w), x.dtype), tile_tot[:c - s]], axis=0)
        x = x + shifted
        tile_tot = tile_tot + shifted
        s *= 2
    return x


def _hgrn_forget(zf, lb):
    f = lb + (1.0 - lb) * _sigmoid(zf)
    return f, jnp.log2(f)


def _hgrn_tile_decays(size, f3, fb3, bs3, cs3, pos):
    half = size // 2
    if size == 2:
        return jnp.where(pos == 1, f3, 1.0), jnp.where(pos == 0, fb3, 1.0)
    if size == 4:
        f_prev, f_next = pltpu.roll(f3, 1, 1), pltpu.roll(f3, SUBLANES - 1, 1)
        fb_prev, fb_next = pltpu.roll(fb3, 1, 1), pltpu.roll(fb3, SUBLANES - 1, 1)
        e_f = jnp.where(pos == 0, f_next, jnp.where(pos == 1, 1.0, jnp.where(pos == 2, f3, f_prev * f3)))
        e_b = jnp.where(pos == 0, fb3 * fb_next, jnp.where(pos == 1, fb3, jnp.where(pos == 2, 1.0, fb_prev)))
        return e_f, e_b
    sign = jnp.where(pos >= half, 1.0, -1.0)
    e_f = jnp.exp2((bs3 - bs3[:, half - 1:half, :]) * sign)
    e_b = jnp.exp2((cs3[:, half:half + 1, :] - cs3) * sign)
    return e_f, e_b


def _hgrn_level_operands(size, q, kf, kb, f, fb, bs, cs):
    c, w = q.shape
    half = size // 2
    if size <= SUBLANES:
        tiled = lambda a: a.reshape(c // SUBLANES, SUBLANES, w)
        pos = lax.broadcasted_iota(jnp.int32, (1, SUBLANES, w), 1) % size
        upper = pos >= half
        e_f, e_b = _hgrn_tile_decays(size, tiled(f), tiled(fb), tiled(bs), tiled(cs), pos)
        z_f = e_f * jnp.where(upper, tiled(q), tiled(kf))
        z_b = e_b * jnp.where(upper, tiled(kb), tiled(q))
        x = jnp.concatenate([jnp.where(upper, z_f, 0.0), jnp.where(upper, 0.0, z_b)], axis=2)
        y = jnp.concatenate([z_f, z_b], axis=2)
        return x.reshape(c, 2 * w), y.reshape(c, 2 * w)
    xs, ys = [], []
    zero = jnp.zeros((half, w), F32)
    for i in range(c // half):
        rows = slice(i * half, (i + 1) * half)
        if i % 2 == 0:
            ref_f = bs[(i + 1) * half - 1:(i + 1) * half]
            ref_b = cs[(i + 1) * half:(i + 1) * half + 1]
            z_f = kf[rows] * jnp.exp2(ref_f - bs[rows])
            z_b = q[rows] * jnp.exp2(cs[rows] - ref_b)
            xs.append(jnp.concatenate([zero, z_b], axis=1))
        else:
            ref_f = bs[i * half - 1:i * half]
            ref_b = cs[i * half:i * half + 1]
            z_f = q[rows] * jnp.exp2(bs[rows] - ref_f)
            z_b = kb[rows] * jnp.exp2(ref_b - cs[rows])
            xs.append(jnp.concatenate([z_f, zero], axis=1))
        ys.append(jnp.concatenate([z_f, z_b], axis=1))
    return jnp.concatenate(xs, axis=0), jnp.concatenate(ys, axis=0)


def _hgrn_kernel(q_ref, ff_ref, fb_ref, v_ref, g_ref, lbl_ref, gain_ref, h_ref, wmg_ref, out_ref, gate_ref,
                 fb_scr, cs_scr, sb_scr, st_scr, *, layer):
    t = q_ref.shape[0]
    c = HGRN_CHUNK
    n_chunks = t // c
    w = HGRN_DK

    logits = lbl_ref[...]
    mx = jnp.max(logits, axis=0)
    ex = jnp.exp(logits - mx[None])
    tot = jnp.sum(ex, axis=0)
    lb = jnp.zeros_like(tot)
    for i in range(1, layer + 1):
        lb = lb + ex[i] / tot
    lb_f = lb[0:1, :]
    lb_b = lb[1:2, :]

    pair_xor = lax.broadcasted_iota(jnp.int32, (c, c), 0) ^ lax.broadcasted_iota(jnp.int32, (c, c), 1)
    pair_level = jnp.zeros((c, c), jnp.int32)
    size = 2
    while size <= c:
        pair_level = jnp.where(pair_xor >= size // 2, size, pair_level)
        size *= 2

    st_scr[...] = jnp.zeros_like(st_scr)

    def bwd_body(i, carry):
        n = n_chunks - 1 - i
        sl = pl.ds(pl.multiple_of(n * c, c), c)
        fb, gb = _hgrn_forget(fb_ref[sl, :], lb_b)
        pre = _prefix_sum_rows(gb)
        total = pre[c - 1:c, :]
        cs = total - pre + gb
        fb_scr[sl, :] = fb
        cs_scr[sl, :] = cs
        sb_scr[n] = st_scr[...]
        st_scr[...] = st_scr[...] * jnp.exp2(total) + _dot_tn(v_ref[sl, :], (1.0 - fb) * jnp.exp2(total - cs))
        return carry

    lax.fori_loop(0, n_chunks, bwd_body, 0, unroll=8)

    st_scr[...] = jnp.zeros_like(st_scr)

    def fwd_chunk(n):
        sl = pl.ds(pl.multiple_of(n * c, c), c)
        q = _silu(q_ref[sl, :])
        v = v_ref[sl, :]
        f, gf = _hgrn_forget(ff_ref[sl, :], lb_f)
        kf = 1.0 - f
        bs = _prefix_sum_rows(gf)
        fb = fb_scr[sl, :]
        kb = 1.0 - fb
        cs = cs_scr[sl, :]

        att = None
        size = 2
        while size <= c:
            x, y = _hgrn_level_operands(size, q, kf, kb, f, fb, bs, cs)
            att = jnp.where(pair_level == size, _dot_nt(x, y), 0.0 if att is None else att)
            size *= 2

        diag = jnp.sum(q * (kf + kb), axis=-1, keepdims=True)
        o = _dot(att, v) + diag * v
        inter = jnp.concatenate([q * jnp.exp2(bs), q * jnp.exp2(cs)], axis=1)
        states = jnp.concatenate([st_scr[...], sb_scr[n]], axis=1)
        o = o + _dot_nt(inter, states)
        last = bs[c - 1:c, :]
        st_scr[...] = st_scr[...] * jnp.exp2(last) + _dot_tn(v, kf * jnp.exp2(last - bs))
        o = o * lax.rsqrt(jnp.mean(o * o, axis=-1, keepdims=True) + EPS) * gain_ref[...]
        out_ref[sl, :] = (o * _silu(g_ref[sl, :])).astype(out_ref.dtype)

    chunks_per_block = min(HGRN_GATE_CHUNKS, n_chunks)
    gate_rows = chunks_per_block * c

    col_pieces = D_MODEL // GATE_PIECE_COLS
    row_pieces = chunks_per_block // col_pieces
    piece_rows = gate_rows // row_pieces

    def block_body(m, carry):
        for u in range(chunks_per_block):
            r0 = pl.multiple_of(m * gate_rows + (u % row_pieces) * piece_rows, piece_rows)
            c0 = (u // row_pieces) * GATE_PIECE_COLS
            gate_ref[pl.ds(r0, piece_rows), c0:c0 + GATE_PIECE_COLS] = jnp.dot(
                h_ref[pl.ds(r0, piece_rows), :], wmg_ref[:, c0:c0 + GATE_PIECE_COLS],
                preferred_element_type=F32).astype(gate_ref.dtype)
            fwd_chunk(m * chunks_per_block + u)
        return carry

    lax.fori_loop(0, n_chunks // chunks_per_block, block_body, 0)


def _hgrn(z3, lb_logits, gain, h3, wmg, layer):
    b, t, _ = z3.shape
    depth = lb_logits.shape[0]
    w = HGRN_DK
    n_chunks = t // HGRN_CHUNK
    return pl.pallas_call(
        functools.partial(_hgrn_kernel, layer=layer),
        grid=(b, HGRN_HEADS),
        in_specs=[
            _zspec(t, w, OFF_HQ, 0),
            _zspec(t, w, OFF_HFF, 0),
            _zspec(t, w, OFF_HFB, 0),
            _zspec(t, w, OFF_HI, 0),
            _zspec(t, w, OFF_HG, 0),
            pl.BlockSpec((depth, 2, w), lambda i, h: (0, 0, h)),
            pl.BlockSpec((1, w), lambda i, h: (0, h)),
            pl.BlockSpec((None, t, D_MODEL), lambda i, h: (i, 0, 0)),
            pl.BlockSpec((None, D_MODEL, D_MODEL), lambda i, h: (h, 0, 0)),
        ],
        out_specs=[
            pl.BlockSpec((None, t, w), lambda i, h: (i, 0, h)),
            pl.BlockSpec((None, t, D_MODEL), lambda i, h: (i, 0, h)),
        ],
        out_shape=[
            jax.ShapeDtypeStruct((b, t, BRANCH_W), BF16),
            jax.ShapeDtypeStruct((b, t, N_BRANCH * D_MODEL), BF16),
        ],
        scratch_shapes=[
            pltpu.VMEM((t, w), F32),
            pltpu.VMEM((t, w), F32),
            pltpu.VMEM((n_chunks, w, w), F32),
            pltpu.VMEM((w, w), F32),
        ],
        compiler_params=_params("parallel", "parallel"),
        name="hgrn",
    )(z3, z3, z3, z3, z3, lb_logits, gain.reshape(1, -1), h3, wmg)


def _encoder(x, p, w):
    b, t, _ = x.shape
    depth = w['w_in'].shape[0]
    x2d = x.reshape(b * t, D_MODEL)
    for l in range(depth):
        z2d, h2d = _inproj(x2d, w['norm_mix'][l], w['w_in'][l])
        z3 = z2d.reshape(b, t, W_IN)
        br_a = _retention(z3, w['cos'], w['sin'], w['lgq'][l], w['lgv'][l])
        br_b = _nat(z3, w['nat_bias'][l])
        br_c = _rglru(z3, w['conv_w'][l], w['conv_b'][l], w['lru_wg'][l], w['lru_bg'][l], w['lam'][l])
        br_d, gates = _hgrn(z3, w['lb_logits'], w['hgrn_gain'][l], h2d.reshape(b, t, D_MODEL),
                            w['w_merge'][l], l)
        branches = [a.reshape(b * t, BRANCH_W) for a in (br_a, br_b, br_c, br_d)]
        x2d = _merge(x2d, branches, gates.reshape(b * t, N_BRANCH * D_MODEL), p[l].reshape(b * t, PLE_DIM),
                     w['w_branch'][l], w['w_out'][l], w['ple_norm'][l], w['w_ple_gate'][l],
                     w['w_ple_proj'][l], w['final_norm'], l == depth - 1)
    return x2d.reshape(b, t, D_MODEL)


def kernel(x_prompt, x_sample, p_prompt, p_sample, norm_mix, w_in, ret_decay_logit, nat_rpb, lru_conv_w,
           lru_conv_b, lru_wa, lru_ba, lru_wx, lru_bx, lru_lambda, hgrn_lb_logits, hgrn_norm, w_branch,
           w_merge, w_out, ple_norm, w_ple_gate, w_ple_proj, final_norm):
    depth = w_in.shape[0]
    t = x_prompt.shape[1]
    rows = t // GRID_W
    cos_tab, sin_tab = _rotary_tables(t)
    gate_w = [_rglru_gate_weights(lru_wa[l], lru_ba[l], lru_wx[l], lru_bx[l]) for l in range(depth)]
    weights = {
        'norm_mix': norm_mix,
        'w_in': jnp.concatenate([_pair_rotary_layout(w_in[..., OFF_RQ:OFF_RK]),
                                 _pair_rotary_layout(w_in[..., OFF_RK:OFF_RV]),
                                 w_in[..., OFF_RV:]], axis=-1).astype(BF16),
        'cos': cos_tab,
        'sin': sin_tab,
        'lgq': _pair_rotary_layout(jnp.repeat(ret_decay_logit.astype(F32), RET_QK, axis=-1)),
        'lgv': jnp.repeat(ret_decay_logit.astype(F32), RET_V, axis=-1),
        'nat_bias': [_nat_bias_tables(nat_rpb[l], rows) for l in range(depth)],
        'conv_w': lru_conv_w,
        'conv_b': lru_conv_b,
        'lru_wg': [g[0] for g in gate_w],
        'lru_bg': [g[1] for g in gate_w],
        'lam': lru_lambda,
        'lb_logits': hgrn_lb_logits,
        'hgrn_gain': hgrn_norm,
        'w_branch': w_branch.astype(BF16),
        'w_merge': w_merge.astype(BF16),
        'w_out': w_out.astype(BF16),
        'ple_norm': ple_norm,
        'w_ple_gate': w_ple_gate.astype(BF16),
        'w_ple_proj': w_ple_proj.astype(BF16),
        'final_norm': final_norm,
    }
    y_prompt = _encoder(x_prompt, p_prompt, weights)
    y_sample = _encoder(x_sample, p_sample, weights)
    return (y_prompt, y_sample)
```

```python
import functools

import numpy as np
import jax
import jax.numpy as jnp
from jax import lax
from jax.experimental import pallas as pl
from jax.experimental.pallas import tpu as pltpu

F32 = jnp.float32
BF16 = jnp.bfloat16

D_MODEL = 1024
PLE_DIM = 256
GRID_W = 64
N_BRANCH = 4
BRANCH_W = 512
RET_HEADS = 4
RET_QK = 64
RET_V = 128
ROPE_BASE = 10000.0
NAT_HEADS = 8
NAT_HD = 64
NAT_WIN_ROWS = 8
NAT_WIN_COLS = 16
LRU_BLOCKS = 8
LRU_BW = 64
LRU_CONV = 4
LRU_C = 8.0
HGRN_HEADS = 4
HGRN_DK = 128
EPS = 1e-6
W_IN = 7168

OFF_RQ, OFF_RK, OFF_RV, OFF_RG = 0, 256, 512, 1024
OFF_NQ, OFF_NK, OFF_NV, OFF_NG = 1536, 2048, 2560, 3072
OFF_LX, OFF_LG = 3584, 4096
OFF_HQ, OFF_HFF, OFF_HFB, OFF_HI, OFF_HG = 4608, 5120, 5632, 6144, 6656

LANES = 128
SUBLANES = 8
VMEM_LIMIT = 56 * 1024 * 1024

CHUNK = 128
HGRN_CHUNK = 128
HGRN_GATE_CHUNKS = 8
GATE_PIECE_COLS = 256


def _params(*sem):
    return pltpu.CompilerParams(dimension_semantics=sem, vmem_limit_bytes=VMEM_LIMIT)


def _dot(a, b):
    return jnp.dot(a.astype(BF16), b.astype(BF16), preferred_element_type=F32)


def _dot_nt(a, b):
    return lax.dot_general(a.astype(BF16), b.astype(BF16), (((1,), (1,)), ((), ())),
                           preferred_element_type=F32)


def _dot_tn(a, b):
    return lax.dot_general(a.astype(BF16), b.astype(BF16), (((0,), (0,)), ((), ())),
                           preferred_element_type=F32)


def _rms(x, g):
    return x * lax.rsqrt(jnp.mean(x * x, axis=-1, keepdims=True) + EPS) * g


def _sigmoid(x):
    return jax.nn.sigmoid(x)


def _silu(x):
    return x * jax.nn.sigmoid(x)


def _zspec(t, width, off, grid_pos):
    base = off // width
    if grid_pos == 0:
        return pl.BlockSpec((None, t, width), lambda b, h: (b, 0, base + h))
    return pl.BlockSpec((None, t, width), lambda h, b: (b, 0, base + h))


def _inproj_kernel(x_ref, g_ref, w_ref, z_ref, h_ref):
    @pl.when(pl.program_id(1) == 0)
    def _():
        h_ref[...] = _rms(x_ref[...], g_ref[...]).astype(h_ref.dtype)

    z_ref[...] = jnp.dot(h_ref[...], w_ref[...], preferred_element_type=F32)


def _inproj(x2d, g, w_bf16):
    m = x2d.shape[0]
    tm, tn = 1024, 1792
    return pl.pallas_call(
        _inproj_kernel,
        grid=(m // tm, W_IN // tn),
        in_specs=[
            pl.BlockSpec((tm, D_MODEL), lambda i, j: (i, 0)),
            pl.BlockSpec((1, D_MODEL), lambda i, j: (0, 0)),
            pl.BlockSpec((D_MODEL, tn), lambda i, j: (0, j)),
        ],
        out_specs=[
            pl.BlockSpec((tm, tn), lambda i, j: (i, j)),
            pl.BlockSpec((tm, D_MODEL), lambda i, j: (i, 0)),
        ],
        out_shape=[
            jax.ShapeDtypeStruct((m, W_IN), F32),
            jax.ShapeDtypeStruct((m, D_MODEL), BF16),
        ],
        compiler_params=_params("parallel", "arbitrary"),
        name="inproj",
    )(x2d, g.reshape(1, D_MODEL), w_bf16)


def _merge_kernel(x_ref, ba_ref, bb_ref, bc_ref, bd_ref, gates_ref, p_ref, wbr_ref, wo_ref,
                  gple_ref, wpg_ref, wpp_ref, gfin_ref, out_ref, *, final):
    x = x_ref[...]
    merged = None
    for j, b_ref in enumerate((ba_ref, bb_ref, bc_ref, bd_ref)):
        gate = _sigmoid(gates_ref[:, j * D_MODEL:(j + 1) * D_MODEL].astype(F32))
        term = gate * jnp.dot(b_ref[...], wbr_ref[j], preferred_element_type=F32)
        merged = term if merged is None else merged + term
    x1 = x + _dot(merged, wo_ref[...])
    gate2 = _sigmoid(_dot(_rms(x1, gple_ref[...]), wpg_ref[...]))
    x2 = x1 + gate2 * _dot(p_ref[...], wpp_ref[...])
    if final:
        x2 = _rms(x2, gfin_ref[...])
    out_ref[...] = x2


def _merge(x2d, branches, gates, p2d, wbr, wo, gple, wpg, wpp, gfin, final):
    m = x2d.shape[0]
    tm = 512
    row = lambda i: (i, 0)
    const2 = lambda i: (0, 0)
    const3 = lambda i: (0, 0, 0)
    once = pl.Buffered(1)
    vec = pl.BlockSpec((1, D_MODEL), const2)
    return pl.pallas_call(
        functools.partial(_merge_kernel, final=final),
        grid=(m // tm,),
        in_specs=[
            pl.BlockSpec((tm, D_MODEL), row),
            pl.BlockSpec((tm, BRANCH_W), row),
            pl.BlockSpec((tm, BRANCH_W), row),
            pl.BlockSpec((tm, BRANCH_W), row),
            pl.BlockSpec((tm, BRANCH_W), row),
            pl.BlockSpec((tm, N_BRANCH * D_MODEL), row),
            pl.BlockSpec((tm, PLE_DIM), row),
            pl.BlockSpec((N_BRANCH, BRANCH_W, D_MODEL), const3, pipeline_mode=once),
            pl.BlockSpec((D_MODEL, D_MODEL), const2, pipeline_mode=once),
            vec,
            pl.BlockSpec((D_MODEL, D_MODEL), const2, pipeline_mode=once),
            pl.BlockSpec((PLE_DIM, D_MODEL), const2, pipeline_mode=once),
            vec,
        ],
        out_specs=pl.BlockSpec((tm, D_MODEL), row),
        out_shape=jax.ShapeDtypeStruct((m, D_MODEL), F32),
        compiler_params=_params("parallel"),
        name="merge",
    )(x2d, *branches, gates, p2d, wbr, wo, gple.reshape(1, -1), wpg, wpp, gfin.reshape(1, -1))


def _retention_kernel(q_ref, k_ref, v_ref, g_ref, cos_ref, sin_ref, lgq_ref, lgv_ref, out_ref,
                      kr_scr, sb_scr, sf_scr):
    t = q_ref.shape[0]
    c = CHUNK
    n_chunks = t // c
    hd = RET_QK

    lane = lax.broadcasted_iota(jnp.int32, (1, 2 * hd), 1)
    head0_q = (lane // (hd // 2)) % 2 == 0
    lane_v = lax.broadcasted_iota(jnp.int32, (1, 2 * RET_V), 1)
    head0_v = lane_v < RET_V

    def rotary(x, cos, sin):
        return x * cos + pltpu.roll(x, hd, 1) * sin

    lg_f = -jnp.log(1.0 + jnp.exp(-lgq_ref[0:1, :]))
    lg_b = -jnp.log(1.0 + jnp.exp(-lgq_ref[1:2, :]))
    lgv_f = -jnp.log(1.0 + jnp.exp(-lgv_ref[0:1, :]))
    lgv_b = -jnp.log(1.0 + jnp.exp(-lgv_ref[1:2, :]))
    tcol = lax.broadcasted_iota(jnp.int32, (c, 1), 0).astype(F32)
    head_f = jnp.exp((tcol + 1.0) * lg_f)
    head_b = jnp.exp((c - tcol) * lg_b)
    tail_f = jnp.exp((c - 1.0 - tcol) * lg_f)
    tail_b = jnp.exp(tcol * lg_b)
    dec_f = jnp.exp(c * lgv_f)
    dec_b = jnp.exp(c * lgv_b)
    rowk = lax.broadcasted_iota(jnp.int32, (2 * hd, 2 * RET_V), 0)
    colv = lax.broadcasted_iota(jnp.int32, (2 * hd, 2 * RET_V), 1)
    blockdiag = ((rowk // (hd // 2)) % 2) == (colv // RET_V)

    diff = (lax.broadcasted_iota(jnp.int32, (c, c), 0) - lax.broadcasted_iota(jnp.int32, (c, c), 1)).astype(F32)

    def decay_mask(lf, lb):
        fwd = jnp.exp(jnp.maximum(diff, 0.0) * lf)
        bwd = jnp.exp(jnp.maximum(-diff, 0.0) * lb)
        return jnp.where(diff > 0, fwd, jnp.where(diff < 0, bwd, 2.0))

    dmask = jnp.concatenate([decay_mask(lgv_f[:, :c], lgv_b[:, :c]),
                             decay_mask(lgv_f[:, RET_V:RET_V + c], lgv_b[:, RET_V:RET_V + c])], axis=1)

    sf_scr[...] = jnp.zeros_like(sf_scr)

    def bwd_body(i, carry):
        n = n_chunks - 1 - i
        sl = pl.ds(pl.multiple_of(n * c, c), c)
        sb_scr[n] = sf_scr[...]
        kn = rotary(k_ref[sl, :], cos_ref[sl, :], sin_ref[sl, :]) * (hd ** -0.5)
        kr_scr[sl, :] = kn
        loc = _dot_tn(kn * tail_b, v_ref[sl, :])
        sf_scr[...] = sf_scr[...] * dec_b + jnp.where(blockdiag, loc, 0.0)
        return carry

    lax.fori_loop(0, n_chunks, bwd_body, 0, unroll=8)

    sf_scr[...] = jnp.zeros_like(sf_scr)

    def fwd_body(n, carry):
        sl = pl.ds(pl.multiple_of(n * c, c), c)
        qn = rotary(q_ref[sl, :], cos_ref[sl, :], sin_ref[sl, :])
        kn = kr_scr[sl, :]
        vn = v_ref[sl, :]
        kstack = jnp.concatenate([jnp.where(head0_q, kn, 0.0), jnp.where(head0_q, 0.0, kn)], axis=0)
        scores = _dot_nt(qn, kstack) * dmask
        vstack = jnp.concatenate([jnp.where(head0_v, vn, 0.0), jnp.where(head0_v, 0.0, vn)], axis=0)
        lhs = jnp.concatenate([scores, qn * head_f, qn * head_b], axis=1)
        rhs = jnp.concatenate([vstack, sf_scr[...], sb_scr[n]], axis=0)
        o = _dot(lhs, rhs)
        loc = _dot_tn(kn * tail_f, vn)
        sf_scr[...] = sf_scr[...] * dec_f + jnp.where(blockdiag, loc, 0.0)
        o0 = o[:, :RET_V]
        o1 = o[:, RET_V:]
        o0 = o0 * lax.rsqrt(jnp.mean(o0 * o0, axis=-1, keepdims=True) + EPS)
        o1 = o1 * lax.rsqrt(jnp.mean(o1 * o1, axis=-1, keepdims=True) + EPS)
        on = jnp.concatenate([o0, o1], axis=1)
        out_ref[sl, :] = (on * _silu(g_ref[sl, :])).astype(out_ref.dtype)
        return carry

    lax.fori_loop(0, n_chunks, fwd_body, 0, unroll=8)


def _retention(z3, cos_tab, sin_tab, lgq, lgv):
    b, t, _ = z3.shape
    n_chunks = t // CHUNK
    return pl.pallas_call(
        _retention_kernel,
        grid=(b, RET_HEADS // 2),
        in_specs=[
            _zspec(t, 2 * RET_QK, OFF_RQ, 0),
            _zspec(t, 2 * RET_QK, OFF_RK, 0),
            _zspec(t, 2 * RET_V, OFF_RV, 0),
            _zspec(t, 2 * RET_V, OFF_RG, 0),
            pl.BlockSpec((t, 2 * RET_QK), lambda i, h: (0, 0)),
            pl.BlockSpec((t, 2 * RET_QK), lambda i, h: (0, 0)),
            pl.BlockSpec((2, 2 * RET_QK), lambda i, h: (0, h)),
            pl.BlockSpec((2, 2 * RET_V), lambda i, h: (0, h)),
        ],
        out_specs=pl.BlockSpec((None, t, 2 * RET_V), lambda i, h: (i, 0, h)),
        out_shape=jax.ShapeDtypeStruct((b, t, BRANCH_W), BF16),
        scratch_shapes=[
            pltpu.VMEM((t, 2 * RET_QK), F32),
            pltpu.VMEM((n_chunks, 2 * RET_QK, 2 * RET_V), F32),
            pltpu.VMEM((2 * RET_QK, 2 * RET_V), F32),
        ],
        compiler_params=_params("parallel", "parallel"),
        name="retention",
    )(z3, z3, z3, z3, cos_tab, sin_tab, lgq, lgv)


def _rotary_tables(t):
    half = RET_QK // 2
    inv = ROPE_BASE ** (-jnp.arange(half, dtype=F32) / half)
    ang = jnp.arange(t, dtype=F32)[:, None] * inv[None, :]
    cos = jnp.cos(ang)
    sin = jnp.sin(ang)
    cos_tab = jnp.tile(cos, (1, 4))
    sin_tab = jnp.concatenate([-sin, -sin, sin, sin], axis=1)
    return cos_tab, sin_tab


def _pair_rotary_layout(a):
    lead = a.shape[:-1]
    a = a.reshape(lead + (RET_HEADS // 2, 2, 2, RET_QK // 2))
    return jnp.swapaxes(a, -3, -2).reshape(lead + (RET_HEADS * RET_QK,))


NAT_QROWS = 8
NAT_KROWS = 16
NAT_KCOLS = 2 * NAT_WIN_COLS


def _nat_kstart(j):
    return int(np.clip(j * NAT_WIN_COLS - NAT_WIN_COLS // 2, 0, GRID_W - NAT_KCOLS))


def _nat_key_row_base(g, rows):
    return int(np.clip(g * NAT_QROWS - NAT_WIN_ROWS // 2, 0, rows - NAT_KROWS))


def _edge_class(i, n):
    return 0 if i == 0 else (2 if i == n - 1 else 1)


def _nat_bias_tables(rpb, rows):
    n_g = rows // NAT_QROWS
    n_cb = GRID_W // NAT_WIN_COLS

    def row_geometry(g):
        dr = np.zeros((NAT_QROWS, NAT_KROWS), np.int32)
        ok = np.zeros((NAT_QROWS, NAT_KROWS), bool)
        kb = _nat_key_row_base(g, rows)
        for rr in range(NAT_QROWS):
            r = g * NAT_QROWS + rr
            rs = int(np.clip(r - NAT_WIN_ROWS // 2, 0, rows - NAT_WIN_ROWS))
            for kr in range(NAT_KROWS):
                ka = kb + kr
                ok[rr, kr] = rs <= ka < rs + NAT_WIN_ROWS
                dr[rr, kr] = int(np.clip(ka - r + NAT_WIN_ROWS - 1, 0, 2 * NAT_WIN_ROWS - 2))
        return dr, ok

    def col_geometry(j):
        dc = np.zeros((NAT_WIN_COLS, NAT_KCOLS), np.int32)
        ok = np.zeros((NAT_WIN_COLS, NAT_KCOLS), bool)
        ks = _nat_kstart(j)
        for qq in range(NAT_WIN_COLS):
            qc = j * NAT_WIN_COLS + qq
            ws = int(np.clip(qc - NAT_WIN_COLS // 2, 0, GRID_W - NAT_WIN_COLS))
            for kc in range(NAT_KCOLS):
                ka = ks + kc
                ok[qq, kc] = ws <= ka < ws + NAT_WIN_COLS
                dc[qq, kc] = int(np.clip(ka - qc + NAT_WIN_COLS - 1, 0, 2 * NAT_WIN_COLS - 2))
        return dc, ok

    def by_class(geometry, n):
        reps = {}
        for i in range(n):
            dx, ok = geometry(i)
            cls = _edge_class(i, n)
            if cls in reps:
                assert (np.where(ok, dx, -1) == np.where(reps[cls][1], reps[cls][0], -1)).all()
            else:
                reps[cls] = (dx, ok)
        filled = [reps.get(cls, reps[0]) for cls in range(3)]
        return np.stack([f[0] for f in filled]), np.stack([f[1] for f in filled])

    dr, row_ok = by_class(row_geometry, n_g)
    dc, col_ok = by_class(col_geometry, n_cb)
    col_sel = (dc[..., None] == np.arange(2 * NAT_WIN_COLS - 1)).astype(np.float32)
    blocks = jnp.einsum('hab,jqcb->hjaqc', rpb.astype(F32), col_sel, precision=lax.Precision.HIGHEST)
    blocks = jnp.where(col_ok[None, :, None], blocks, -jnp.inf)
    nq = NAT_QROWS * NAT_WIN_COLS
    nk = NAT_KROWS * NAT_KCOLS

    def tile_kernel(blocks_ref, out_ref):
        masked = jnp.full((NAT_WIN_COLS, NAT_KCOLS), -jnp.inf, F32)
        for g in range(3):
            for j in range(3):
                for rr in range(NAT_QROWS):
                    pieces = [blocks_ref[j, int(dr[g, rr, kr])] if row_ok[g, rr, kr] else masked
                              for kr in range(NAT_KROWS)]
                    out_ref[g, j, rr * NAT_WIN_COLS:(rr + 1) * NAT_WIN_COLS, :] = jnp.concatenate(pieces, axis=1)

    return pl.pallas_call(
        tile_kernel,
        grid=(NAT_HEADS,),
        in_specs=[pl.BlockSpec((None,) + blocks.shape[1:], lambda h: (h, 0, 0, 0, 0))],
        out_specs=pl.BlockSpec((None, None, 3, 3, nq, nk), lambda h: (h // 2, h % 2, 0, 0, 0, 0)),
        out_shape=jax.ShapeDtypeStruct((NAT_HEADS // 2, 2, 3, 3, nq, nk), F32),
        compiler_params=_params("parallel"),
        name="nat_bias",
    )(blocks)


def _nat_kernel(q_ref, k_ref, v_ref, g_ref, bias_ref, out_ref):
    t = q_ref.shape[0]
    rows = t // GRID_W
    n_g = rows // NAT_QROWS
    n_cb = GRID_W // NAT_WIN_COLS
    nq = NAT_QROWS * NAT_WIN_COLS
    lane = lax.broadcasted_iota(jnp.int32, (1, 2 * NAT_HD), 1)
    head0 = lane < NAT_HD

    def group_body(g, carry):
        kb = jnp.clip(g * NAT_QROWS - NAT_WIN_ROWS // 2, 0, rows - NAT_KROWS)
        g_cls = jnp.where(g == 0, 0, jnp.where(g == n_g - 1, 2, 1))
        for j in range(n_cb):
            ks = _nat_kstart(j)

            def qrows(ref, rr):
                start = pl.multiple_of((g * NAT_QROWS + rr) * GRID_W + j * NAT_WIN_COLS, NAT_WIN_COLS)
                return ref[pl.ds(start, NAT_WIN_COLS), :]

            def krows(ref, kr):
                start = pl.multiple_of((kb + kr) * GRID_W + ks, SUBLANES)
                return ref[pl.ds(start, NAT_KCOLS), :]

            qs = jnp.concatenate([qrows(q_ref, rr) for rr in range(NAT_QROWS)], axis=0) * (NAT_HD ** -0.5)
            q2 = jnp.concatenate([jnp.where(head0, qs, 0.0), jnp.where(head0, 0.0, qs)], axis=0)
            kblk = jnp.concatenate([krows(k_ref, kr) for kr in range(NAT_KROWS)], axis=0)
            vblk = jnp.concatenate([krows(v_ref, kr) for kr in range(NAT_KROWS)], axis=0)
            s = _dot_nt(q2, kblk)
            j_cls = _edge_class(j, n_cb)
            s = s + jnp.concatenate([bias_ref[0, g_cls, j_cls], bias_ref[1, g_cls, j_cls]], axis=0)
            m = jnp.max(s, axis=-1, keepdims=True)
            p = jnp.exp(s - m)
            vext = jnp.concatenate([vblk, jnp.ones_like(vblk)], axis=1)
            o_ext = _dot(p, vext)
            o2 = o_ext[:, :2 * NAT_HD] / o_ext[:, 2 * NAT_HD:]
            o = jnp.where(head0, o2[:nq], o2[nq:])
            gs = jnp.concatenate([qrows(g_ref, rr) for rr in range(NAT_QROWS)], axis=0)
            res = (o * _silu(gs)).astype(out_ref.dtype)
            for rr in range(NAT_QROWS):
                start = pl.multiple_of((g * NAT_QROWS + rr) * GRID_W + j * NAT_WIN_COLS, NAT_WIN_COLS)
                out_ref[pl.ds(start, NAT_WIN_COLS), :] = res[rr * NAT_WIN_COLS:(rr + 1) * NAT_WIN_COLS]
        return carry

    lax.fori_loop(0, n_g, group_body, 0, unroll=4)


def _nat(z3, bias_tab):
    b, t, _ = z3.shape
    n_hp = NAT_HEADS // 2
    w = 2 * NAT_HD
    return pl.pallas_call(
        _nat_kernel,
        grid=(n_hp, b),
        in_specs=[
            _zspec(t, w, OFF_NQ, 1),
            _zspec(t, w, OFF_NK, 1),
            _zspec(t, w, OFF_NV, 1),
            _zspec(t, w, OFF_NG, 1),
            pl.BlockSpec((None,) + bias_tab.shape[1:], lambda h, i: (h, 0, 0, 0, 0, 0)),
        ],
        out_specs=pl.BlockSpec((None, t, w), lambda h, i: (i, 0, h)),
        out_shape=jax.ShapeDtypeStruct((b, t, BRANCH_W), BF16),
        compiler_params=_params("parallel", "parallel"),
        name="nat",
    )(z3, z3, z3, z3, bias_tab)


def _rglru_kernel(x_ref, g_ref, cw_ref, cb_ref, wg_ref, bg_ref, lam_ref, out_ref,
                  xp_scr, af_scr, bf_scr, ab_scr, bb_scr, hf_scr, pf_scr, hb_scr, pb_scr):
    t = x_ref.shape[0]
    n_seg = SUBLANES
    tc = t // n_seg
    pad = SUBLANES
    w = LANES

    xp_scr[0:pad, :] = jnp.zeros((pad, w), F32)
    xp_scr[pad + t:pad + t + pad, :] = jnp.zeros((pad, w), F32)
    xp_scr[pad:pad + t, :] = x_ref[...]

    lam = lam_ref[...]
    neg = -lam
    softplus = jnp.maximum(neg, 0.0) + jnp.log(1.0 + jnp.exp(-jnp.abs(neg)))
    log2_a_per_r = softplus * float(-LRU_C * np.log2(np.e))
    cw = cw_ref[...]
    cb = cb_ref[...]
    bg = bg_ref[...]

    def gate_body(n, carry):
        t0 = pl.multiple_of(n * tc, tc)
        xx = xp_scr[pl.ds(t0, tc + 2 * pad), :]
        total = tc + 2 * pad
        xc = cb
        for j in range(LRU_CONV):
            shift = LRU_CONV // 2 - j
            xs = xx if shift == 0 else pltpu.roll(xx, shift % total, 0)
            xc = xc + xs[pad:pad + tc] * cw[j:j + 1, :]
        gates = _dot(xc, wg_ref[...]) + bg
        seg_rows = pl.ds(n, tc, stride=n_seg)
        for d, (a_scr, b_scr) in enumerate(((af_scr, bf_scr), (ab_scr, bb_scr))):
            r = _sigmoid(gates[:, (2 * d) * w:(2 * d + 1) * w])
            i = _sigmoid(gates[:, (2 * d + 1) * w:(2 * d + 2) * w])
            a = jnp.exp2(r * log2_a_per_r[d:d + 1, :])
            a_scr[seg_rows, :] = a
            gap = 1.0 - a * a
            b_scr[seg_rows, :] = gap * lax.rsqrt(jnp.maximum(gap, 1e-30)) * (i * xc)
        return carry

    lax.fori_loop(0, n_seg, gate_body, 0, unroll=2)

    def scan_body(i, carry):
        hf, pf, hb, pb = carry
        sl = pl.ds(pl.multiple_of(i * n_seg, n_seg), n_seg)
        a = af_scr[sl, :]
        hf = a * hf + bf_scr[sl, :]
        pf = a * pf
        hf_scr[sl, :] = hf
        pf_scr[sl, :] = pf
        slb = pl.ds(pl.multiple_of((tc - 1 - i) * n_seg, n_seg), n_seg)
        a = ab_scr[slb, :]
        hb = a * hb + bb_scr[slb, :]
        pb = a * pb
        hb_scr[slb, :] = hb
        pb_scr[slb, :] = pb
        return hf, pf, hb, pb

    zero = jnp.zeros((n_seg, w), F32)
    one = jnp.ones((n_seg, w), F32)
    hf, pf, hb, pb = lax.fori_loop(0, tc, scan_body, (zero, one, zero, one), unroll=8)

    row = lax.broadcasted_iota(jnp.int32, (n_seg, w), 0)
    for s in (1, 2, 4):
        keep = row >= s
        hf = jnp.where(keep, pf * pltpu.roll(hf, s, 0) + hf, hf)
        pf = jnp.where(keep, pf * pltpu.roll(pf, s, 0), pf)
        keep = row < n_seg - s
        hb = jnp.where(keep, pb * pltpu.roll(hb, n_seg - s, 0) + hb, hb)
        pb = jnp.where(keep, pb * pltpu.roll(pb, n_seg - s, 0), pb)
    carry_f = jnp.where(row >= 1, pltpu.roll(hf, 1, 0), 0.0)
    carry_b = jnp.where(row < n_seg - 1, pltpu.roll(hb, n_seg - 1, 0), 0.0)

    def fix_body(i, carry):
        sl = pl.ds(pl.multiple_of(i * n_seg, n_seg), n_seg)
        af_scr[sl, :] = (hf_scr[sl, :] + pf_scr[sl, :] * carry_f) + (hb_scr[sl, :] + pb_scr[sl, :] * carry_b)
        return carry

    lax.fori_loop(0, tc, fix_body, 0, unroll=8)

    def out_body(n, carry):
        h = af_scr[pl.ds(n, tc, stride=n_seg), :]
        sl = pl.ds(pl.multiple_of(n * tc, tc), tc)
        out_ref[sl, :] = (h * _silu(g_ref[sl, :])).astype(out_ref.dtype)
        return carry

    lax.fori_loop(0, n_seg, out_body, 0)


def _rglru(z3, conv_w, conv_b, wg, bg, lam):
    b, t, _ = z3.shape
    n_cb = BRANCH_W // LANES
    return pl.pallas_call(
        _rglru_kernel,
        grid=(b, n_cb),
        in_specs=[
            _zspec(t, LANES, OFF_LX, 0),
            _zspec(t, LANES, OFF_LG, 0),
            pl.BlockSpec((LRU_CONV, LANES), lambda i, h: (0, h)),
            pl.BlockSpec((1, LANES), lambda i, h: (0, h)),
            pl.BlockSpec((None, LANES, 4 * LANES), lambda i, h: (h, 0, 0)),
            pl.BlockSpec((None, 1, 4 * LANES), lambda i, h: (h, 0, 0)),
            pl.BlockSpec((2, LANES), lambda i, h: (0, h)),
        ],
        out_specs=pl.BlockSpec((None, t, LANES), lambda i, h: (i, 0, h)),
        out_shape=jax.ShapeDtypeStruct((b, t, BRANCH_W), BF16),
        scratch_shapes=[pltpu.VMEM((t + 2 * SUBLANES, LANES), F32)] + [pltpu.VMEM((t, LANES), F32)] * 8,
        compiler_params=_params("parallel", "parallel"),
        name="rglru",
    )(z3, z3, conv_w, conv_b.reshape(1, BRANCH_W), wg, bg, lam)


def _rglru_gate_weights(wa, ba, wx, bx):
    n_cb = BRANCH_W // LANES
    per = LANES // LRU_BW

    def blockdiag(w):
        w = w.reshape(n_cb, per, LRU_BW, LRU_BW)
        eye = jnp.eye(per, dtype=w.dtype)
        return jnp.einsum('cpjk,pq->cpjqk', w, eye).reshape(n_cb, LANES, LANES)

    wg = jnp.concatenate([blockdiag(wa[0]), blockdiag(wx[0]), blockdiag(wa[1]), blockdiag(wx[1])], axis=-1)
    bg = jnp.concatenate([ba[0].reshape(n_cb, 1, LANES), bx[0].reshape(n_cb, 1, LANES),
                          ba[1].reshape(n_cb, 1, LANES), bx[1].reshape(n_cb, 1, LANES)], axis=-1)
    return wg.astype(BF16), bg.astype(F32)


def _prefix_sum_rows(x):
    c, w = x.shape
    x3 = x.reshape(c // SUBLANES, SUBLANES, w)
    row = lax.broadcasted_iota(jnp.int32, (1, SUBLANES, w), 1)
    s = 1
    while s < SUBLANES:
        x3 = x3 + jnp.where(row >= s, pltpu.roll(x3, s, 1), 0.0)
        s *= 2
    tile_tot = jnp.broadcast_to(x3[:, SUBLANES - 1:, :], x3.shape).reshape(c, w)
    x = x3.reshape(c, w)
    while s < c:
        shifted = jnp.concatenate([jnp.zeros((s, w), x.dtype), tile_tot[:c - s]], axis=0)
        x = x + shifted
        tile_tot = tile_tot + shifted
        s *= 2
    return x


def _hgrn_forget(zf, lb):
    f = lb + (1.0 - lb) * _sigmoid(zf)
    return f, jnp.log2(f)


def _hgrn_tile_decays(size, f3, fb3, bs3, cs3, pos):
    half = size // 2
    if size == 2:
        return jnp.where(pos == 1, f3, 1.0), jnp.where(pos == 0, fb3, 1.0)
    if size == 4:
        f_prev, f_next = pltpu.roll(f3, 1, 1), pltpu.roll(f3, SUBLANES - 1, 1)
        fb_prev, fb_next = pltpu.roll(fb3, 1, 1), pltpu.roll(fb3, SUBLANES - 1, 1)
        e_f = jnp.where(pos == 0, f_next, jnp.where(pos == 1, 1.0, jnp.where(pos == 2, f3, f_prev * f3)))
        e_b = jnp.where(pos == 0, fb3 * fb_next, jnp.where(pos == 1, fb3, jnp.where(pos == 2, 1.0, fb_prev)))
        return e_f, e_b
    sign = jnp.where(pos >= half, 1.0, -1.0)
    e_f = jnp.exp2((bs3 - bs3[:, half - 1:half, :]) * sign)
    e_b = jnp.exp2((cs3[:, half:half + 1, :] - cs3) * sign)
    return e_f, e_b


def _hgrn_level_operands(size, q, kf, kb, f, fb, bs, cs):
    c, w = q.shape
    half = size // 2
    if size <= SUBLANES:
        tiled = lambda a: a.reshape(c // SUBLANES, SUBLANES, w)
        pos = lax.broadcasted_iota(jnp.int32, (1, SUBLANES, w), 1) % size
        upper = pos >= half
        e_f, e_b = _hgrn_tile_decays(size, tiled(f), tiled(fb), tiled(bs), tiled(cs), pos)
        z_f = e_f * jnp.where(upper, tiled(q), tiled(kf))
        z_b = e_b * jnp.where(upper, tiled(kb), tiled(q))
        x = jnp.concatenate([jnp.where(upper, z_f, 0.0), jnp.where(upper, 0.0, z_b)], axis=2)
        y = jnp.concatenate([z_f, z_b], axis=2)
        return x.reshape(c, 2 * w), y.reshape(c, 2 * w)
    xs, ys = [], []
    zero = jnp.zeros((half, w), F32)
    for i in range(c // half):
        rows = slice(i * half, (i + 1) * half)
        if i % 2 == 0:
            ref_f = bs[(i + 1) * half - 1:(i + 1) * half]
            ref_b = cs[(i + 1) * half:(i + 1) * half + 1]
            z_f = kf[rows] * jnp.exp2(ref_f - bs[rows])
            z_b = q[rows] * jnp.exp2(cs[rows] - ref_b)
            xs.append(jnp.concatenate([zero, z_b], axis=1))
        else:
            ref_f = bs[i * half - 1:i * half]
            ref_b = cs[i * half:i * half + 1]
            z_f = q[rows] * jnp.exp2(bs[rows] - ref_f)
            z_b = kb[rows] * jnp.exp2(ref_b - cs[rows])
            xs.append(jnp.concatenate([z_f, zero], axis=1))
        ys.append(jnp.concatenate([z_f, z_b], axis=1))
    return jnp.concatenate(xs, axis=0), jnp.concatenate(ys, axis=0)


def _hgrn_kernel(q_ref, ff_ref, fb_ref, v_ref, g_ref, lbl_ref, gain_ref, h_ref, wmg_ref, out_ref, gate_ref,
                 fb_scr, cs_scr, sb_scr, st_scr, *, layer):
    t = q_ref.shape[0]
    c = HGRN_CHUNK
    n_chunks = t // c
    w = HGRN_DK

    logits = lbl_ref[...]
    mx = jnp.max(logits, axis=0)
    ex = jnp.exp(logits - mx[None])
    tot = jnp.sum(ex, axis=0)
    lb = jnp.zeros_like(tot)
    for i in range(1, layer + 1):
        lb = lb + ex[i] / tot
    lb_f = lb[0:1, :]
    lb_b = lb[1:2, :]

    pair_xor = lax.broadcasted_iota(jnp.int32, (c, c), 0) ^ lax.broadcasted_iota(jnp.int32, (c, c), 1)
    pair_level = jnp.zeros((c, c), jnp.int32)
    size = 2
    while size <= c:
        pair_level = jnp.where(pair_xor >= size // 2, size, pair_level)
        size *= 2

    st_scr[...] = jnp.zeros_like(st_scr)

    def bwd_body(i, carry):
        n = n_chunks - 1 - i
        sl = pl.ds(pl.multiple_of(n * c, c), c)
        fb, gb = _hgrn_forget(fb_ref[sl, :], lb_b)
        pre = _prefix_sum_rows(gb)
        total = pre[c - 1:c, :]
        cs = total - pre + gb
        fb_scr[sl, :] = fb
        cs_scr[sl, :] = cs
        sb_scr[n] = st_scr[...]
        st_scr[...] = st_scr[...] * jnp.exp2(total) + _dot_tn(v_ref[sl, :], (1.0 - fb) * jnp.exp2(total - cs))
        return carry

    lax.fori_loop(0, n_chunks, bwd_body, 0, unroll=8)

    st_scr[...] = jnp.zeros_like(st_scr)

    def fwd_chunk(n):
        sl = pl.ds(pl.multiple_of(n * c, c), c)
        q = _silu(q_ref[sl, :])
        v = v_ref[sl, :]
        f, gf = _hgrn_forget(ff_ref[sl, :], lb_f)
        kf = 1.0 - f
        bs = _prefix_sum_rows(gf)
        fb = fb_scr[sl, :]
        kb = 1.0 - fb
        cs = cs_scr[sl, :]

        att = None
        size = 2
        while size <= c:
            x, y = _hgrn_level_operands(size, q, kf, kb, f, fb, bs, cs)
            att = jnp.where(pair_level == size, _dot_nt(x, y), 0.0 if att is None else att)
            size *= 2

        diag = jnp.sum(q * (kf + kb), axis=-1, keepdims=True)
        o = _dot(att, v) + diag * v
        inter = jnp.concatenate([q * jnp.exp2(bs), q * jnp.exp2(cs)], axis=1)
        states = jnp.concatenate([st_scr[...], sb_scr[n]], axis=1)
        o = o + _dot_nt(inter, states)
        last = bs[c - 1:c, :]
        st_scr[...] = st_scr[...] * jnp.exp2(last) + _dot_tn(v, kf * jnp.exp2(last - bs))
        o = o * lax.rsqrt(jnp.mean(o * o, axis=-1, keepdims=True) + EPS) * gain_ref[...]
        out_ref[sl, :] = (o * _silu(g_ref[sl, :])).astype(out_ref.dtype)

    chunks_per_block = min(HGRN_GATE_CHUNKS, n_chunks)
    gate_rows = chunks_per_block * c

    col_pieces = D_MODEL // GATE_PIECE_COLS
    assert chunks_per_block % col_pieces == 0
    chunks_per_piece = chunks_per_block // col_pieces

    def block_body(m, carry):
        rows = pl.ds(pl.multiple_of(m * gate_rows, gate_rows), gate_rows)
        for u in range(chunks_per_block):
            if u % chunks_per_piece == 0 and u // chunks_per_piece < col_pieces:
                c0 = (u // chunks_per_piece) * GATE_PIECE_COLS
                gate_ref[rows, c0:c0 + GATE_PIECE_COLS] = jnp.dot(
                    h_ref[rows, :], wmg_ref[:, c0:c0 + GATE_PIECE_COLS],
                    preferred_element_type=F32).astype(gate_ref.dtype)
            fwd_chunk(m * chunks_per_block + u)
        return carry

    lax.fori_loop(0, n_chunks // chunks_per_block, block_body, 0)


def _hgrn(z3, lb_logits, gain, h3, wmg, layer):
    b, t, _ = z3.shape
    depth = lb_logits.shape[0]
    w = HGRN_DK
    n_chunks = t // HGRN_CHUNK
    return pl.pallas_call(
        functools.partial(_hgrn_kernel, layer=layer),
        grid=(b, HGRN_HEADS),
        in_specs=[
            _zspec(t, w, OFF_HQ, 0),
            _zspec(t, w, OFF_HFF, 0),
            _zspec(t, w, OFF_HFB, 0),
            _zspec(t, w, OFF_HI, 0),
            _zspec(t, w, OFF_HG, 0),
            pl.BlockSpec((depth, 2, w), lambda i, h: (0, 0, h)),
            pl.BlockSpec((1, w), lambda i, h: (0, h)),
            pl.BlockSpec((None, t, D_MODEL), lambda i, h: (i, 0, 0)),
            pl.BlockSpec((None, D_MODEL, D_MODEL), lambda i, h: (h, 0, 0)),
        ],
        out_specs=[
            pl.BlockSpec((None, t, w), lambda i, h: (i, 0, h)),
            pl.BlockSpec((None, t, D_MODEL), lambda i, h: (i, 0, h)),
        ],
        out_shape=[
            jax.ShapeDtypeStruct((b, t, BRANCH_W), BF16),
            jax.ShapeDtypeStruct((b, t, N_BRANCH * D_MODEL), BF16),
        ],
        scratch_shapes=[
            pltpu.VMEM((t, w), F32),
            pltpu.VMEM((t, w), F32),
            pltpu.VMEM((n_chunks, w, w), F32),
            pltpu.VMEM((w, w), F32),
        ],
        compiler_params=_params("parallel", "parallel"),
        name="hgrn",
    )(z3, z3, z3, z3, z3, lb_logits, gain.reshape(1, -1), h3, wmg)


def _encoder(x, p, w):
    b, t, _ = x.shape
    depth = w['w_in'].shape[0]
    x2d = x.reshape(b * t, D_MODEL)
    for l in range(depth):
        z2d, h2d = _inproj(x2d, w['norm_mix'][l], w['w_in'][l])
        z3 = z2d.reshape(b, t, W_IN)
        br_a = _retention(z3, w['cos'], w['sin'], w['lgq'][l], w['lgv'][l])
        br_b = _nat(z3, w['nat_bias'][l])
        br_c = _rglru(z3, w['conv_w'][l], w['conv_b'][l], w['lru_wg'][l], w['lru_bg'][l], w['lam'][l])
        br_d, gates = _hgrn(z3, w['lb_logits'], w['hgrn_gain'][l], h2d.reshape(b, t, D_MODEL),
                            w['w_merge'][l], l)
        branches = [a.reshape(b * t, BRANCH_W) for a in (br_a, br_b, br_c, br_d)]
        x2d = _merge(x2d, branches, gates.reshape(b * t, N_BRANCH * D_MODEL), p[l].reshape(b * t, PLE_DIM),
                     w['w_branch'][l], w['w_out'][l], w['ple_norm'][l], w['w_ple_gate'][l],
                     w['w_ple_proj'][l], w['final_norm'], l == depth - 1)
    return x2d.reshape(b, t, D_MODEL)


def kernel(x_prompt, x_sample, p_prompt, p_sample, norm_mix, w_in, ret_decay_logit, nat_rpb, lru_conv_w,
           lru_conv_b, lru_wa, lru_ba, lru_wx, lru_bx, lru_lambda, hgrn_lb_logits, hgrn_norm, w_branch,
           w_merge, w_out, ple_norm, w_ple_gate, w_ple_proj, final_norm):
    depth = w_in.shape[0]
    t = x_prompt.shape[1]
    rows = t // GRID_W
    cos_tab, sin_tab = _rotary_tables(t)
    gate_w = [_rglru_gate_weights(lru_wa[l], lru_ba[l], lru_wx[l], lru_bx[l]) for l in range(depth)]
    weights = {
        'norm_mix': norm_mix,
        'w_in': jnp.concatenate([_pair_rotary_layout(w_in[..., OFF_RQ:OFF_RK]),
                                 _pair_rotary_layout(w_in[..., OFF_RK:OFF_RV]),
                                 w_in[..., OFF_RV:]], axis=-1).astype(BF16),
        'cos': cos_tab,
        'sin': sin_tab,
        'lgq': _pair_rotary_layout(jnp.repeat(ret_decay_logit.astype(F32), RET_QK, axis=-1)),
        'lgv': jnp.repeat(ret_decay_logit.astype(F32), RET_V, axis=-1),
        'nat_bias': [_nat_bias_tables(nat_rpb[l], rows) for l in range(depth)],
        'conv_w': lru_conv_w,
        'conv_b': lru_conv_b,
        'lru_wg': [g[0] for g in gate_w],
        'lru_bg': [g[1] for g in gate_w],
        'lam': lru_lambda,
        'lb_logits': hgrn_lb_logits,
        'hgrn_gain': hgrn_norm,
        'w_branch': w_branch.astype(BF16),
        'w_merge': w_merge.astype(BF16),
        'w_out': w_out.astype(BF16),
        'ple_norm': ple_norm,
        'w_ple_gate': w_ple_gate.astype(BF16),
        'w_ple_proj': w_ple_proj.astype(BF16),
        'final_norm': final_norm,
    }
    y_prompt = _encoder(x_prompt, p_prompt, weights)
    y_sample = _encoder(x_sample, p_sample, weights)
    return (y_prompt, y_sample)
```

```python
import functools

import numpy as np
import jax
import jax.numpy as jnp
from jax import lax
from jax.experimental import pallas as pl
from jax.experimental.pallas import tpu as pltpu

F32 = jnp.float32
BF16 = jnp.bfloat16

D_MODEL = 1024
PLE_DIM = 256
GRID_W = 64
N_BRANCH = 4
BRANCH_W = 512
RET_HEADS = 4
RET_QK = 64
RET_V = 128
ROPE_BASE = 10000.0
NAT_HEADS = 8
NAT_HD = 64
NAT_WIN_ROWS = 8
NAT_WIN_COLS = 16
LRU_BLOCKS = 8
LRU_BW = 64
LRU_CONV = 4
LRU_C = 8.0
HGRN_HEADS = 4
HGRN_DK = 128
EPS = 1e-6
LOG2E = float(np.log2(np.e))
W_IN = 7168

OFF_RQ, OFF_RK, OFF_RV, OFF_RG = 0, 256, 512, 1024
OFF_NQ, OFF_NK, OFF_NV, OFF_NG = 1536, 2048, 2560, 3072
OFF_LX, OFF_LG = 3584, 4096
OFF_HQ, OFF_HFF, OFF_HFB, OFF_HI, OFF_HG = 4608, 5120, 5632, 6144, 6656

LANES = 128
SUBLANES = 8
VMEM_LIMIT = 56 * 1024 * 1024

CHUNK = 128
HGRN_CHUNK = 128
HGRN_GATE_CHUNKS = 8
GATE_PIECE_COLS = 256


def _params(*sem):
    return pltpu.CompilerParams(dimension_semantics=sem, vmem_limit_bytes=VMEM_LIMIT)


def _dot(a, b):
    return jnp.dot(a.astype(BF16), b.astype(BF16), preferred_element_type=F32)


def _dot_nt(a, b):
    return lax.dot_general(a.astype(BF16), b.astype(BF16), (((1,), (1,)), ((), ())),
                           preferred_element_type=F32)


def _dot_tn(a, b):
    return lax.dot_general(a.astype(BF16), b.astype(BF16), (((0,), (0,)), ((), ())),
                           preferred_element_type=F32)


def _rms(x, g):
    return x * lax.rsqrt(jnp.mean(x * x, axis=-1, keepdims=True) + EPS) * g


def _sigmoid(x):
    return jax.nn.sigmoid(x)


def _silu(x):
    return x * jax.nn.sigmoid(x)


def _zspec(t, width, off, grid_pos):
    base = off // width
    if grid_pos == 0:
        return pl.BlockSpec((None, t, width), lambda b, h: (b, 0, base + h))
    return pl.BlockSpec((None, t, width), lambda h, b: (b, 0, base + h))


def _inproj_kernel(x_ref, g_ref, w_ref, z_ref, h_ref):
    @pl.when(pl.program_id(1) == 0)
    def _():
        h_ref[...] = _rms(x_ref[...], g_ref[...]).astype(h_ref.dtype)

    z_ref[...] = jnp.dot(h_ref[...], w_ref[...], preferred_element_type=F32)


def _inproj(x2d, g, w_bf16):
    m = x2d.shape[0]
    tm, tn = 2048, 1024
    return pl.pallas_call(
        _inproj_kernel,
        grid=(m // tm, W_IN // tn),
        in_specs=[
            pl.BlockSpec((tm, D_MODEL), lambda i, j: (i, 0)),
            pl.BlockSpec((1, D_MODEL), lambda i, j: (0, 0)),
            pl.BlockSpec((D_MODEL, tn), lambda i, j: (0, j)),
        ],
        out_specs=[
            pl.BlockSpec((tm, tn), lambda i, j: (i, j)),
            pl.BlockSpec((tm, D_MODEL), lambda i, j: (i, 0)),
        ],
        out_shape=[
            jax.ShapeDtypeStruct((m, W_IN), F32),
            jax.ShapeDtypeStruct((m, D_MODEL), BF16),
        ],
        compiler_params=_params("parallel", "arbitrary"),
        name="inproj",
    )(x2d, g.reshape(1, D_MODEL), w_bf16)


def _merge_kernel(x_ref, ba_ref, bb_ref, bc_ref, bd_ref, gates_ref, p_ref, wbr_ref, wo_ref,
                  gple_ref, wpg_ref, wpp_ref, gfin_ref, out_ref, *, final):
    x = x_ref[...]
    merged = None
    for j, b_ref in enumerate((ba_ref, bb_ref, bc_ref, bd_ref)):
        gate = _sigmoid(gates_ref[:, j * D_MODEL:(j + 1) * D_MODEL].astype(F32))
        term = gate * jnp.dot(b_ref[...], wbr_ref[j], preferred_element_type=F32)
        merged = term if merged is None else merged + term
    x1 = x + _dot(merged, wo_ref[...])
    gate2 = _sigmoid(_dot(_rms(x1, gple_ref[...]), wpg_ref[...]))
    x2 = x1 + gate2 * _dot(p_ref[...], wpp_ref[...])
    if final:
        x2 = _rms(x2, gfin_ref[...])
    out_ref[...] = x2


def _merge(x2d, branches, gates, p2d, wbr, wo, gple, wpg, wpp, gfin, final):
    m = x2d.shape[0]
    tm = 512
    row = lambda i: (i, 0)
    const2 = lambda i: (0, 0)
    const3 = lambda i: (0, 0, 0)
    once = pl.Buffered(1)
    vec = pl.BlockSpec((1, D_MODEL), const2)
    return pl.pallas_call(
        functools.partial(_merge_kernel, final=final),
        grid=(m // tm,),
        in_specs=[
            pl.BlockSpec((tm, D_MODEL), row),
            pl.BlockSpec((tm, BRANCH_W), row),
            pl.BlockSpec((tm, BRANCH_W), row),
            pl.BlockSpec((tm, BRANCH_W), row),
            pl.BlockSpec((tm, BRANCH_W), row),
            pl.BlockSpec((tm, N_BRANCH * D_MODEL), row),
            pl.BlockSpec((tm, PLE_DIM), row),
            pl.BlockSpec((N_BRANCH, BRANCH_W, D_MODEL), const3, pipeline_mode=once),
            pl.BlockSpec((D_MODEL, D_MODEL), const2, pipeline_mode=once),
            vec,
            pl.BlockSpec((D_MODEL, D_MODEL), const2, pipeline_mode=once),
            pl.BlockSpec((PLE_DIM, D_MODEL), const2, pipeline_mode=once),
            vec,
        ],
        out_specs=pl.BlockSpec((tm, D_MODEL), row),
        out_shape=jax.ShapeDtypeStruct((m, D_MODEL), F32),
        compiler_params=_params("parallel"),
        name="merge",
    )(x2d, *branches, gates, p2d, wbr, wo, gple.reshape(1, -1), wpg, wpp, gfin.reshape(1, -1))


def _retention_kernel(q_ref, k_ref, v_ref, g_ref, cos_ref, sin_ref, lgq_ref, lgv_ref, out_ref,
                      kr_scr, sb_scr, sf_scr):
    t = q_ref.shape[0]
    c = CHUNK
    n_chunks = t // c
    hd = RET_QK

    lane = lax.broadcasted_iota(jnp.int32, (1, 2 * hd), 1)
    head0_q = (lane // (hd // 2)) % 2 == 0
    lane_v = lax.broadcasted_iota(jnp.int32, (1, 2 * RET_V), 1)
    head0_v = lane_v < RET_V

    def rotary(x, cos, sin):
        return x * cos + pltpu.roll(x, hd, 1) * sin

    lg_f = -jnp.log(1.0 + jnp.exp(-lgq_ref[0:1, :]))
    lg_b = -jnp.log(1.0 + jnp.exp(-lgq_ref[1:2, :]))
    lgv_f = -jnp.log(1.0 + jnp.exp(-lgv_ref[0:1, :]))
    lgv_b = -jnp.log(1.0 + jnp.exp(-lgv_ref[1:2, :]))
    tcol = lax.broadcasted_iota(jnp.int32, (c, 1), 0).astype(F32)
    head_f = jnp.exp((tcol + 1.0) * lg_f)
    head_b = jnp.exp((c - tcol) * lg_b)
    tail_f = jnp.exp((c - 1.0 - tcol) * lg_f)
    tail_b = jnp.exp(tcol * lg_b)
    dec_f = jnp.exp(c * lgv_f)
    dec_b = jnp.exp(c * lgv_b)
    rowk = lax.broadcasted_iota(jnp.int32, (2 * hd, 2 * RET_V), 0)
    colv = lax.broadcasted_iota(jnp.int32, (2 * hd, 2 * RET_V), 1)
    blockdiag = ((rowk // (hd // 2)) % 2) == (colv // RET_V)

    diff = (lax.broadcasted_iota(jnp.int32, (c, c), 0) - lax.broadcasted_iota(jnp.int32, (c, c), 1)).astype(F32)

    def decay_mask(lf, lb):
        fwd = jnp.exp(jnp.maximum(diff, 0.0) * lf)
        bwd = jnp.exp(jnp.maximum(-diff, 0.0) * lb)
        return jnp.where(diff > 0, fwd, jnp.where(diff < 0, bwd, 2.0))

    dmask = jnp.concatenate([decay_mask(lgv_f[:, :c], lgv_b[:, :c]),
                             decay_mask(lgv_f[:, RET_V:RET_V + c], lgv_b[:, RET_V:RET_V + c])], axis=1)

    sf_scr[...] = jnp.zeros_like(sf_scr)

    def bwd_body(i, carry):
        n = n_chunks - 1 - i
        sl = pl.ds(pl.multiple_of(n * c, c), c)
        sb_scr[n] = sf_scr[...]
        kn = rotary(k_ref[sl, :], cos_ref[sl, :], sin_ref[sl, :]) * (hd ** -0.5)
        kr_scr[sl, :] = kn
        loc = _dot_tn(kn * tail_b, v_ref[sl, :])
        sf_scr[...] = sf_scr[...] * dec_b + jnp.where(blockdiag, loc, 0.0)
        return carry

    lax.fori_loop(0, n_chunks, bwd_body, 0, unroll=8)

    sf_scr[...] = jnp.zeros_like(sf_scr)

    def fwd_body(n, carry):
        sl = pl.ds(pl.multiple_of(n * c, c), c)
        qn = rotary(q_ref[sl, :], cos_ref[sl, :], sin_ref[sl, :])
        kn = kr_scr[sl, :]
        vn = v_ref[sl, :]
        kstack = jnp.concatenate([jnp.where(head0_q, kn, 0.0), jnp.where(head0_q, 0.0, kn)], axis=0)
        scores = _dot_nt(qn, kstack) * dmask
        vstack = jnp.concatenate([jnp.where(head0_v, vn, 0.0), jnp.where(head0_v, 0.0, vn)], axis=0)
        lhs = jnp.concatenate([scores, qn * head_f, qn * head_b], axis=1)
        rhs = jnp.concatenate([vstack, sf_scr[...], sb_scr[n]], axis=0)
        o = _dot(lhs, rhs)
        loc = _dot_tn(kn * tail_f, vn)
        sf_scr[...] = sf_scr[...] * dec_f + jnp.where(blockdiag, loc, 0.0)
        o0 = o[:, :RET_V]
        o1 = o[:, RET_V:]
        o0 = o0 * lax.rsqrt(jnp.mean(o0 * o0, axis=-1, keepdims=True) + EPS)
        o1 = o1 * lax.rsqrt(jnp.mean(o1 * o1, axis=-1, keepdims=True) + EPS)
        on = jnp.concatenate([o0, o1], axis=1)
        out_ref[sl, :] = (on * _silu(g_ref[sl, :])).astype(out_ref.dtype)
        return carry

    lax.fori_loop(0, n_chunks, fwd_body, 0, unroll=8)


def _retention(z3, cos_tab, sin_tab, lgq, lgv):
    b, t, _ = z3.shape
    n_chunks = t // CHUNK
    return pl.pallas_call(
        _retention_kernel,
        grid=(b, RET_HEADS // 2),
        in_specs=[
            _zspec(t, 2 * RET_QK, OFF_RQ, 0),
            _zspec(t, 2 * RET_QK, OFF_RK, 0),
            _zspec(t, 2 * RET_V, OFF_RV, 0),
            _zspec(t, 2 * RET_V, OFF_RG, 0),
            pl.BlockSpec((t, 2 * RET_QK), lambda i, h: (0, 0)),
            pl.BlockSpec((t, 2 * RET_QK), lambda i, h: (0, 0)),
            pl.BlockSpec((2, 2 * RET_QK), lambda i, h: (0, h)),
            pl.BlockSpec((2, 2 * RET_V), lambda i, h: (0, h)),
        ],
        out_specs=pl.BlockSpec((None, t, 2 * RET_V), lambda i, h: (i, 0, h)),
        out_shape=jax.ShapeDtypeStruct((b, t, BRANCH_W), BF16),
        scratch_shapes=[
            pltpu.VMEM((t, 2 * RET_QK), F32),
            pltpu.VMEM((n_chunks, 2 * RET_QK, 2 * RET_V), F32),
            pltpu.VMEM((2 * RET_QK, 2 * RET_V), F32),
        ],
        compiler_params=_params("parallel", "parallel"),
        name="retention",
    )(z3, z3, z3, z3, cos_tab, sin_tab, lgq, lgv)


def _rotary_tables(t):
    half = RET_QK // 2
    inv = ROPE_BASE ** (-jnp.arange(half, dtype=F32) / half)
    ang = jnp.arange(t, dtype=F32)[:, None] * inv[None, :]
    cos = jnp.cos(ang)
    sin = jnp.sin(ang)
    cos_tab = jnp.tile(cos, (1, 4))
    sin_tab = jnp.concatenate([-sin, -sin, sin, sin], axis=1)
    return cos_tab, sin_tab


def _pair_rotary_layout(a):
    lead = a.shape[:-1]
    a = a.reshape(lead + (RET_HEADS // 2, 2, 2, RET_QK // 2))
    return jnp.swapaxes(a, -3, -2).reshape(lead + (RET_HEADS * RET_QK,))


NAT_QROWS = 8
NAT_KROWS = 16
NAT_KCOLS = 2 * NAT_WIN_COLS


def _nat_kstart(j):
    return int(np.clip(j * NAT_WIN_COLS - NAT_WIN_COLS // 2, 0, GRID_W - NAT_KCOLS))


def _nat_key_row_base(g, rows):
    return int(np.clip(g * NAT_QROWS - NAT_WIN_ROWS // 2, 0, rows - NAT_KROWS))


def _edge_class(i, n):
    return 0 if i == 0 else (2 if i == n - 1 else 1)


def _nat_bias_tables(rpb, rows):
    n_g = rows // NAT_QROWS
    n_cb = GRID_W // NAT_WIN_COLS

    def row_geometry(g):
        dr = np.zeros((NAT_QROWS, NAT_KROWS), np.int32)
        ok = np.zeros((NAT_QROWS, NAT_KROWS), bool)
        kb = _nat_key_row_base(g, rows)
        for rr in range(NAT_QROWS):
            r = g * NAT_QROWS + rr
            rs = int(np.clip(r - NAT_WIN_ROWS // 2, 0, rows - NAT_WIN_ROWS))
            for kr in range(NAT_KROWS):
                ka = kb + kr
                ok[rr, kr] = rs <= ka < rs + NAT_WIN_ROWS
                dr[rr, kr] = int(np.clip(ka - r + NAT_WIN_ROWS - 1, 0, 2 * NAT_WIN_ROWS - 2))
        return dr, ok

    def col_geometry(j):
        dc = np.zeros((NAT_WIN_COLS, NAT_KCOLS), np.int32)
        ok = np.zeros((NAT_WIN_COLS, NAT_KCOLS), bool)
        ks = _nat_kstart(j)
        for qq in range(NAT_WIN_COLS):
            qc = j * NAT_WIN_COLS + qq
            ws = int(np.clip(qc - NAT_WIN_COLS // 2, 0, GRID_W - NAT_WIN_COLS))
            for kc in range(NAT_KCOLS):
                ka = ks + kc
                ok[qq, kc] = ws <= ka < ws + NAT_WIN_COLS
                dc[qq, kc] = int(np.clip(ka - qc + NAT_WIN_COLS - 1, 0, 2 * NAT_WIN_COLS - 2))
        return dc, ok

    def by_class(geometry, n):
        reps = {}
        for i in range(n):
            dx, ok = geometry(i)
            cls = _edge_class(i, n)
            if cls in reps:
                assert (np.where(ok, dx, -1) == np.where(reps[cls][1], reps[cls][0], -1)).all()
            else:
                reps[cls] = (dx, ok)
        filled = [reps.get(cls, reps[0]) for cls in range(3)]
        return np.stack([f[0] for f in filled]), np.stack([f[1] for f in filled])

    dr, row_ok = by_class(row_geometry, n_g)
    dc, col_ok = by_class(col_geometry, n_cb)
    col_sel = (dc[..., None] == np.arange(2 * NAT_WIN_COLS - 1)).astype(np.float32)
    blocks = jnp.einsum('hab,jqcb->hjaqc', rpb.astype(F32), col_sel, precision=lax.Precision.HIGHEST)
    blocks = jnp.where(col_ok[None, :, None], blocks * LOG2E, -jnp.inf)
    nq = NAT_QROWS * NAT_WIN_COLS
    nk = NAT_KROWS * NAT_KCOLS

    def tile_kernel(blocks_ref, out_ref):
        masked = jnp.full((NAT_WIN_COLS, NAT_KCOLS), -jnp.inf, F32)
        for g in range(3):
            for j in range(3):
                for rr in range(NAT_QROWS):
                    pieces = [blocks_ref[j, int(dr[g, rr, kr])] if row_ok[g, rr, kr] else masked
                              for kr in range(NAT_KROWS)]
                    out_ref[g, j, rr * NAT_WIN_COLS:(rr + 1) * NAT_WIN_COLS, :] = jnp.concatenate(pieces, axis=1)

    return pl.pallas_call(
        tile_kernel,
        grid=(NAT_HEADS,),
        in_specs=[pl.BlockSpec((None,) + blocks.shape[1:], lambda h: (h, 0, 0, 0, 0))],
        out_specs=pl.BlockSpec((None, None, 3, 3, nq, nk), lambda h: (h // 2, h % 2, 0, 0, 0, 0)),
        out_shape=jax.ShapeDtypeStruct((NAT_HEADS // 2, 2, 3, 3, nq, nk), F32),
        compiler_params=_params("parallel"),
        name="nat_bias",
    )(blocks)


def _nat_kernel(q_ref, k_ref, v_ref, g_ref, bias_ref, out_ref):
    t = q_ref.shape[0]
    rows = t // GRID_W
    n_g = rows // NAT_QROWS
    n_cb = GRID_W // NAT_WIN_COLS
    nq = NAT_QROWS * NAT_WIN_COLS
    lane = lax.broadcasted_iota(jnp.int32, (1, 2 * NAT_HD), 1)
    head0 = lane < NAT_HD

    def group_body(g, carry):
        kb = jnp.clip(g * NAT_QROWS - NAT_WIN_ROWS // 2, 0, rows - NAT_KROWS)
        g_cls = jnp.where(g == 0, 0, jnp.where(g == n_g - 1, 2, 1))
        for j in range(n_cb):
            ks = _nat_kstart(j)

            def qrows(ref, rr):
                start = pl.multiple_of((g * NAT_QROWS + rr) * GRID_W + j * NAT_WIN_COLS, NAT_WIN_COLS)
                return ref[pl.ds(start, NAT_WIN_COLS), :]

            def krows(ref, kr):
                start = pl.multiple_of((kb + kr) * GRID_W + ks, SUBLANES)
                return ref[pl.ds(start, NAT_KCOLS), :]

            qs = jnp.concatenate([qrows(q_ref, rr) for rr in range(NAT_QROWS)], axis=0) * (NAT_HD ** -0.5 * LOG2E)
            q2 = jnp.concatenate([jnp.where(head0, qs, 0.0), jnp.where(head0, 0.0, qs)], axis=0)
            kblk = jnp.concatenate([krows(k_ref, kr) for kr in range(NAT_KROWS)], axis=0)
            vblk = jnp.concatenate([krows(v_ref, kr) for kr in range(NAT_KROWS)], axis=0)
            s = _dot_nt(q2, kblk)
            j_cls = _edge_class(j, n_cb)
            s = s + jnp.concatenate([bias_ref[0, g_cls, j_cls], bias_ref[1, g_cls, j_cls]], axis=0)
            m = jnp.max(s, axis=-1, keepdims=True)
            p = jnp.exp2(s - m)
            vext = jnp.concatenate([vblk, jnp.ones_like(vblk)], axis=1)
            o_ext = _dot(p, vext)
            o2 = o_ext[:, :2 * NAT_HD] / o_ext[:, 2 * NAT_HD:]
            o = jnp.where(head0, o2[:nq], o2[nq:])
            gs = jnp.concatenate([qrows(g_ref, rr) for rr in range(NAT_QROWS)], axis=0)
            res = (o * _silu(gs)).astype(out_ref.dtype)
            for rr in range(NAT_QROWS):
                start = pl.multiple_of((g * NAT_QROWS + rr) * GRID_W + j * NAT_WIN_COLS, NAT_WIN_COLS)
                out_ref[pl.ds(start, NAT_WIN_COLS), :] = res[rr * NAT_WIN_COLS:(rr + 1) * NAT_WIN_COLS]
        return carry

    lax.fori_loop(0, n_g, group_body, 0, unroll=4)


def _nat(z3, bias_tab):
    b, t, _ = z3.shape
    n_hp = NAT_HEADS // 2
    w = 2 * NAT_HD
    return pl.pallas_call(
        _nat_kernel,
        grid=(n_hp, b),
        in_specs=[
            _zspec(t, w, OFF_NQ, 1),
            _zspec(t, w, OFF_NK, 1),
            _zspec(t, w, OFF_NV, 1),
            _zspec(t, w, OFF_NG, 1),
            pl.BlockSpec((None,) + bias_tab.shape[1:], lambda h, i: (h, 0, 0, 0, 0, 0)),
        ],
        out_specs=pl.BlockSpec((None, t, w), lambda h, i: (i, 0, h)),
        out_shape=jax.ShapeDtypeStruct((b, t, BRANCH_W), BF16),
        compiler_params=_params("parallel", "parallel"),
        name="nat",
    )(z3, z3, z3, z3, bias_tab)


def _rglru_kernel(x_ref, g_ref, cw_ref, cb_ref, wg_ref, bg_ref, lam_ref, out_ref,
                  xp_scr, af_scr, bf_scr, ab_scr, bb_scr, hf_scr, pf_scr, hb_scr, pb_scr):
    t = x_ref.shape[0]
    n_seg = SUBLANES
    tc = t // n_seg
    pad = SUBLANES
    w = LANES

    xp_scr[0:pad, :] = jnp.zeros((pad, w), F32)
    xp_scr[pad + t:pad + t + pad, :] = jnp.zeros((pad, w), F32)
    xp_scr[pad:pad + t, :] = x_ref[...]

    lam = lam_ref[...]
    neg = -lam
    softplus = jnp.maximum(neg, 0.0) + jnp.log(1.0 + jnp.exp(-jnp.abs(neg)))
    log2_a_per_r = softplus * float(-LRU_C * np.log2(np.e))
    cw = cw_ref[...]
    cb = cb_ref[...]
    bg = bg_ref[...]

    def gate_body(n, carry):
        t0 = pl.multiple_of(n * tc, tc)
        xx = xp_scr[pl.ds(t0, tc + 2 * pad), :]
        total = tc + 2 * pad
        xc = cb
        for j in range(LRU_CONV):
            shift = LRU_CONV // 2 - j
            xs = xx if shift == 0 else pltpu.roll(xx, shift % total, 0)
            xc = xc + xs[pad:pad + tc] * cw[j:j + 1, :]
        gates = _dot(xc, wg_ref[...]) + bg
        seg_rows = pl.ds(n, tc, stride=n_seg)
        for d, (a_scr, b_scr) in enumerate(((af_scr, bf_scr), (ab_scr, bb_scr))):
            r = _sigmoid(gates[:, (2 * d) * w:(2 * d + 1) * w])
            i = _sigmoid(gates[:, (2 * d + 1) * w:(2 * d + 2) * w])
            a = jnp.exp2(r * log2_a_per_r[d:d + 1, :])
            a_scr[seg_rows, :] = a
            gap = 1.0 - a * a
            b_scr[seg_rows, :] = gap * lax.rsqrt(jnp.maximum(gap, 1e-30)) * (i * xc)
        return carry

    lax.fori_loop(0, n_seg, gate_body, 0, unroll=4)

    def scan_body(i, carry):
        hf, pf, hb, pb = carry
        sl = pl.ds(pl.multiple_of(i * n_seg, n_seg), n_seg)
        a = af_scr[sl, :]
        hf = a * hf + bf_scr[sl, :]
        pf = a * pf
        hf_scr[sl, :] = hf
        pf_scr[sl, :] = pf
        slb = pl.ds(pl.multiple_of((tc - 1 - i) * n_seg, n_seg), n_seg)
        a = ab_scr[slb, :]
        hb = a * hb + bb_scr[slb, :]
        pb = a * pb
        hb_scr[slb, :] = hb
        pb_scr[slb, :] = pb
        return hf, pf, hb, pb

    zero = jnp.zeros((n_seg, w), F32)
    one = jnp.ones((n_seg, w), F32)
    hf, pf, hb, pb = lax.fori_loop(0, tc, scan_body, (zero, one, zero, one), unroll=8)

    row = lax.broadcasted_iota(jnp.int32, (n_seg, w), 0)
    for s in (1, 2, 4):
        keep = row >= s
        hf = jnp.where(keep, pf * pltpu.roll(hf, s, 0) + hf, hf)
        pf = jnp.where(keep, pf * pltpu.roll(pf, s, 0), pf)
        keep = row < n_seg - s
        hb = jnp.where(keep, pb * pltpu.roll(hb, n_seg - s, 0) + hb, hb)
        pb = jnp.where(keep, pb * pltpu.roll(pb, n_seg - s, 0), pb)
    carry_f = jnp.where(row >= 1, pltpu.roll(hf, 1, 0), 0.0)
    carry_b = jnp.where(row < n_seg - 1, pltpu.roll(hb, n_seg - 1, 0), 0.0)

    def fix_body(i, carry):
        sl = pl.ds(pl.multiple_of(i * n_seg, n_seg), n_seg)
        af_scr[sl, :] = (hf_scr[sl, :] + pf_scr[sl, :] * carry_f) + (hb_scr[sl, :] + pb_scr[sl, :] * carry_b)
        return carry

    lax.fori_loop(0, tc, fix_body, 0, unroll=8)

    def out_body(n, carry):
        h = af_scr[pl.ds(n, tc, stride=n_seg), :]
        sl = pl.ds(pl.multiple_of(n * tc, tc), tc)
        out_ref[sl, :] = (h * _silu(g_ref[sl, :])).astype(out_ref.dtype)
        return carry

    lax.fori_loop(0, n_seg, out_body, 0)


def _rglru(z3, conv_w, conv_b, wg, bg, lam):
    b, t, _ = z3.shape
    n_cb = BRANCH_W // LANES
    return pl.pallas_call(
        _rglru_kernel,
        grid=(b, n_cb),
        in_specs=[
            _zspec(t, LANES, OFF_LX, 0),
            _zspec(t, LANES, OFF_LG, 0),
            pl.BlockSpec((LRU_CONV, LANES), lambda i, h: (0, h)),
            pl.BlockSpec((1, LANES), lambda i, h: (0, h)),
            pl.BlockSpec((None, LANES, 4 * LANES), lambda i, h: (h, 0, 0)),
            pl.BlockSpec((None, 1, 4 * LANES), lambda i, h: (h, 0, 0)),
            pl.BlockSpec((2, LANES), lambda i, h: (0, h)),
        ],
        out_specs=pl.BlockSpec((None, t, LANES), lambda i, h: (i, 0, h)),
        out_shape=jax.ShapeDtypeStruct((b, t, BRANCH_W), BF16),
        scratch_shapes=[pltpu.VMEM((t + 2 * SUBLANES, LANES), F32)] + [pltpu.VMEM((t, LANES), F32)] * 8,
        compiler_params=_params("parallel", "parallel"),
        name="rglru",
    )(z3, z3, conv_w, conv_b.reshape(1, BRANCH_W), wg, bg, lam)


def _rglru_gate_weights(wa, ba, wx, bx):
    n_cb = BRANCH_W // LANES
    per = LANES // LRU_BW

    def blockdiag(w):
        w = w.reshape(n_cb, per, LRU_BW, LRU_BW)
        eye = jnp.eye(per, dtype=w.dtype)
        return jnp.einsum('cpjk,pq->cpjqk', w, eye).reshape(n_cb, LANES, LANES)

    wg = jnp.concatenate([blockdiag(wa[0]), blockdiag(wx[0]), blockdiag(wa[1]), blockdiag(wx[1])], axis=-1)
    bg = jnp.concatenate([ba[0].reshape(n_cb, 1, LANES), bx[0].reshape(n_cb, 1, LANES),
                          ba[1].reshape(n_cb, 1, LANES), bx[1].reshape(n_cb, 1, LANES)], axis=-1)
    return wg.astype(BF16), bg.astype(F32)


def _prefix_sum_rows(x):
    c, w = x.shape
    x3 = x.reshape(c // SUBLANES, SUBLANES, w)
    row = lax.broadcasted_iota(jnp.int32, (1, SUBLANES, w), 1)
    s = 1
    while s < SUBLANES:
        x3 = x3 + jnp.where(row >= s, pltpu.roll(x3, s, 1), 0.0)
        s *= 2
    tile_tot = jnp.broadcast_to(x3[:, SUBLANES - 1:, :], x3.shape).reshape(c, w)
    x = x3.reshape(c, w)
    while s < c:
        shifted = jnp.concatenate([jnp.zeros((s, w), x.dtype), tile_tot[:c - s]], axis=0)
        x = x + shifted
        tile_tot = tile_tot + shifted
        s *= 2
    return x


def _hgrn_forget(zf, lb):
    f = lb + (1.0 - lb) * _sigmoid(zf)
    return f, jnp.log2(f)


def _hgrn_tile_decays(size, f3, fb3, bs3, cs3, pos):
    half = size // 2
    if size == 2:
        return jnp.where(pos == 1, f3, 1.0), jnp.where(pos == 0, fb3, 1.0)
    if size == 4:
        f_prev, f_next = pltpu.roll(f3, 1, 1), pltpu.roll(f3, SUBLANES - 1, 1)
        fb_prev, fb_next = pltpu.roll(fb3, 1, 1), pltpu.roll(fb3, SUBLANES - 1, 1)
        e_f = jnp.where(pos == 0, f_next, jnp.where(pos == 1, 1.0, jnp.where(pos == 2, f3, f_prev * f3)))
        e_b = jnp.where(pos == 0, fb3 * fb_next, jnp.where(pos == 1, fb3, jnp.where(pos == 2, 1.0, fb_prev)))
        return e_f, e_b
    sign = jnp.where(pos >= half, 1.0, -1.0)
    e_f = jnp.exp2((bs3 - bs3[:, half - 1:half, :]) * sign)
    e_b = jnp.exp2((cs3[:, half:half + 1, :] - cs3) * sign)
    return e_f, e_b


def _hgrn_level_operands(size, q, kf, kb, f, fb, bs, cs):
    c, w = q.shape
    half = size // 2
    if size <= SUBLANES:
        tiled = lambda a: a.reshape(c // SUBLANES, SUBLANES, w)
        pos = lax.broadcasted_iota(jnp.int32, (1, SUBLANES, w), 1) % size
        upper = pos >= half
        e_f, e_b = _hgrn_tile_decays(size, tiled(f), tiled(fb), tiled(bs), tiled(cs), pos)
        z_f = e_f * jnp.where(upper, tiled(q), tiled(kf))
        z_b = e_b * jnp.where(upper, tiled(kb), tiled(q))
        x = jnp.concatenate([jnp.where(upper, z_f, 0.0), jnp.where(upper, 0.0, z_b)], axis=2)
        y = jnp.concatenate([z_f, z_b], axis=2)
        return x.reshape(c, 2 * w), y.reshape(c, 2 * w)
    xs, ys = [], []
    zero = jnp.zeros((half, w), F32)
    for i in range(c // half):
        rows = slice(i * half, (i + 1) * half)
        if i % 2 == 0:
            ref_f = bs[(i + 1) * half - 1:(i + 1) * half]
            ref_b = cs[(i + 1) * half:(i + 1) * half + 1]
            z_f = kf[rows] * jnp.exp2(ref_f - bs[rows])
            z_b = q[rows] * jnp.exp2(cs[rows] - ref_b)
            xs.append(jnp.concatenate([zero, z_b], axis=1))
        else:
            ref_f = bs[i * half - 1:i * half]
            ref_b = cs[i * half:i * half + 1]
            z_f = q[rows] * jnp.exp2(bs[rows] - ref_f)
            z_b = kb[rows] * jnp.exp2(ref_b - cs[rows])
            xs.append(jnp.concatenate([z_f, zero], axis=1))
        ys.append(jnp.concatenate([z_f, z_b], axis=1))
    return jnp.concatenate(xs, axis=0), jnp.concatenate(ys, axis=0)


def _hgrn_kernel(q_ref, ff_ref, fb_ref, v_ref, g_ref, lbl_ref, gain_ref, h_ref, wmg_ref, out_ref, gate_ref,
                 fb_scr, cs_scr, sb_scr, st_scr, *, layer):
    t = q_ref.shape[0]
    c = HGRN_CHUNK
    n_chunks = t // c
    w = HGRN_DK

    logits = lbl_ref[...]
    mx = jnp.max(logits, axis=0)
    ex = jnp.exp(logits - mx[None])
    tot = jnp.sum(ex, axis=0)
    lb = jnp.zeros_like(tot)
    for i in range(1, layer + 1):
        lb = lb + ex[i] / tot
    lb_f = lb[0:1, :]
    lb_b = lb[1:2, :]

    pair_xor = lax.broadcasted_iota(jnp.int32, (c, c), 0) ^ lax.broadcasted_iota(jnp.int32, (c, c), 1)
    pair_level = jnp.zeros((c, c), jnp.int32)
    size = 2
    while size <= c:
        pair_level = jnp.where(pair_xor >= size // 2, size, pair_level)
        size *= 2

    st_scr[...] = jnp.zeros_like(st_scr)

    def bwd_body(i, carry):
        n = n_chunks - 1 - i
        sl = pl.ds(pl.multiple_of(n * c, c), c)
        fb, gb = _hgrn_forget(fb_ref[sl, :], lb_b)
        pre = _prefix_sum_rows(gb)
        total = pre[c - 1:c, :]
        cs = total - pre + gb
        fb_scr[sl, :] = fb
        cs_scr[sl, :] = cs
        sb_scr[n] = st_scr[...]
        st_scr[...] = st_scr[...] * jnp.exp2(total) + _dot_tn(v_ref[sl, :], (1.0 - fb) * jnp.exp2(total - cs))
        return carry

    lax.fori_loop(0, n_chunks, bwd_body, 0, unroll=8)

    st_scr[...] = jnp.zeros_like(st_scr)

    def fwd_chunk(n):
        sl = pl.ds(pl.multiple_of(n * c, c), c)
        q = _silu(q_ref[sl, :])
        v = v_ref[sl, :]
        f, gf = _hgrn_forget(ff_ref[sl, :], lb_f)
        kf = 1.0 - f
        bs = _prefix_sum_rows(gf)
        fb = fb_scr[sl, :]
        kb = 1.0 - fb
        cs = cs_scr[sl, :]

        att = None
        size = 2
        while size <= c:
            x, y = _hgrn_level_operands(size, q, kf, kb, f, fb, bs, cs)
            att = jnp.where(pair_level == size, _dot_nt(x, y), 0.0 if att is None else att)
            size *= 2

        diag = jnp.sum(q * (kf + kb), axis=-1, keepdims=True)
        o = _dot(att, v) + diag * v
        inter = jnp.concatenate([q * jnp.exp2(bs), q * jnp.exp2(cs)], axis=1)
        states = jnp.concatenate([st_scr[...], sb_scr[n]], axis=1)
        o = o + _dot_nt(inter, states)
        last = bs[c - 1:c, :]
        st_scr[...] = st_scr[...] * jnp.exp2(last) + _dot_tn(v, kf * jnp.exp2(last - bs))
        o = o * lax.rsqrt(jnp.mean(o * o, axis=-1, keepdims=True) + EPS) * gain_ref[...]
        out_ref[sl, :] = (o * _silu(g_ref[sl, :])).astype(out_ref.dtype)

    chunks_per_block = min(HGRN_GATE_CHUNKS, n_chunks)
    gate_rows = chunks_per_block * c

    col_pieces = D_MODEL // GATE_PIECE_COLS
    assert chunks_per_block % col_pieces == 0
    chunks_per_piece = chunks_per_block // col_pieces

    def block_body(m, carry):
        rows = pl.ds(pl.multiple_of(m * gate_rows, gate_rows), gate_rows)
        for u in range(chunks_per_block):
            if u % chunks_per_piece == 0 and u // chunks_per_piece < col_pieces:
                c0 = (u // chunks_per_piece) * GATE_PIECE_COLS
                gate_ref[rows, c0:c0 + GATE_PIECE_COLS] = jnp.dot(
                    h_ref[rows, :], wmg_ref[:, c0:c0 + GATE_PIECE_COLS],
                    preferred_element_type=F32).astype(gate_ref.dtype)
            fwd_chunk(m * chunks_per_block + u)
        return carry

    lax.fori_loop(0, n_chunks // chunks_per_block, block_body, 0)


def _hgrn(z3, lb_logits, gain, h3, wmg, layer):
    b, t, _ = z3.shape
    depth = lb_logits.shape[0]
    w = HGRN_DK
    n_chunks = t // HGRN_CHUNK
    return pl.pallas_call(
        functools.partial(_hgrn_kernel, layer=layer),
        grid=(b, HGRN_HEADS),
        in_specs=[
            _zspec(t, w, OFF_HQ, 0),
            _zspec(t, w, OFF_HFF, 0),
            _zspec(t, w, OFF_HFB, 0),
            _zspec(t, w, OFF_HI, 0),
            _zspec(t, w, OFF_HG, 0),
            pl.BlockSpec((depth, 2, w), lambda i, h: (0, 0, h)),
            pl.BlockSpec((1, w), lambda i, h: (0, h)),
            pl.BlockSpec((None, t, D_MODEL), lambda i, h: (i, 0, 0)),
            pl.BlockSpec((None, D_MODEL, D_MODEL), lambda i, h: (h, 0, 0)),
        ],
        out_specs=[
            pl.BlockSpec((None, t, w), lambda i, h: (i, 0, h)),
            pl.BlockSpec((None, t, D_MODEL), lambda i, h: (i, 0, h)),
        ],
        out_shape=[
            jax.ShapeDtypeStruct((b, t, BRANCH_W), BF16),
            jax.ShapeDtypeStruct((b, t, N_BRANCH * D_MODEL), BF16),
        ],
        scratch_shapes=[
            pltpu.VMEM((t, w), F32),
            pltpu.VMEM((t, w), F32),
            pltpu.VMEM((n_chunks, w, w), F32),
            pltpu.VMEM((w, w), F32),
        ],
        compiler_params=_params("parallel", "parallel"),
        name="hgrn",
    )(z3, z3, z3, z3, z3, lb_logits, gain.reshape(1, -1), h3, wmg)


def _encoder(x, p, w):
    b, t, _ = x.shape
    depth = w['w_in'].shape[0]
    x2d = x.reshape(b * t, D_MODEL)
    for l in range(depth):
        z2d, h2d = _inproj(x2d, w['norm_mix'][l], w['w_in'][l])
        z3 = z2d.reshape(b, t, W_IN)
        br_a = _retention(z3, w['cos'], w['sin'], w['lgq'][l], w['lgv'][l])
        br_b = _nat(z3, w['nat_bias'][l])
        br_c = _rglru(z3, w['conv_w'][l], w['conv_b'][l], w['lru_wg'][l], w['lru_bg'][l], w['lam'][l])
        br_d, gates = _hgrn(z3, w['lb_logits'], w['hgrn_gain'][l], h2d.reshape(b, t, D_MODEL),
                            w['w_merge'][l], l)
        branches = [a.reshape(b * t, BRANCH_W) for a in (br_a, br_b, br_c, br_d)]
        x2d = _merge(x2d, branches, gates.reshape(b * t, N_BRANCH * D_MODEL), p[l].reshape(b * t, PLE_DIM),
                     w['w_branch'][l], w['w_out'][l], w['ple_norm'][l], w['w_ple_gate'][l],
                     w['w_ple_proj'][l], w['final_norm'], l == depth - 1)
    return x2d.reshape(b, t, D_MODEL)


def kernel(x_prompt, x_sample, p_prompt, p_sample, norm_mix, w_in, ret_decay_logit, nat_rpb, lru_conv_w,
           lru_conv_b, lru_wa, lru_ba, lru_wx, lru_bx, lru_lambda, hgrn_lb_logits, hgrn_norm, w_branch,
           w_merge, w_out, ple_norm, w_ple_gate, w_ple_proj, final_norm):
    depth = w_in.shape[0]
    t = x_prompt.shape[1]
    rows = t // GRID_W
    cos_tab, sin_tab = _rotary_tables(t)
    gate_w = [_rglru_gate_weights(lru_wa[l], lru_ba[l], lru_wx[l], lru_bx[l]) for l in range(depth)]
    weights = {
        'norm_mix': norm_mix,
        'w_in': jnp.concatenate([_pair_rotary_layout(w_in[..., OFF_RQ:OFF_RK]),
                                 _pair_rotary_layout(w_in[..., OFF_RK:OFF_RV]),
                                 w_in[..., OFF_RV:]], axis=-1).astype(BF16),
        'cos': cos_tab,
        'sin': sin_tab,
        'lgq': _pair_rotary_layout(jnp.repeat(ret_decay_logit.astype(F32), RET_QK, axis=-1)),
        'lgv': jnp.repeat(ret_decay_logit.astype(F32), RET_V, axis=-1),
        'nat_bias': [_nat_bias_tables(nat_rpb[l], rows) for l in range(depth)],
        'conv_w': lru_conv_w,
        'conv_b': lru_conv_b,
        'lru_wg': [g[0] for g in gate_w],
        'lru_bg': [g[1] for g in gate_w],
        'lam': lru_lambda,
        'lb_logits': hgrn_lb_logits,
        'hgrn_gain': hgrn_norm,
        'w_branch': w_branch.astype(BF16),
        'w_merge': w_merge.astype(BF16),
        'w_out': w_out.astype(BF16),
        'ple_norm': ple_norm,
        'w_ple_gate': w_ple_gate.astype(BF16),
        'w_ple_proj': w_ple_proj.astype(BF16),
        'final_norm': final_norm,
    }
    y_prompt = _encoder(x_prompt, p_prompt, weights)
    y_sample = _encoder(x_sample, p_sample, weights)
    return (y_prompt, y_sample)
```

```python
import functools

import numpy as np
import jax
import jax.numpy as jnp
from jax import lax
from jax.experimental import pallas as pl
from jax.experimental.pallas import tpu as pltpu

F32 = jnp.float32
BF16 = jnp.bfloat16

D_MODEL = 1024
PLE_DIM = 256
GRID_W = 64
N_BRANCH = 4
BRANCH_W = 512
RET_HEADS = 4
RET_QK = 64
RET_V = 128
ROPE_BASE = 10000.0
NAT_HEADS = 8
NAT_HD = 64
NAT_WIN_ROWS = 8
NAT_WIN_COLS = 16
LRU_BLOCKS = 8
LRU_BW = 64
LRU_CONV = 4
LRU_C = 8.0
HGRN_HEADS = 4
HGRN_DK = 128
EPS = 1e-6
LOG2E = float(np.log2(np.e))
W_IN = 7168

OFF_RQ, OFF_RK, OFF_RV, OFF_RG = 0, 256, 512, 1024
OFF_NQ, OFF_NK, OFF_NV, OFF_NG = 1536, 2048, 2560, 3072
OFF_LX, OFF_LG = 3584, 4096
OFF_HQ, OFF_HFF, OFF_HFB, OFF_HI, OFF_HG = 4608, 5120, 5632, 6144, 6656

LANES = 128
SUBLANES = 8
VMEM_LIMIT = 56 * 1024 * 1024

CHUNK = 128
HGRN_CHUNK = 128
HGRN_GATE_CHUNKS = 8
GATE_PIECE_COLS = 256


def _params(*sem):
    return pltpu.CompilerParams(dimension_semantics=sem, vmem_limit_bytes=VMEM_LIMIT)


def _dot(a, b):
    return jnp.dot(a.astype(BF16), b.astype(BF16), preferred_element_type=F32)


def _dot_nt(a, b):
    return lax.dot_general(a.astype(BF16), b.astype(BF16), (((1,), (1,)), ((), ())),
                           preferred_element_type=F32)


def _dot_tn(a, b):
    return lax.dot_general(a.astype(BF16), b.astype(BF16), (((0,), (0,)), ((), ())),
                           preferred_element_type=F32)


def _rms(x, g):
    return x * lax.rsqrt(jnp.mean(x * x, axis=-1, keepdims=True) + EPS) * g


def _sigmoid(x):
    return jax.nn.sigmoid(x)


def _silu(x):
    return x * jax.nn.sigmoid(x)


def _zspec(t, width, off, grid_pos):
    base = off // width
    if grid_pos == 0:
        return pl.BlockSpec((None, t, width), lambda b, h: (b, 0, base + h))
    return pl.BlockSpec((None, t, width), lambda h, b: (b, 0, base + h))


def _inproj_kernel(x_ref, g_ref, w_ref, z_ref, h_ref):
    @pl.when(pl.program_id(1) == 0)
    def _():
        h_ref[...] = _rms(x_ref[...], g_ref[...]).astype(h_ref.dtype)

    z_ref[...] = jnp.dot(h_ref[...], w_ref[...], preferred_element_type=F32)


def _inproj(x2d, g, w_bf16):
    m = x2d.shape[0]
    tm, tn = 2048, 1024
    return pl.pallas_call(
        _inproj_kernel,
        grid=(m // tm, W_IN // tn),
        in_specs=[
            pl.BlockSpec((tm, D_MODEL), lambda i, j: (i, 0)),
            pl.BlockSpec((1, D_MODEL), lambda i, j: (0, 0)),
            pl.BlockSpec((D_MODEL, tn), lambda i, j: (0, j)),
        ],
        out_specs=[
            pl.BlockSpec((tm, tn), lambda i, j: (i, j)),
            pl.BlockSpec((tm, D_MODEL), lambda i, j: (i, 0)),
        ],
        out_shape=[
            jax.ShapeDtypeStruct((m, W_IN), F32),
            jax.ShapeDtypeStruct((m, D_MODEL), BF16),
        ],
        compiler_params=_params("parallel", "arbitrary"),
        name="inproj",
    )(x2d, g.reshape(1, D_MODEL), w_bf16)


def _merge_kernel(x_ref, ba_ref, bb_ref, bc_ref, bd_ref, gates_ref, p_ref, wbr_ref, wo_ref,
                  gple_ref, wpg_ref, wpp_ref, gfin_ref, out_ref, *, final):
    x = x_ref[...]
    merged = None
    for j, b_ref in enumerate((ba_ref, bb_ref, bc_ref, bd_ref)):
        gate = _sigmoid(gates_ref[:, j * D_MODEL:(j + 1) * D_MODEL].astype(F32))
        term = gate * jnp.dot(b_ref[...], wbr_ref[j], preferred_element_type=F32)
        merged = term if merged is None else merged + term
    x1 = x + _dot(merged, wo_ref[...])
    gate2 = _sigmoid(_dot(_rms(x1, gple_ref[...]), wpg_ref[...]))
    x2 = x1 + gate2 * _dot(p_ref[...], wpp_ref[...])
    if final:
        x2 = _rms(x2, gfin_ref[...])
    out_ref[...] = x2


def _merge(x2d, branches, gates, p2d, wbr, wo, gple, wpg, wpp, gfin, final):
    m = x2d.shape[0]
    tm = 512
    row = lambda i: (i, 0)
    const2 = lambda i: (0, 0)
    const3 = lambda i: (0, 0, 0)
    once = pl.Buffered(1)
    vec = pl.BlockSpec((1, D_MODEL), const2)
    return pl.pallas_call(
        functools.partial(_merge_kernel, final=final),
        grid=(m // tm,),
        in_specs=[
            pl.BlockSpec((tm, D_MODEL), row),
            pl.BlockSpec((tm, BRANCH_W), row),
            pl.BlockSpec((tm, BRANCH_W), row),
            pl.BlockSpec((tm, BRANCH_W), row),
            pl.BlockSpec((tm, BRANCH_W), row),
            pl.BlockSpec((tm, N_BRANCH * D_MODEL), row),
            pl.BlockSpec((tm, PLE_DIM), row),
            pl.BlockSpec((N_BRANCH, BRANCH_W, D_MODEL), const3, pipeline_mode=once),
            pl.BlockSpec((D_MODEL, D_MODEL), const2, pipeline_mode=once),
            vec,
            pl.BlockSpec((D_MODEL, D_MODEL), const2, pipeline_mode=once),
            pl.BlockSpec((PLE_DIM, D_MODEL), const2, pipeline_mode=once),
            vec,
        ],
        out_specs=pl.BlockSpec((tm, D_MODEL), row),
        out_shape=jax.ShapeDtypeStruct((m, D_MODEL), F32),
        compiler_params=_params("parallel"),
        name="merge",
    )(x2d, *branches, gates, p2d, wbr, wo, gple.reshape(1, -1), wpg, wpp, gfin.reshape(1, -1))


def _retention_kernel(q_ref, k_ref, v_ref, g_ref, cos_ref, sin_ref, lgq_ref, lgv_ref, out_ref,
                      kr_scr, sb_scr, sf_scr):
    t = q_ref.shape[0]
    c = CHUNK
    n_chunks = t // c
    hd = RET_QK

    lane = lax.broadcasted_iota(jnp.int32, (1, 2 * hd), 1)
    head0_q = (lane // (hd // 2)) % 2 == 0
    lane_v = lax.broadcasted_iota(jnp.int32, (1, 2 * RET_V), 1)
    head0_v = lane_v < RET_V

    def rotary(x, cos, sin):
        return x * cos + pltpu.roll(x, hd, 1) * sin

    lg_f = -jnp.log(1.0 + jnp.exp(-lgq_ref[0:1, :]))
    lg_b = -jnp.log(1.0 + jnp.exp(-lgq_ref[1:2, :]))
    lgv_f = -jnp.log(1.0 + jnp.exp(-lgv_ref[0:1, :]))
    lgv_b = -jnp.log(1.0 + jnp.exp(-lgv_ref[1:2, :]))
    tcol = lax.broadcasted_iota(jnp.int32, (c, 1), 0).astype(F32)
    head_f = jnp.exp((tcol + 1.0) * lg_f)
    head_b = jnp.exp((c - tcol) * lg_b)
    tail_f = jnp.exp((c - 1.0 - tcol) * lg_f)
    tail_b = jnp.exp(tcol * lg_b)
    dec_f = jnp.exp(c * lgv_f)
    dec_b = jnp.exp(c * lgv_b)
    rowk = lax.broadcasted_iota(jnp.int32, (2 * hd, 2 * RET_V), 0)
    colv = lax.broadcasted_iota(jnp.int32, (2 * hd, 2 * RET_V), 1)
    blockdiag = ((rowk // (hd // 2)) % 2) == (colv // RET_V)

    diff = (lax.broadcasted_iota(jnp.int32, (c, c), 0) - lax.broadcasted_iota(jnp.int32, (c, c), 1)).astype(F32)

    def decay_mask(lf, lb):
        fwd = jnp.exp(jnp.maximum(diff, 0.0) * lf)
        bwd = jnp.exp(jnp.maximum(-diff, 0.0) * lb)
        return jnp.where(diff > 0, fwd, jnp.where(diff < 0, bwd, 2.0))

    dmask = jnp.concatenate([decay_mask(lgv_f[:, :c], lgv_b[:, :c]),
                             decay_mask(lgv_f[:, RET_V:RET_V + c], lgv_b[:, RET_V:RET_V + c])], axis=1)

    sf_scr[...] = jnp.zeros_like(sf_scr)

    def bwd_body(i, carry):
        n = n_chunks - 1 - i
        sl = pl.ds(pl.multiple_of(n * c, c), c)
        sb_scr[n] = sf_scr[...]
        kn = rotary(k_ref[sl, :], cos_ref[sl, :], sin_ref[sl, :]) * (hd ** -0.5)
        kr_scr[sl, :] = kn
        loc = _dot_tn(kn * tail_b, v_ref[sl, :])
        sf_scr[...] = sf_scr[...] * dec_b + jnp.where(blockdiag, loc, 0.0)
        return carry

    lax.fori_loop(0, n_chunks, bwd_body, 0, unroll=16)

    sf_scr[...] = jnp.zeros_like(sf_scr)

    def fwd_body(n, carry):
        sl = pl.ds(pl.multiple_of(n * c, c), c)
        qn = rotary(q_ref[sl, :], cos_ref[sl, :], sin_ref[sl, :])
        kn = kr_scr[sl, :]
        vn = v_ref[sl, :]
        kstack = jnp.concatenate([jnp.where(head0_q, kn, 0.0), jnp.where(head0_q, 0.0, kn)], axis=0)
        scores = _dot_nt(qn, kstack) * dmask
        vstack = jnp.concatenate([jnp.where(head0_v, vn, 0.0), jnp.where(head0_v, 0.0, vn)], axis=0)
        lhs = jnp.concatenate([scores, qn * head_f, qn * head_b], axis=1)
        rhs = jnp.concatenate([vstack, sf_scr[...], sb_scr[n]], axis=0)
        o = _dot(lhs, rhs)
        loc = _dot_tn(kn * tail_f, vn)
        sf_scr[...] = sf_scr[...] * dec_f + jnp.where(blockdiag, loc, 0.0)
        o0 = o[:, :RET_V]
        o1 = o[:, RET_V:]
        o0 = o0 * lax.rsqrt(jnp.mean(o0 * o0, axis=-1, keepdims=True) + EPS)
        o1 = o1 * lax.rsqrt(jnp.mean(o1 * o1, axis=-1, keepdims=True) + EPS)
        on = jnp.concatenate([o0, o1], axis=1)
        out_ref[sl, :] = (on * _silu(g_ref[sl, :])).astype(out_ref.dtype)
        return carry

    lax.fori_loop(0, n_chunks, fwd_body, 0, unroll=16)


def _retention(z3, cos_tab, sin_tab, lgq, lgv):
    b, t, _ = z3.shape
    n_chunks = t // CHUNK
    return pl.pallas_call(
        _retention_kernel,
        grid=(b, RET_HEADS // 2),
        in_specs=[
            _zspec(t, 2 * RET_QK, OFF_RQ, 0),
            _zspec(t, 2 * RET_QK, OFF_RK, 0),
            _zspec(t, 2 * RET_V, OFF_RV, 0),
            _zspec(t, 2 * RET_V, OFF_RG, 0),
            pl.BlockSpec((t, 2 * RET_QK), lambda i, h: (0, 0)),
            pl.BlockSpec((t, 2 * RET_QK), lambda i, h: (0, 0)),
            pl.BlockSpec((2, 2 * RET_QK), lambda i, h: (0, h)),
            pl.BlockSpec((2, 2 * RET_V), lambda i, h: (0, h)),
        ],
        out_specs=pl.BlockSpec((None, t, 2 * RET_V), lambda i, h: (i, 0, h)),
        out_shape=jax.ShapeDtypeStruct((b, t, BRANCH_W), BF16),
        scratch_shapes=[
            pltpu.VMEM((t, 2 * RET_QK), F32),
            pltpu.VMEM((n_chunks, 2 * RET_QK, 2 * RET_V), F32),
            pltpu.VMEM((2 * RET_QK, 2 * RET_V), F32),
        ],
        compiler_params=_params("parallel", "parallel"),
        name="retention",
    )(z3, z3, z3, z3, cos_tab, sin_tab, lgq, lgv)


def _rotary_tables(t):
    half = RET_QK // 2
    inv = ROPE_BASE ** (-jnp.arange(half, dtype=F32) / half)
    ang = jnp.arange(t, dtype=F32)[:, None] * inv[None, :]
    cos = jnp.cos(ang)
    sin = jnp.sin(ang)
    cos_tab = jnp.tile(cos, (1, 4))
    sin_tab = jnp.concatenate([-sin, -sin, sin, sin], axis=1)
    return cos_tab, sin_tab


def _pair_rotary_layout(a):
    lead = a.shape[:-1]
    a = a.reshape(lead + (RET_HEADS // 2, 2, 2, RET_QK // 2))
    return jnp.swapaxes(a, -3, -2).reshape(lead + (RET_HEADS * RET_QK,))


NAT_QROWS = 8
NAT_KROWS = 16
NAT_KCOLS = 2 * NAT_WIN_COLS


def _nat_kstart(j):
    return int(np.clip(j * NAT_WIN_COLS - NAT_WIN_COLS // 2, 0, GRID_W - NAT_KCOLS))


def _nat_key_row_base(g, rows):
    return int(np.clip(g * NAT_QROWS - NAT_WIN_ROWS // 2, 0, rows - NAT_KROWS))


def _edge_class(i, n):
    return 0 if i == 0 else (2 if i == n - 1 else 1)


def _nat_bias_tables(rpb, rows):
    n_g = rows // NAT_QROWS
    n_cb = GRID_W // NAT_WIN_COLS

    def row_geometry(g):
        dr = np.zeros((NAT_QROWS, NAT_KROWS), np.int32)
        ok = np.zeros((NAT_QROWS, NAT_KROWS), bool)
        kb = _nat_key_row_base(g, rows)
        for rr in range(NAT_QROWS):
            r = g * NAT_QROWS + rr
            rs = int(np.clip(r - NAT_WIN_ROWS // 2, 0, rows - NAT_WIN_ROWS))
            for kr in range(NAT_KROWS):
                ka = kb + kr
                ok[rr, kr] = rs <= ka < rs + NAT_WIN_ROWS
                dr[rr, kr] = int(np.clip(ka - r + NAT_WIN_ROWS - 1, 0, 2 * NAT_WIN_ROWS - 2))
        return dr, ok

    def col_geometry(j):
        dc = np.zeros((NAT_WIN_COLS, NAT_KCOLS), np.int32)
        ok = np.zeros((NAT_WIN_COLS, NAT_KCOLS), bool)
        ks = _nat_kstart(j)
        for qq in range(NAT_WIN_COLS):
            qc = j * NAT_WIN_COLS + qq
            ws = int(np.clip(qc - NAT_WIN_COLS // 2, 0, GRID_W - NAT_WIN_COLS))
            for kc in range(NAT_KCOLS):
                ka = ks + kc
                ok[qq, kc] = ws <= ka < ws + NAT_WIN_COLS
                dc[qq, kc] = int(np.clip(ka - qc + NAT_WIN_COLS - 1, 0, 2 * NAT_WIN_COLS - 2))
        return dc, ok

    def by_class(geometry, n):
        reps = {}
        for i in range(n):
            dx, ok = geometry(i)
            cls = _edge_class(i, n)
            if cls in reps:
                assert (np.where(ok, dx, -1) == np.where(reps[cls][1], reps[cls][0], -1)).all()
            else:
                reps[cls] = (dx, ok)
        filled = [reps.get(cls, reps[0]) for cls in range(3)]
        return np.stack([f[0] for f in filled]), np.stack([f[1] for f in filled])

    dr, row_ok = by_class(row_geometry, n_g)
    dc, col_ok = by_class(col_geometry, n_cb)
    col_sel = (dc[..., None] == np.arange(2 * NAT_WIN_COLS - 1)).astype(np.float32)
    blocks = jnp.einsum('hab,jqcb->hjaqc', rpb.astype(F32), col_sel, precision=lax.Precision.HIGHEST)
    blocks = jnp.where(col_ok[None, :, None], blocks * LOG2E, -jnp.inf)
    nq = NAT_QROWS * NAT_WIN_COLS
    nk = NAT_KROWS * NAT_KCOLS

    def tile_kernel(blocks_ref, out_ref):
        masked = jnp.full((NAT_WIN_COLS, NAT_KCOLS), -jnp.inf, F32)
        for g in range(3):
            for j in range(3):
                for rr in range(NAT_QROWS):
                    pieces = [blocks_ref[j, int(dr[g, rr, kr])] if row_ok[g, rr, kr] else masked
                              for kr in range(NAT_KROWS)]
                    out_ref[g, j, rr * NAT_WIN_COLS:(rr + 1) * NAT_WIN_COLS, :] = jnp.concatenate(pieces, axis=1)

    return pl.pallas_call(
        tile_kernel,
        grid=(NAT_HEADS,),
        in_specs=[pl.BlockSpec((None,) + blocks.shape[1:], lambda h: (h, 0, 0, 0, 0))],
        out_specs=pl.BlockSpec((None, None, 3, 3, nq, nk), lambda h: (h // 2, h % 2, 0, 0, 0, 0)),
        out_shape=jax.ShapeDtypeStruct((NAT_HEADS // 2, 2, 3, 3, nq, nk), F32),
        compiler_params=_params("parallel"),
        name="nat_bias",
    )(blocks)


def _nat_kernel(q_ref, k_ref, v_ref, g_ref, bias_ref, out_ref):
    t = q_ref.shape[0]
    rows = t // GRID_W
    n_g = rows // NAT_QROWS
    n_cb = GRID_W // NAT_WIN_COLS
    nq = NAT_QROWS * NAT_WIN_COLS
    lane = lax.broadcasted_iota(jnp.int32, (1, 2 * NAT_HD), 1)
    head0 = lane < NAT_HD

    def group_body(g, carry):
        kb = jnp.clip(g * NAT_QROWS - NAT_WIN_ROWS // 2, 0, rows - NAT_KROWS)
        g_cls = jnp.where(g == 0, 0, jnp.where(g == n_g - 1, 2, 1))
        for j in range(n_cb):
            ks = _nat_kstart(j)

            def qrows(ref, rr):
                start = pl.multiple_of((g * NAT_QROWS + rr) * GRID_W + j * NAT_WIN_COLS, NAT_WIN_COLS)
                return ref[pl.ds(start, NAT_WIN_COLS), :]

            def krows(ref, kr):
                start = pl.multiple_of((kb + kr) * GRID_W + ks, SUBLANES)
                return ref[pl.ds(start, NAT_KCOLS), :]

            qs = jnp.concatenate([qrows(q_ref, rr) for rr in range(NAT_QROWS)], axis=0) * (NAT_HD ** -0.5 * LOG2E)
            q2 = jnp.concatenate([jnp.where(head0, qs, 0.0), jnp.where(head0, 0.0, qs)], axis=0)
            kblk = jnp.concatenate([krows(k_ref, kr) for kr in range(NAT_KROWS)], axis=0)
            vblk = jnp.concatenate([krows(v_ref, kr) for kr in range(NAT_KROWS)], axis=0)
            s = _dot_nt(q2, kblk)
            j_cls = _edge_class(j, n_cb)
            s = s + jnp.concatenate([bias_ref[0, g_cls, j_cls], bias_ref[1, g_cls, j_cls]], axis=0)
            m = jnp.max(s, axis=-1, keepdims=True)
            p = jnp.exp2(s - m)
            vext = jnp.concatenate([vblk, jnp.ones_like(vblk)], axis=1)
            o_ext = _dot(p, vext)
            o2 = o_ext[:, :2 * NAT_HD] / o_ext[:, 2 * NAT_HD:]
            o = jnp.where(head0, o2[:nq], o2[nq:])
            gs = jnp.concatenate([qrows(g_ref, rr) for rr in range(NAT_QROWS)], axis=0)
            res = (o * _silu(gs)).astype(out_ref.dtype)
            for rr in range(NAT_QROWS):
                start = pl.multiple_of((g * NAT_QROWS + rr) * GRID_W + j * NAT_WIN_COLS, NAT_WIN_COLS)
                out_ref[pl.ds(start, NAT_WIN_COLS), :] = res[rr * NAT_WIN_COLS:(rr + 1) * NAT_WIN_COLS]
        return carry

    lax.fori_loop(0, n_g, group_body, 0, unroll=4)


def _nat(z3, bias_tab):
    b, t, _ = z3.shape
    n_hp = NAT_HEADS // 2
    w = 2 * NAT_HD
    return pl.pallas_call(
        _nat_kernel,
        grid=(n_hp, b),
        in_specs=[
            _zspec(t, w, OFF_NQ, 1),
            _zspec(t, w, OFF_NK, 1),
            _zspec(t, w, OFF_NV, 1),
            _zspec(t, w, OFF_NG, 1),
            pl.BlockSpec((None,) + bias_tab.shape[1:], lambda h, i: (h, 0, 0, 0, 0, 0)),
        ],
        out_specs=pl.BlockSpec((None, t, w), lambda h, i: (i, 0, h)),
        out_shape=jax.ShapeDtypeStruct((b, t, BRANCH_W), BF16),
        compiler_params=_params("parallel", "parallel"),
        name="nat",
    )(z3, z3, z3, z3, bias_tab)


def _rglru_kernel(x_ref, g_ref, cw_ref, cb_ref, wg_ref, bg_ref, lam_ref, out_ref,
                  xp_scr, af_scr, bf_scr, ab_scr, bb_scr, hf_scr, pf_scr, hb_scr, pb_scr):
    t = x_ref.shape[0]
    n_seg = SUBLANES
    tc = t // n_seg
    pad = SUBLANES
    w = LANES

    xp_scr[0:pad, :] = jnp.zeros((pad, w), F32)
    xp_scr[pad + t:pad + t + pad, :] = jnp.zeros((pad, w), F32)
    xp_scr[pad:pad + t, :] = x_ref[...]

    lam = lam_ref[...]
    neg = -lam
    softplus = jnp.maximum(neg, 0.0) + jnp.log(1.0 + jnp.exp(-jnp.abs(neg)))
    log2_a_per_r = softplus * float(-LRU_C * np.log2(np.e))
    cw = cw_ref[...]
    cb = cb_ref[...]
    bg = bg_ref[...]

    def gate_body(n, carry):
        t0 = pl.multiple_of(n * tc, tc)
        xx = xp_scr[pl.ds(t0, tc + 2 * pad), :]
        total = tc + 2 * pad
        xc = cb
        for j in range(LRU_CONV):
            shift = LRU_CONV // 2 - j
            xs = xx if shift == 0 else pltpu.roll(xx, shift % total, 0)
            xc = xc + xs[pad:pad + tc] * cw[j:j + 1, :]
        gates = _dot(xc, wg_ref[...]) + bg
        seg_rows = pl.ds(n, tc, stride=n_seg)
        for d, (a_scr, b_scr) in enumerate(((af_scr, bf_scr), (ab_scr, bb_scr))):
            r = _sigmoid(gates[:, (2 * d) * w:(2 * d + 1) * w])
            i = _sigmoid(gates[:, (2 * d + 1) * w:(2 * d + 2) * w])
            a = jnp.exp2(r * log2_a_per_r[d:d + 1, :])
            a_scr[seg_rows, :] = a
            gap = 1.0 - a * a
            b_scr[seg_rows, :] = gap * lax.rsqrt(jnp.maximum(gap, 1e-30)) * (i * xc)
        return carry

    lax.fori_loop(0, n_seg, gate_body, 0, unroll=4)

    def scan_body(i, carry):
        hf, pf, hb, pb = carry
        sl = pl.ds(pl.multiple_of(i * n_seg, n_seg), n_seg)
        a = af_scr[sl, :]
        hf = a * hf + bf_scr[sl, :]
        pf = a * pf
        hf_scr[sl, :] = hf
        pf_scr[sl, :] = pf
        slb = pl.ds(pl.multiple_of((tc - 1 - i) * n_seg, n_seg), n_seg)
        a = ab_scr[slb, :]
        hb = a * hb + bb_scr[slb, :]
        pb = a * pb
        hb_scr[slb, :] = hb
        pb_scr[slb, :] = pb
        return hf, pf, hb, pb

    zero = jnp.zeros((n_seg, w), F32)
    one = jnp.ones((n_seg, w), F32)
    hf, pf, hb, pb = lax.fori_loop(0, tc, scan_body, (zero, one, zero, one), unroll=8)

    row = lax.broadcasted_iota(jnp.int32, (n_seg, w), 0)
    for s in (1, 2, 4):
        keep = row >= s
        hf = jnp.where(keep, pf * pltpu.roll(hf, s, 0) + hf, hf)
        pf = jnp.where(keep, pf * pltpu.roll(pf, s, 0), pf)
        keep = row < n_seg - s
        hb = jnp.where(keep, pb * pltpu.roll(hb, n_seg - s, 0) + hb, hb)
        pb = jnp.where(keep, pb * pltpu.roll(pb, n_seg - s, 0), pb)
    carry_f = jnp.where(row >= 1, pltpu.roll(hf, 1, 0), 0.0)
    carry_b = jnp.where(row < n_seg - 1, pltpu.roll(hb, n_seg - 1, 0), 0.0)

    def fix_body(i, carry):
        sl = pl.ds(pl.multiple_of(i * n_seg, n_seg), n_seg)
        af_scr[sl, :] = (hf_scr[sl, :] + pf_scr[sl, :] * carry_f) + (hb_scr[sl, :] + pb_scr[sl, :] * carry_b)
        return carry

    lax.fori_loop(0, tc, fix_body, 0, unroll=8)

    def out_body(n, carry):
        h = af_scr[pl.ds(n, tc, stride=n_seg), :]
        sl = pl.ds(pl.multiple_of(n * tc, tc), tc)
        out_ref[sl, :] = (h * _silu(g_ref[sl, :])).astype(out_ref.dtype)
        return carry

    lax.fori_loop(0, n_seg, out_body, 0)


def _rglru(z3, conv_w, conv_b, wg, bg, lam):
    b, t, _ = z3.shape
    n_cb = BRANCH_W // LANES
    return pl.pallas_call(
        _rglru_kernel,
        grid=(b, n_cb),
        in_specs=[
            _zspec(t, LANES, OFF_LX, 0),
            _zspec(t, LANES, OFF_LG, 0),
            pl.BlockSpec((LRU_CONV, LANES), lambda i, h: (0, h)),
            pl.BlockSpec((1, LANES), lambda i, h: (0, h)),
            pl.BlockSpec((None, LANES, 4 * LANES), lambda i, h: (h, 0, 0)),
            pl.BlockSpec((None, 1, 4 * LANES), lambda i, h: (h, 0, 0)),
            pl.BlockSpec((2, LANES), lambda i, h: (0, h)),
        ],
        out_specs=pl.BlockSpec((None, t, LANES), lambda i, h: (i, 0, h)),
        out_shape=jax.ShapeDtypeStruct((b, t, BRANCH_W), BF16),
        scratch_shapes=[pltpu.VMEM((t + 2 * SUBLANES, LANES), F32)] + [pltpu.VMEM((t, LANES), F32)] * 8,
        compiler_params=_params("parallel", "parallel"),
        name="rglru",
    )(z3, z3, conv_w, conv_b.reshape(1, BRANCH_W), wg, bg, lam)


def _rglru_gate_weights(wa, ba, wx, bx):
    n_cb = BRANCH_W // LANES
    per = LANES // LRU_BW

    def blockdiag(w):
        w = w.reshape(n_cb, per, LRU_BW, LRU_BW)
        eye = jnp.eye(per, dtype=w.dtype)
        return jnp.einsum('cpjk,pq->cpjqk', w, eye).reshape(n_cb, LANES, LANES)

    wg = jnp.concatenate([blockdiag(wa[0]), blockdiag(wx[0]), blockdiag(wa[1]), blockdiag(wx[1])], axis=-1)
    bg = jnp.concatenate([ba[0].reshape(n_cb, 1, LANES), bx[0].reshape(n_cb, 1, LANES),
                          ba[1].reshape(n_cb, 1, LANES), bx[1].reshape(n_cb, 1, LANES)], axis=-1)
    return wg.astype(BF16), bg.astype(F32)


def _prefix_sum_rows(x):
    c, w = x.shape
    x3 = x.reshape(c // SUBLANES, SUBLANES, w)
    row = lax.broadcasted_iota(jnp.int32, (1, SUBLANES, w), 1)
    s = 1
    while s < SUBLANES:
        x3 = x3 + jnp.where(row >= s, pltpu.roll(x3, s, 1), 0.0)
        s *= 2
    tile_tot = jnp.broadcast_to(x3[:, SUBLANES - 1:, :], x3.shape).reshape(c, w)
    x = x3.reshape(c, w)
    while s < c:
        shifted = jnp.concatenate([jnp.zeros((s, w), x.dtype), tile_tot[:c - s]], axis=0)
        x = x + shifted
        tile_tot = tile_tot + shifted
        s *= 2
    return x


def _hgrn_forget(zf, lb):
    f = lb + (1.0 - lb) * _sigmoid(zf)
    return f, jnp.log2(f)


def _hgrn_tile_decays(size, f3, fb3, bs3, cs3, pos):
    half = size // 2
    if size == 2:
        return jnp.where(pos == 1, f3, 1.0), jnp.where(pos == 0, fb3, 1.0)
    if size == 4:
        f_prev, f_next = pltpu.roll(f3, 1, 1), pltpu.roll(f3, SUBLANES - 1, 1)
        fb_prev, fb_next = pltpu.roll(fb3, 1, 1), pltpu.roll(fb3, SUBLANES - 1, 1)
        e_f = jnp.where(pos == 0, f_next, jnp.where(pos == 1, 1.0, jnp.where(pos == 2, f3, f_prev * f3)))
        e_b = jnp.where(pos == 0, fb3 * fb_next, jnp.where(pos == 1, fb3, jnp.where(pos == 2, 1.0, fb_prev)))
        return e_f, e_b
    sign = jnp.where(pos >= half, 1.0, -1.0)
    e_f = jnp.exp2((bs3 - bs3[:, half - 1:half, :]) * sign)
    e_b = jnp.exp2((cs3[:, half:half + 1, :] - cs3) * sign)
    return e_f, e_b


def _hgrn_level_operands(size, q, kf, kb, f, fb, bs, cs):
    c, w = q.shape
    half = size // 2
    if size <= SUBLANES:
        tiled = lambda a: a.reshape(c // SUBLANES, SUBLANES, w)
        pos = lax.broadcasted_iota(jnp.int32, (1, SUBLANES, w), 1) % size
        upper = pos >= half
        e_f, e_b = _hgrn_tile_decays(size, tiled(f), tiled(fb), tiled(bs), tiled(cs), pos)
        z_f = e_f * jnp.where(upper, tiled(q), tiled(kf))
        z_b = e_b * jnp.where(upper, tiled(kb), tiled(q))
        x = jnp.concatenate([jnp.where(upper, z_f, 0.0), jnp.where(upper, 0.0, z_b)], axis=2)
        y = jnp.concatenate([z_f, z_b], axis=2)
        return x.reshape(c, 2 * w), y.reshape(c, 2 * w)
    xs, ys = [], []
    zero = jnp.zeros((half, w), F32)
    for i in range(c // half):
        rows = slice(i * half, (i + 1) * half)
        if i % 2 == 0:
            ref_f = bs[(i + 1) * half - 1:(i + 1) * half]
            ref_b = cs[(i + 1) * half:(i + 1) * half + 1]
            z_f = kf[rows] * jnp.exp2(ref_f - bs[rows])
            z_b = q[rows] * jnp.exp2(cs[rows] - ref_b)
            xs.append(jnp.concatenate([zero, z_b], axis=1))
        else:
            ref_f = bs[i * half - 1:i * half]
            ref_b = cs[i * half:i * half + 1]
            z_f = q[rows] * jnp.exp2(bs[rows] - ref_f)
            z_b = kb[rows] * jnp.exp2(ref_b - cs[rows])
            xs.append(jnp.concatenate([z_f, zero], axis=1))
        ys.append(jnp.concatenate([z_f, z_b], axis=1))
    return jnp.concatenate(xs, axis=0), jnp.concatenate(ys, axis=0)


def _hgrn_kernel(q_ref, ff_ref, fb_ref, v_ref, g_ref, lbl_ref, gain_ref, h_ref, wmg_ref, out_ref, gate_ref,
                 fb_scr, cs_scr, sb_scr, st_scr, *, layer):
    t = q_ref.shape[0]
    c = HGRN_CHUNK
    n_chunks = t // c
    w = HGRN_DK

    logits = lbl_ref[...]
    mx = jnp.max(logits, axis=0)
    ex = jnp.exp(logits - mx[None])
    tot = jnp.sum(ex, axis=0)
    lb = jnp.zeros_like(tot)
    for i in range(1, layer + 1):
        lb = lb + ex[i] / tot
    lb_f = lb[0:1, :]
    lb_b = lb[1:2, :]

    pair_xor = lax.broadcasted_iota(jnp.int32, (c, c), 0) ^ lax.broadcasted_iota(jnp.int32, (c, c), 1)
    pair_level = jnp.zeros((c, c), jnp.int32)
    size = 2
    while size <= c:
        pair_level = jnp.where(pair_xor >= size // 2, size, pair_level)
        size *= 2

    st_scr[...] = jnp.zeros_like(st_scr)

    def bwd_body(i, carry):
        n = n_chunks - 1 - i
        sl = pl.ds(pl.multiple_of(n * c, c), c)
        fb, gb = _hgrn_forget(fb_ref[sl, :], lb_b)
        pre = _prefix_sum_rows(gb)
        total = pre[c - 1:c, :]
        cs = total - pre + gb
        fb_scr[sl, :] = fb
        cs_scr[sl, :] = cs
        sb_scr[n] = st_scr[...]
        st_scr[...] = st_scr[...] * jnp.exp2(total) + _dot_tn(v_ref[sl, :], (1.0 - fb) * jnp.exp2(total - cs))
        return carry

    lax.fori_loop(0, n_chunks, bwd_body, 0, unroll=16)

    st_scr[...] = jnp.zeros_like(st_scr)

    def fwd_chunk(n):
        sl = pl.ds(pl.multiple_of(n * c, c), c)
        q = _silu(q_ref[sl, :])
        v = v_ref[sl, :]
        f, gf = _hgrn_forget(ff_ref[sl, :], lb_f)
        kf = 1.0 - f
        bs = _prefix_sum_rows(gf)
        fb = fb_scr[sl, :]
        kb = 1.0 - fb
        cs = cs_scr[sl, :]

        att = None
        size = 2
        while size <= c:
            x, y = _hgrn_level_operands(size, q, kf, kb, f, fb, bs, cs)
            att = jnp.where(pair_level == size, _dot_nt(x, y), 0.0 if att is None else att)
            size *= 2

        diag = jnp.sum(q * (kf + kb), axis=-1, keepdims=True)
        o = _dot(att, v) + diag * v
        inter = jnp.concatenate([q * jnp.exp2(bs), q * jnp.exp2(cs)], axis=1)
        states = jnp.concatenate([st_scr[...], sb_scr[n]], axis=1)
        o = o + _dot_nt(inter, states)
        last = bs[c - 1:c, :]
        st_scr[...] = st_scr[...] * jnp.exp2(last) + _dot_tn(v, kf * jnp.exp2(last - bs))
        o = o * lax.rsqrt(jnp.mean(o * o, axis=-1, keepdims=True) + EPS) * gain_ref[...]
        out_ref[sl, :] = (o * _silu(g_ref[sl, :])).astype(out_ref.dtype)

    chunks_per_block = min(HGRN_GATE_CHUNKS, n_chunks)
    gate_rows = chunks_per_block * c

    col_pieces = D_MODEL // GATE_PIECE_COLS
    assert chunks_per_block % col_pieces == 0
    chunks_per_piece = chunks_per_block // col_pieces

    def block_body(m, carry):
        rows = pl.ds(pl.multiple_of(m * gate_rows, gate_rows), gate_rows)
        for u in range(chunks_per_block):
            if u % chunks_per_piece == 0 and u // chunks_per_piece < col_pieces:
                c0 = (u // chunks_per_piece) * GATE_PIECE_COLS
                gate_ref[rows, c0:c0 + GATE_PIECE_COLS] = jnp.dot(
                    h_ref[rows, :], wmg_ref[:, c0:c0 + GATE_PIECE_COLS],
                    preferred_element_type=F32).astype(gate_ref.dtype)
            fwd_chunk(m * chunks_per_block + u)
        return carry

    lax.fori_loop(0, n_chunks // chunks_per_block, block_body, 0)


def _hgrn(z3, lb_logits, gain, h3, wmg, layer):
    b, t, _ = z3.shape
    depth = lb_logits.shape[0]
    w = HGRN_DK
    n_chunks = t // HGRN_CHUNK
    return pl.pallas_call(
        functools.partial(_hgrn_kernel, layer=layer),
        grid=(b, HGRN_HEADS),
        in_specs=[
            _zspec(t, w, OFF_HQ, 0),
            _zspec(t, w, OFF_HFF, 0),
            _zspec(t, w, OFF_HFB, 0),
            _zspec(t, w, OFF_HI, 0),
            _zspec(t, w, OFF_HG, 0),
            pl.BlockSpec((depth, 2, w), lambda i, h: (0, 0, h)),
            pl.BlockSpec((1, w), lambda i, h: (0, h)),
            pl.BlockSpec((None, t, D_MODEL), lambda i, h: (i, 0, 0)),
            pl.BlockSpec((None, D_MODEL, D_MODEL), lambda i, h: (h, 0, 0)),
        ],
        out_specs=[
            pl.BlockSpec((None, t, w), lambda i, h: (i, 0, h)),
            pl.BlockSpec((None, t, D_MODEL), lambda i, h: (i, 0, h)),
        ],
        out_shape=[
            jax.ShapeDtypeStruct((b, t, BRANCH_W), BF16),
            jax.ShapeDtypeStruct((b, t, N_BRANCH * D_MODEL), BF16),
        ],
        scratch_shapes=[
            pltpu.VMEM((t, w), F32),
            pltpu.VMEM((t, w), F32),
            pltpu.VMEM((n_chunks, w, w), F32),
            pltpu.VMEM((w, w), F32),
        ],
        compiler_params=_params("parallel", "parallel"),
        name="hgrn",
    )(z3, z3, z3, z3, z3, lb_logits, gain.reshape(1, -1), h3, wmg)


def _encoder(x, p, w):
    b, t, _ = x.shape
    depth = w['w_in'].shape[0]
    x2d = x.reshape(b * t, D_MODEL)
    for l in range(depth):
        z2d, h2d = _inproj(x2d, w['norm_mix'][l], w['w_in'][l])
        z3 = z2d.reshape(b, t, W_IN)
        br_a = _retention(z3, w['cos'], w['sin'], w['lgq'][l], w['lgv'][l])
        br_b = _nat(z3, w['nat_bias'][l])
        br_c = _rglru(z3, w['conv_w'][l], w['conv_b'][l], w['lru_wg'][l], w['lru_bg'][l], w['lam'][l])
        br_d, gates = _hgrn(z3, w['lb_logits'], w['hgrn_gain'][l], h2d.reshape(b, t, D_MODEL),
                            w['w_merge'][l], l)
        branches = [a.reshape(b * t, BRANCH_W) for a in (br_a, br_b, br_c, br_d)]
        x2d = _merge(x2d, branches, gates.reshape(b * t, N_BRANCH * D_MODEL), p[l].reshape(b * t, PLE_DIM),
                     w['w_branch'][l], w['w_out'][l], w['ple_norm'][l], w['w_ple_gate'][l],
                     w['w_ple_proj'][l], w['final_norm'], l == depth - 1)
    return x2d.reshape(b, t, D_MODEL)


def kernel(x_prompt, x_sample, p_prompt, p_sample, norm_mix, w_in, ret_decay_logit, nat_rpb, lru_conv_w,
           lru_conv_b, lru_wa, lru_ba, lru_wx, lru_bx, lru_lambda, hgrn_lb_logits, hgrn_norm, w_branch,
           w_merge, w_out, ple_norm, w_ple_gate, w_ple_proj, final_norm):
    depth = w_in.shape[0]
    t = x_prompt.shape[1]
    rows = t // GRID_W
    cos_tab, sin_tab = _rotary_tables(t)
    gate_w = [_rglru_gate_weights(lru_wa[l], lru_ba[l], lru_wx[l], lru_bx[l]) for l in range(depth)]
    weights = {
        'norm_mix': norm_mix,
        'w_in': jnp.concatenate([_pair_rotary_layout(w_in[..., OFF_RQ:OFF_RK]),
                                 _pair_rotary_layout(w_in[..., OFF_RK:OFF_RV]),
                                 w_in[..., OFF_RV:]], axis=-1).astype(BF16),
        'cos': cos_tab,
        'sin': sin_tab,
        'lgq': _pair_rotary_layout(jnp.repeat(ret_decay_logit.astype(F32), RET_QK, axis=-1)),
        'lgv': jnp.repeat(ret_decay_logit.astype(F32), RET_V, axis=-1),
        'nat_bias': [_nat_bias_tables(nat_rpb[l], rows) for l in range(depth)],
        'conv_w': lru_conv_w,
        'conv_b': lru_conv_b,
        'lru_wg': [g[0] for g in gate_w],
        'lru_bg': [g[1] for g in gate_w],
        'lam': lru_lambda,
        'lb_logits': hgrn_lb_logits,
        'hgrn_gain': hgrn_norm,
        'w_branch': w_branch.astype(BF16),
        'w_merge': w_merge.astype(BF16),
        'w_out': w_out.astype(BF16),
        'ple_norm': ple_norm,
        'w_ple_gate': w_ple_gate.astype(BF16),
        'w_ple_proj': w_ple_proj.astype(BF16),
        'final_norm': final_norm,
    }
    y_prompt = _encoder(x_prompt, p_prompt, weights)
    y_sample = _encoder(x_sample, p_sample, weights)
    return (y_prompt, y_sample)
```

```python
import functools

import numpy as np
import jax
import jax.numpy as jnp
from jax import lax
from jax.experimental import pallas as pl
from jax.experimental.pallas import tpu as pltpu

F32 = jnp.float32
BF16 = jnp.bfloat16

D_MODEL = 1024
PLE_DIM = 256
GRID_W = 64
N_BRANCH = 4
BRANCH_W = 512
RET_HEADS = 4
RET_QK = 64
RET_V = 128
ROPE_BASE = 10000.0
NAT_HEADS = 8
NAT_HD = 64
NAT_WIN_ROWS = 8
NAT_WIN_COLS = 16
LRU_BLOCKS = 8
LRU_BW = 64
LRU_CONV = 4
LRU_C = 8.0
HGRN_HEADS = 4
HGRN_DK = 128
EPS = 1e-6
LOG2E = float(np.log2(np.e))
W_IN = 7168

OFF_RQ, OFF_RK, OFF_RV, OFF_RG = 0, 256, 512, 1024
OFF_NQ, OFF_NK, OFF_NV, OFF_NG = 1536, 2048, 2560, 3072
OFF_LX, OFF_LG = 3584, 4096
OFF_HQ, OFF_HFF, OFF_HFB, OFF_HI, OFF_HG = 4608, 5120, 5632, 6144, 6656

LANES = 128
SUBLANES = 8
VMEM_LIMIT = 56 * 1024 * 1024

CHUNK = 128
HGRN_CHUNK = 128
HGRN_GATE_CHUNKS = 8
GATE_PIECE_COLS = 256
HGRN_GATE_PIECES = D_MODEL // GATE_PIECE_COLS - 1
RGLRU_GATE_SEGS = 4


def _params(*sem):
    return pltpu.CompilerParams(dimension_semantics=sem, vmem_limit_bytes=VMEM_LIMIT)


def _dot(a, b):
    return jnp.dot(a.astype(BF16), b.astype(BF16), preferred_element_type=F32)


def _dot_nt(a, b):
    return lax.dot_general(a.astype(BF16), b.astype(BF16), (((1,), (1,)), ((), ())),
                           preferred_element_type=F32)


def _dot_tn(a, b):
    return lax.dot_general(a.astype(BF16), b.astype(BF16), (((0,), (0,)), ((), ())),
                           preferred_element_type=F32)


def _rms(x, g):
    return x * lax.rsqrt(jnp.mean(x * x, axis=-1, keepdims=True) + EPS) * g


def _sigmoid(x):
    return jax.nn.sigmoid(x)


def _silu(x):
    return x * jax.nn.sigmoid(x)


def _zspec(t, width, off, grid_pos):
    base = off // width
    if grid_pos == 0:
        return pl.BlockSpec((None, t, width), lambda b, h: (b, 0, base + h))
    return pl.BlockSpec((None, t, width), lambda h, b: (b, 0, base + h))


def _inproj_kernel(x_ref, g_ref, w_ref, z_ref, h_ref):
    @pl.when(pl.program_id(1) == 0)
    def _():
        h_ref[...] = _rms(x_ref[...], g_ref[...]).astype(h_ref.dtype)

    z_ref[...] = jnp.dot(h_ref[...], w_ref[...], preferred_element_type=F32)


def _inproj(x2d, g, w_bf16):
    m = x2d.shape[0]
    tm, tn = 2048, 1024
    return pl.pallas_call(
        _inproj_kernel,
        grid=(m // tm, W_IN // tn),
        in_specs=[
            pl.BlockSpec((tm, D_MODEL), lambda i, j: (i, 0)),
            pl.BlockSpec((1, D_MODEL), lambda i, j: (0, 0)),
            pl.BlockSpec((D_MODEL, tn), lambda i, j: (0, j)),
        ],
        out_specs=[
            pl.BlockSpec((tm, tn), lambda i, j: (i, j)),
            pl.BlockSpec((tm, D_MODEL), lambda i, j: (i, 0)),
        ],
        out_shape=[
            jax.ShapeDtypeStruct((m, W_IN), F32),
            jax.ShapeDtypeStruct((m, D_MODEL), BF16),
        ],
        compiler_params=_params("parallel", "arbitrary"),
        name="inproj",
    )(x2d, g.reshape(1, D_MODEL), w_bf16)


def _merge_kernel(x_ref, ba_ref, bb_ref, bc_ref, bd_ref, gates_ref, gtail_ref, p_ref, wbr_ref, wo_ref,
                  gple_ref, wpg_ref, wpp_ref, gfin_ref, out_ref, *, final):
    x = x_ref[...]
    merged = None
    head_w = HGRN_GATE_PIECES * GATE_PIECE_COLS
    for j, b_ref in enumerate((ba_ref, bb_ref, bc_ref, bd_ref)):
        pre = jnp.concatenate([gates_ref[:, j * head_w:(j + 1) * head_w],
                               gtail_ref[:, j * GATE_PIECE_COLS:(j + 1) * GATE_PIECE_COLS]], axis=1)
        gate = _sigmoid(pre.astype(F32))
        term = gate * jnp.dot(b_ref[...], wbr_ref[j], preferred_element_type=F32)
        merged = term if merged is None else merged + term
    x1 = x + _dot(merged, wo_ref[...])
    gate2 = _sigmoid(_dot(_rms(x1, gple_ref[...]), wpg_ref[...]))
    x2 = x1 + gate2 * _dot(p_ref[...], wpp_ref[...])
    if final:
        x2 = _rms(x2, gfin_ref[...])
    out_ref[...] = x2


def _merge(x2d, branches, gates, gtail, p2d, wbr, wo, gple, wpg, wpp, gfin, final):
    m = x2d.shape[0]
    tm = 512
    row = lambda i: (i, 0)
    const2 = lambda i: (0, 0)
    const3 = lambda i: (0, 0, 0)
    once = pl.Buffered(1)
    vec = pl.BlockSpec((1, D_MODEL), const2)
    return pl.pallas_call(
        functools.partial(_merge_kernel, final=final),
        grid=(m // tm,),
        in_specs=[
            pl.BlockSpec((tm, D_MODEL), row),
            pl.BlockSpec((tm, BRANCH_W), row),
            pl.BlockSpec((tm, BRANCH_W), row),
            pl.BlockSpec((tm, BRANCH_W), row),
            pl.BlockSpec((tm, BRANCH_W), row),
            pl.BlockSpec((tm, N_BRANCH * HGRN_GATE_PIECES * GATE_PIECE_COLS), row),
            pl.BlockSpec((tm, N_BRANCH * GATE_PIECE_COLS), row),
            pl.BlockSpec((tm, PLE_DIM), row),
            pl.BlockSpec((N_BRANCH, BRANCH_W, D_MODEL), const3, pipeline_mode=once),
            pl.BlockSpec((D_MODEL, D_MODEL), const2, pipeline_mode=once),
            vec,
            pl.BlockSpec((D_MODEL, D_MODEL), const2, pipeline_mode=once),
            pl.BlockSpec((PLE_DIM, D_MODEL), const2, pipeline_mode=once),
            vec,
        ],
        out_specs=pl.BlockSpec((tm, D_MODEL), row),
        out_shape=jax.ShapeDtypeStruct((m, D_MODEL), F32),
        compiler_params=_params("parallel"),
        name="merge",
    )(x2d, *branches, gates, gtail, p2d, wbr, wo, gple.reshape(1, -1), wpg, wpp, gfin.reshape(1, -1))


def _retention_kernel(q_ref, k_ref, v_ref, g_ref, cos_ref, sin_ref, lgq_ref, lgv_ref, out_ref,
                      kr_scr, sb_scr, sf_scr):
    t = q_ref.shape[0]
    c = CHUNK
    n_chunks = t // c
    hd = RET_QK

    lane = lax.broadcasted_iota(jnp.int32, (1, 2 * hd), 1)
    head0_q = (lane // (hd // 2)) % 2 == 0
    lane_v = lax.broadcasted_iota(jnp.int32, (1, 2 * RET_V), 1)
    head0_v = lane_v < RET_V

    def rotary(x, cos, sin):
        return x * cos + pltpu.roll(x, hd, 1) * sin

    lg_f = -jnp.log(1.0 + jnp.exp(-lgq_ref[0:1, :]))
    lg_b = -jnp.log(1.0 + jnp.exp(-lgq_ref[1:2, :]))
    lgv_f = -jnp.log(1.0 + jnp.exp(-lgv_ref[0:1, :]))
    lgv_b = -jnp.log(1.0 + jnp.exp(-lgv_ref[1:2, :]))
    tcol = lax.broadcasted_iota(jnp.int32, (c, 1), 0).astype(F32)
    head_f = jnp.exp((tcol + 1.0) * lg_f)
    head_b = jnp.exp((c - tcol) * lg_b)
    tail_f = jnp.exp((c - 1.0 - tcol) * lg_f)
    tail_b = jnp.exp(tcol * lg_b)
    dec_f = jnp.exp(c * lgv_f)
    dec_b = jnp.exp(c * lgv_b)
    rowk = lax.broadcasted_iota(jnp.int32, (2 * hd, 2 * RET_V), 0)
    colv = lax.broadcasted_iota(jnp.int32, (2 * hd, 2 * RET_V), 1)
    blockdiag = ((rowk // (hd // 2)) % 2) == (colv // RET_V)

    diff = (lax.broadcasted_iota(jnp.int32, (c, c), 0) - lax.broadcasted_iota(jnp.int32, (c, c), 1)).astype(F32)

    def decay_mask(lf, lb):
        fwd = jnp.exp(jnp.maximum(diff, 0.0) * lf)
        bwd = jnp.exp(jnp.maximum(-diff, 0.0) * lb)
        return jnp.where(diff > 0, fwd, jnp.where(diff < 0, bwd, 2.0))

    dmask = jnp.concatenate([decay_mask(lgv_f[:, :c], lgv_b[:, :c]),
                             decay_mask(lgv_f[:, RET_V:RET_V + c], lgv_b[:, RET_V:RET_V + c])], axis=1)

    sf_scr[...] = jnp.zeros_like(sf_scr)

    def bwd_body(i, carry):
        n = n_chunks - 1 - i
        sl = pl.ds(pl.multiple_of(n * c, c), c)
        sb_scr[n] = sf_scr[...]
        kn = rotary(k_ref[sl, :], cos_ref[sl, :], sin_ref[sl, :]) * (hd ** -0.5)
        kr_scr[sl, :] = kn
        loc = _dot_tn(kn * tail_b, v_ref[sl, :])
        sf_scr[...] = sf_scr[...] * dec_b + jnp.where(blockdiag, loc, 0.0)
        return carry

    lax.fori_loop(0, n_chunks, bwd_body, 0, unroll=16)

    sf_scr[...] = jnp.zeros_like(sf_scr)

    def fwd_body(n, carry):
        sl = pl.ds(pl.multiple_of(n * c, c), c)
        qn = rotary(q_ref[sl, :], cos_ref[sl, :], sin_ref[sl, :])
        kn = kr_scr[sl, :]
        vn = v_ref[sl, :]
        kstack = jnp.concatenate([jnp.where(head0_q, kn, 0.0), jnp.where(head0_q, 0.0, kn)], axis=0)
        scores = _dot_nt(qn, kstack) * dmask
        vstack = jnp.concatenate([jnp.where(head0_v, vn, 0.0), jnp.where(head0_v, 0.0, vn)], axis=0)
        lhs = jnp.concatenate([scores, qn * head_f, qn * head_b], axis=1)
        rhs = jnp.concatenate([vstack, sf_scr[...], sb_scr[n]], axis=0)
        o = _dot(lhs, rhs)
        loc = _dot_tn(kn * tail_f, vn)
        sf_scr[...] = sf_scr[...] * dec_f + jnp.where(blockdiag, loc, 0.0)
        o0 = o[:, :RET_V]
        o1 = o[:, RET_V:]
        o0 = o0 * lax.rsqrt(jnp.mean(o0 * o0, axis=-1, keepdims=True) + EPS)
        o1 = o1 * lax.rsqrt(jnp.mean(o1 * o1, axis=-1, keepdims=True) + EPS)
        on = jnp.concatenate([o0, o1], axis=1)
        out_ref[sl, :] = (on * _silu(g_ref[sl, :])).astype(out_ref.dtype)
        return carry

    lax.fori_loop(0, n_chunks, fwd_body, 0, unroll=16)


def _retention(z3, cos_tab, sin_tab, lgq, lgv):
    b, t, _ = z3.shape
    n_chunks = t // CHUNK
    return pl.pallas_call(
        _retention_kernel,
        grid=(b, RET_HEADS // 2),
        in_specs=[
            _zspec(t, 2 * RET_QK, OFF_RQ, 0),
            _zspec(t, 2 * RET_QK, OFF_RK, 0),
            _zspec(t, 2 * RET_V, OFF_RV, 0),
            _zspec(t, 2 * RET_V, OFF_RG, 0),
            pl.BlockSpec((t, 2 * RET_QK), lambda i, h: (0, 0)),
            pl.BlockSpec((t, 2 * RET_QK), lambda i, h: (0, 0)),
            pl.BlockSpec((2, 2 * RET_QK), lambda i, h: (0, h)),
            pl.BlockSpec((2, 2 * RET_V), lambda i, h: (0, h)),
        ],
        out_specs=pl.BlockSpec((None, t, 2 * RET_V), lambda i, h: (i, 0, h)),
        out_shape=jax.ShapeDtypeStruct((b, t, BRANCH_W), BF16),
        scratch_shapes=[
            pltpu.VMEM((t, 2 * RET_QK), F32),
            pltpu.VMEM((n_chunks, 2 * RET_QK, 2 * RET_V), F32),
            pltpu.VMEM((2 * RET_QK, 2 * RET_V), F32),
        ],
        compiler_params=_params("parallel", "parallel"),
        name="retention",
    )(z3, z3, z3, z3, cos_tab, sin_tab, lgq, lgv)


def _rotary_tables(t):
    half = RET_QK // 2
    inv = ROPE_BASE ** (-jnp.arange(half, dtype=F32) / half)
    ang = jnp.arange(t, dtype=F32)[:, None] * inv[None, :]
    cos = jnp.cos(ang)
    sin = jnp.sin(ang)
    cos_tab = jnp.tile(cos, (1, 4))
    sin_tab = jnp.concatenate([-sin, -sin, sin, sin], axis=1)
    return cos_tab, sin_tab


def _pair_rotary_layout(a):
    lead = a.shape[:-1]
    a = a.reshape(lead + (RET_HEADS // 2, 2, 2, RET_QK // 2))
    return jnp.swapaxes(a, -3, -2).reshape(lead + (RET_HEADS * RET_QK,))


NAT_QROWS = 8
NAT_KROWS = 16
NAT_KCOLS = 2 * NAT_WIN_COLS


def _nat_kstart(j):
    return int(np.clip(j * NAT_WIN_COLS - NAT_WIN_COLS // 2, 0, GRID_W - NAT_KCOLS))


def _nat_key_row_base(g, rows):
    return int(np.clip(g * NAT_QROWS - NAT_WIN_ROWS // 2, 0, rows - NAT_KROWS))


def _edge_class(i, n):
    return 0 if i == 0 else (2 if i == n - 1 else 1)


def _nat_bias_tables(rpb, rows):
    n_g = rows // NAT_QROWS
    n_cb = GRID_W // NAT_WIN_COLS

    def row_geometry(g):
        dr = np.zeros((NAT_QROWS, NAT_KROWS), np.int32)
        ok = np.zeros((NAT_QROWS, NAT_KROWS), bool)
        kb = _nat_key_row_base(g, rows)
        for rr in range(NAT_QROWS):
            r = g * NAT_QROWS + rr
            rs = int(np.clip(r - NAT_WIN_ROWS // 2, 0, rows - NAT_WIN_ROWS))
            for kr in range(NAT_KROWS):
                ka = kb + kr
                ok[rr, kr] = rs <= ka < rs + NAT_WIN_ROWS
                dr[rr, kr] = int(np.clip(ka - r + NAT_WIN_ROWS - 1, 0, 2 * NAT_WIN_ROWS - 2))
        return dr, ok

    def col_geometry(j):
        dc = np.zeros((NAT_WIN_COLS, NAT_KCOLS), np.int32)
        ok = np.zeros((NAT_WIN_COLS, NAT_KCOLS), bool)
        ks = _nat_kstart(j)
        for qq in range(NAT_WIN_COLS):
            qc = j * NAT_WIN_COLS + qq
            ws = int(np.clip(qc - NAT_WIN_COLS // 2, 0, GRID_W - NAT_WIN_COLS))
            for kc in range(NAT_KCOLS):
                ka = ks + kc
                ok[qq, kc] = ws <= ka < ws + NAT_WIN_COLS
                dc[qq, kc] = int(np.clip(ka - qc + NAT_WIN_COLS - 1, 0, 2 * NAT_WIN_COLS - 2))
        return dc, ok

    def by_class(geometry, n):
        reps = {}
        for i in range(n):
            dx, ok = geometry(i)
            cls = _edge_class(i, n)
            if cls in reps:
                assert (np.where(ok, dx, -1) == np.where(reps[cls][1], reps[cls][0], -1)).all()
            else:
                reps[cls] = (dx, ok)
        filled = [reps.get(cls, reps[0]) for cls in range(3)]
        return np.stack([f[0] for f in filled]), np.stack([f[1] for f in filled])

    dr, row_ok = by_class(row_geometry, n_g)
    dc, col_ok = by_class(col_geometry, n_cb)
    col_sel = (dc[..., None] == np.arange(2 * NAT_WIN_COLS - 1)).astype(np.float32)
    blocks = jnp.einsum('hab,jqcb->hjaqc', rpb.astype(F32), col_sel, precision=lax.Precision.HIGHEST)
    blocks = jnp.where(col_ok[None, :, None], blocks * LOG2E, -jnp.inf)
    nq = NAT_QROWS * NAT_WIN_COLS
    nk = NAT_KROWS * NAT_KCOLS

    def tile_kernel(blocks_ref, out_ref):
        masked = jnp.full((NAT_WIN_COLS, NAT_KCOLS), -jnp.inf, F32)
        for g in range(3):
            for j in range(3):
                for rr in range(NAT_QROWS):
                    pieces = [blocks_ref[j, int(dr[g, rr, kr])] if row_ok[g, rr, kr] else masked
                              for kr in range(NAT_KROWS)]
                    out_ref[g, j, rr * NAT_WIN_COLS:(rr + 1) * NAT_WIN_COLS, :] = jnp.concatenate(pieces, axis=1)

    return pl.pallas_call(
        tile_kernel,
        grid=(NAT_HEADS,),
        in_specs=[pl.BlockSpec((None,) + blocks.shape[1:], lambda h: (h, 0, 0, 0, 0))],
        out_specs=pl.BlockSpec((None, None, 3, 3, nq, nk), lambda h: (h // 2, h % 2, 0, 0, 0, 0)),
        out_shape=jax.ShapeDtypeStruct((NAT_HEADS // 2, 2, 3, 3, nq, nk), F32),
        compiler_params=_params("parallel"),
        name="nat_bias",
    )(blocks)


def _nat_kernel(q_ref, k_ref, v_ref, g_ref, bias_ref, out_ref):
    t = q_ref.shape[0]
    rows = t // GRID_W
    n_g = rows // NAT_QROWS
    n_cb = GRID_W // NAT_WIN_COLS
    nq = NAT_QROWS * NAT_WIN_COLS
    lane = lax.broadcasted_iota(jnp.int32, (1, 2 * NAT_HD), 1)
    head0 = lane < NAT_HD

    def group_body(g, carry):
        kb = jnp.clip(g * NAT_QROWS - NAT_WIN_ROWS // 2, 0, rows - NAT_KROWS)
        g_cls = jnp.where(g == 0, 0, jnp.where(g == n_g - 1, 2, 1))
        for j in range(n_cb):
            ks = _nat_kstart(j)

            def qrows(ref, rr):
                start = pl.multiple_of((g * NAT_QROWS + rr) * GRID_W + j * NAT_WIN_COLS, NAT_WIN_COLS)
                return ref[pl.ds(start, NAT_WIN_COLS), :]

            def krows(ref, kr):
                start = pl.multiple_of((kb + kr) * GRID_W + ks, SUBLANES)
                return ref[pl.ds(start, NAT_KCOLS), :]

            qs = jnp.concatenate([qrows(q_ref, rr) for rr in range(NAT_QROWS)], axis=0) * (NAT_HD ** -0.5 * LOG2E)
            q2 = jnp.concatenate([jnp.where(head0, qs, 0.0), jnp.where(head0, 0.0, qs)], axis=0)
            kblk = jnp.concatenate([krows(k_ref, kr) for kr in range(NAT_KROWS)], axis=0)
            vblk = jnp.concatenate([krows(v_ref, kr) for kr in range(NAT_KROWS)], axis=0)
            s = _dot_nt(q2, kblk)
            j_cls = _edge_class(j, n_cb)
            s = s + jnp.concatenate([bias_ref[0, g_cls, j_cls], bias_ref[1, g_cls, j_cls]], axis=0)
            m = jnp.max(s, axis=-1, keepdims=True)
            p = jnp.exp2(s - m)
            vext = jnp.concatenate([vblk, jnp.ones_like(vblk)], axis=1)
            o_ext = _dot(p, vext)
            o2 = o_ext[:, :2 * NAT_HD] / o_ext[:, 2 * NAT_HD:]
            o = jnp.where(head0, o2[:nq], o2[nq:])
            gs = jnp.concatenate([qrows(g_ref, rr) for rr in range(NAT_QROWS)], axis=0)
            res = (o * _silu(gs)).astype(out_ref.dtype)
            for rr in range(NAT_QROWS):
                start = pl.multiple_of((g * NAT_QROWS + rr) * GRID_W + j * NAT_WIN_COLS, NAT_WIN_COLS)
                out_ref[pl.ds(start, NAT_WIN_COLS), :] = res[rr * NAT_WIN_COLS:(rr + 1) * NAT_WIN_COLS]
        return carry

    lax.fori_loop(0, n_g, group_body, 0, unroll=4)


def _nat(z3, bias_tab):
    b, t, _ = z3.shape
    n_hp = NAT_HEADS // 2
    w = 2 * NAT_HD
    return pl.pallas_call(
        _nat_kernel,
        grid=(n_hp, b),
        in_specs=[
            _zspec(t, w, OFF_NQ, 1),
            _zspec(t, w, OFF_NK, 1),
            _zspec(t, w, OFF_NV, 1),
            _zspec(t, w, OFF_NG, 1),
            pl.BlockSpec((None,) + bias_tab.shape[1:], lambda h, i: (h, 0, 0, 0, 0, 0)),
        ],
        out_specs=pl.BlockSpec((None, t, w), lambda h, i: (i, 0, h)),
        out_shape=jax.ShapeDtypeStruct((b, t, BRANCH_W), BF16),
        compiler_params=_params("parallel", "parallel"),
        name="nat",
    )(z3, z3, z3, z3, bias_tab)


def _rglru_kernel(x_ref, g_ref, cw_ref, cb_ref, wg_ref, bg_ref, lam_ref, h_ref, wmg_ref, out_ref, gate_ref,
                  xp_scr, af_scr, bf_scr, ab_scr, bb_scr, hf_scr, pf_scr, hb_scr, pb_scr):
    t = x_ref.shape[0]
    n_seg = SUBLANES
    tc = t // n_seg
    pad = SUBLANES
    w = LANES

    xp_scr[0:pad, :] = jnp.zeros((pad, w), F32)
    xp_scr[pad + t:pad + t + pad, :] = jnp.zeros((pad, w), F32)
    xp_scr[pad:pad + t, :] = x_ref[...]

    lam = lam_ref[...]
    neg = -lam
    softplus = jnp.maximum(neg, 0.0) + jnp.log(1.0 + jnp.exp(-jnp.abs(neg)))
    log2_a_per_r = softplus * float(-LRU_C * np.log2(np.e))
    cw = cw_ref[...]
    cb = cb_ref[...]
    bg = bg_ref[...]

    def gate_body(n, carry):
        t0 = pl.multiple_of(n * tc, tc)
        xx = xp_scr[pl.ds(t0, tc + 2 * pad), :]
        total = tc + 2 * pad
        xc = cb
        for j in range(LRU_CONV):
            shift = LRU_CONV // 2 - j
            xs = xx if shift == 0 else pltpu.roll(xx, shift % total, 0)
            xc = xc + xs[pad:pad + tc] * cw[j:j + 1, :]
        gates = _dot(xc, wg_ref[...]) + bg
        seg_rows = pl.ds(n, tc, stride=n_seg)
        for d, (a_scr, b_scr) in enumerate(((af_scr, bf_scr), (ab_scr, bb_scr))):
            r = _sigmoid(gates[:, (2 * d) * w:(2 * d + 1) * w])
            i = _sigmoid(gates[:, (2 * d + 1) * w:(2 * d + 2) * w])
            a = jnp.exp2(r * log2_a_per_r[d:d + 1, :])
            a_scr[seg_rows, :] = a
            gap = 1.0 - a * a
            b_scr[seg_rows, :] = gap * lax.rsqrt(jnp.maximum(gap, 1e-30)) * (i * xc)
        return carry

    segs_per_block = min(RGLRU_GATE_SEGS, n_seg)
    block_rows = segs_per_block * tc

    def gate_block(m, carry):
        rows = pl.ds(pl.multiple_of(m * block_rows, block_rows), block_rows)
        gate_ref[rows, :] = jnp.dot(h_ref[rows, :], wmg_ref[...],
                                    preferred_element_type=F32).astype(gate_ref.dtype)
        for u in range(segs_per_block):
            gate_body(m * segs_per_block + u, carry)
        return carry

    lax.fori_loop(0, n_seg // segs_per_block, gate_block, 0)

    def scan_body(i, carry):
        hf, pf, hb, pb = carry
        sl = pl.ds(pl.multiple_of(i * n_seg, n_seg), n_seg)
        a = af_scr[sl, :]
        hf = a * hf + bf_scr[sl, :]
        pf = a * pf
        hf_scr[sl, :] = hf
        pf_scr[sl, :] = pf
        slb = pl.ds(pl.multiple_of((tc - 1 - i) * n_seg, n_seg), n_seg)
        a = ab_scr[slb, :]
        hb = a * hb + bb_scr[slb, :]
        pb = a * pb
        hb_scr[slb, :] = hb
        pb_scr[slb, :] = pb
        return hf, pf, hb, pb

    zero = jnp.zeros((n_seg, w), F32)
    one = jnp.ones((n_seg, w), F32)
    hf, pf, hb, pb = lax.fori_loop(0, tc, scan_body, (zero, one, zero, one), unroll=8)

    row = lax.broadcasted_iota(jnp.int32, (n_seg, w), 0)
    for s in (1, 2, 4):
        keep = row >= s
        hf = jnp.where(keep, pf * pltpu.roll(hf, s, 0) + hf, hf)
        pf = jnp.where(keep, pf * pltpu.roll(pf, s, 0), pf)
        keep = row < n_seg - s
        hb = jnp.where(keep, pb * pltpu.roll(hb, n_seg - s, 0) + hb, hb)
        pb = jnp.where(keep, pb * pltpu.roll(pb, n_seg - s, 0), pb)
    carry_f = jnp.where(row >= 1, pltpu.roll(hf, 1, 0), 0.0)
    carry_b = jnp.where(row < n_seg - 1, pltpu.roll(hb, n_seg - 1, 0), 0.0)

    def fix_body(i, carry):
        sl = pl.ds(pl.multiple_of(i * n_seg, n_seg), n_seg)
        af_scr[sl, :] = (hf_scr[sl, :] + pf_scr[sl, :] * carry_f) + (hb_scr[sl, :] + pb_scr[sl, :] * carry_b)
        return carry

    lax.fori_loop(0, tc, fix_body, 0, unroll=8)

    def out_body(n, carry):
        h = af_scr[pl.ds(n, tc, stride=n_seg), :]
        sl = pl.ds(pl.multiple_of(n * tc, tc), tc)
        out_ref[sl, :] = (h * _silu(g_ref[sl, :])).astype(out_ref.dtype)
        return carry

    lax.fori_loop(0, n_seg, out_body, 0)


def _rglru(z3, conv_w, conv_b, wg, bg, lam, h3, wmg):
    b, t, _ = z3.shape
    n_cb = BRANCH_W // LANES
    last_piece = D_MODEL // GATE_PIECE_COLS - 1
    return pl.pallas_call(
        _rglru_kernel,
        grid=(b, n_cb),
        in_specs=[
            _zspec(t, LANES, OFF_LX, 0),
            _zspec(t, LANES, OFF_LG, 0),
            pl.BlockSpec((LRU_CONV, LANES), lambda i, h: (0, h)),
            pl.BlockSpec((1, LANES), lambda i, h: (0, h)),
            pl.BlockSpec((None, LANES, 4 * LANES), lambda i, h: (h, 0, 0)),
            pl.BlockSpec((None, 1, 4 * LANES), lambda i, h: (h, 0, 0)),
            pl.BlockSpec((2, LANES), lambda i, h: (0, h)),
            pl.BlockSpec((None, t, D_MODEL), lambda i, h: (i, 0, 0)),
            pl.BlockSpec((None, D_MODEL, GATE_PIECE_COLS), lambda i, h: (h, 0, last_piece)),
        ],
        out_specs=[
            pl.BlockSpec((None, t, LANES), lambda i, h: (i, 0, h)),
            pl.BlockSpec((None, t, GATE_PIECE_COLS), lambda i, h: (i, 0, h)),
        ],
        out_shape=[
            jax.ShapeDtypeStruct((b, t, BRANCH_W), BF16),
            jax.ShapeDtypeStruct((b, t, N_BRANCH * GATE_PIECE_COLS), BF16),
        ],
        scratch_shapes=[pltpu.VMEM((t + 2 * SUBLANES, LANES), F32)] + [pltpu.VMEM((t, LANES), F32)] * 8,
        compiler_params=_params("parallel", "parallel"),
        name="rglru",
    )(z3, z3, conv_w, conv_b.reshape(1, BRANCH_W), wg, bg, lam, h3, wmg)


def _rglru_gate_weights(wa, ba, wx, bx):
    n_cb = BRANCH_W // LANES
    per = LANES // LRU_BW

    def blockdiag(w):
        w = w.reshape(n_cb, per, LRU_BW, LRU_BW)
        eye = jnp.eye(per, dtype=w.dtype)
        return jnp.einsum('cpjk,pq->cpjqk', w, eye).reshape(n_cb, LANES, LANES)

    wg = jnp.concatenate([blockdiag(wa[0]), blockdiag(wx[0]), blockdiag(wa[1]), blockdiag(wx[1])], axis=-1)
    bg = jnp.concatenate([ba[0].reshape(n_cb, 1, LANES), bx[0].reshape(n_cb, 1, LANES),
                          ba[1].reshape(n_cb, 1, LANES), bx[1].reshape(n_cb, 1, LANES)], axis=-1)
    return wg.astype(BF16), bg.astype(F32)


def _prefix_sum_rows(x):
    c, w = x.shape
    x3 = x.reshape(c // SUBLANES, SUBLANES, w)
    row = lax.broadcasted_iota(jnp.int32, (1, SUBLANES, w), 1)
    s = 1
    while s < SUBLANES:
        x3 = x3 + jnp.where(row >= s, pltpu.roll(x3, s, 1), 0.0)
        s *= 2
    tile_tot = jnp.broadcast_to(x3[:, SUBLANES - 1:, :], x3.shape).reshape(c, w)
    x = x3.reshape(c, w)
    while s < c:
        shifted = jnp.concatenate([jnp.zeros((s, w), x.dtype), tile_tot[:c - s]], axis=0)
        x = x + shifted
        tile_tot = tile_tot + shifted
        s *= 2
    return x


def _hgrn_forget(zf, lb):
    f = lb + (1.0 - lb) * _sigmoid(zf)
    return f, jnp.log2(f)


def _hgrn_tile_decays(size, f3, fb3, bs3, cs3, pos):
    half = size // 2
    if size == 2:
        return jnp.where(pos == 1, f3, 1.0), jnp.where(pos == 0, fb3, 1.0)
    if size == 4:
        f_prev, f_next = pltpu.roll(f3, 1, 1), pltpu.roll(f3, SUBLANES - 1, 1)
        fb_prev, fb_next = pltpu.roll(fb3, 1, 1), pltpu.roll(fb3, SUBLANES - 1, 1)
        e_f = jnp.where(pos == 0, f_next, jnp.where(pos == 1, 1.0, jnp.where(pos == 2, f3, f_prev * f3)))
        e_b = jnp.where(pos == 0, fb3 * fb_next, jnp.where(pos == 1, fb3, jnp.where(pos == 2, 1.0, fb_prev)))
        return e_f, e_b
    sign = jnp.where(pos >= half, 1.0, -1.0)
    e_f = jnp.exp2((bs3 - bs3[:, half - 1:half, :]) * sign)
    e_b = jnp.exp2((cs3[:, half:half + 1, :] - cs3) * sign)
    return e_f, e_b


def _hgrn_level_operands(size, q, kf, kb, f, fb, bs, cs):
    c, w = q.shape
    half = size // 2
    if size <= SUBLANES:
        tiled = lambda a: a.reshape(c // SUBLANES, SUBLANES, w)
        pos = lax.broadcasted_iota(jnp.int32, (1, SUBLANES, w), 1) % size
        upper = pos >= half
        e_f, e_b = _hgrn_tile_decays(size, tiled(f), tiled(fb), tiled(bs), tiled(cs), pos)
        z_f = e_f * jnp.where(upper, tiled(q), tiled(kf))
        z_b = e_b * jnp.where(upper, tiled(kb), tiled(q))
        x = jnp.concatenate([jnp.where(upper, z_f, 0.0), jnp.where(upper, 0.0, z_b)], axis=2)
        y = jnp.concatenate([z_f, z_b], axis=2)
        return x.reshape(c, 2 * w), y.reshape(c, 2 * w)
    xs, ys = [], []
    zero = jnp.zeros((half, w), F32)
    for i in range(c // half):
        rows = slice(i * half, (i + 1) * half)
        if i % 2 == 0:
            ref_f = bs[(i + 1) * half - 1:(i + 1) * half]
            ref_b = cs[(i + 1) * half:(i + 1) * half + 1]
            z_f = kf[rows] * jnp.exp2(ref_f - bs[rows])
            z_b = q[rows] * jnp.exp2(cs[rows] - ref_b)
            xs.append(jnp.concatenate([zero, z_b], axis=1))
        else:
            ref_f = bs[i * half - 1:i * half]
            ref_b = cs[i * half:i * half + 1]
            z_f = q[rows] * jnp.exp2(bs[rows] - ref_f)
            z_b = kb[rows] * jnp.exp2(ref_b - cs[rows])
            xs.append(jnp.concatenate([z_f, zero], axis=1))
        ys.append(jnp.concatenate([z_f, z_b], axis=1))
    return jnp.concatenate(xs, axis=0), jnp.concatenate(ys, axis=0)


def _hgrn_kernel(q_ref, ff_ref, fb_ref, v_ref, g_ref, lbl_ref, gain_ref, h_ref, wmg_ref, out_ref, gate_ref,
                 fb_scr, cs_scr, sb_scr, st_scr, *, layer):
    t = q_ref.shape[0]
    c = HGRN_CHUNK
    n_chunks = t // c
    w = HGRN_DK

    logits = lbl_ref[...]
    mx = jnp.max(logits, axis=0)
    ex = jnp.exp(logits - mx[None])
    tot = jnp.sum(ex, axis=0)
    lb = jnp.zeros_like(tot)
    for i in range(1, layer + 1):
        lb = lb + ex[i] / tot
    lb_f = lb[0:1, :]
    lb_b = lb[1:2, :]

    pair_xor = lax.broadcasted_iota(jnp.int32, (c, c), 0) ^ lax.broadcasted_iota(jnp.int32, (c, c), 1)
    pair_level = jnp.zeros((c, c), jnp.int32)
    size = 2
    while size <= c:
        pair_level = jnp.where(pair_xor >= size // 2, size, pair_level)
        size *= 2

    st_scr[...] = jnp.zeros_like(st_scr)

    def bwd_body(i, carry):
        n = n_chunks - 1 - i
        sl = pl.ds(pl.multiple_of(n * c, c), c)
        fb, gb = _hgrn_forget(fb_ref[sl, :], lb_b)
        pre = _prefix_sum_rows(gb)
        total = pre[c - 1:c, :]
        cs = total - pre + gb
        fb_scr[sl, :] = fb
        cs_scr[sl, :] = cs
        sb_scr[n] = st_scr[...]
        st_scr[...] = st_scr[...] * jnp.exp2(total) + _dot_tn(v_ref[sl, :], (1.0 - fb) * jnp.exp2(total - cs))
        return carry

    lax.fori_loop(0, n_chunks, bwd_body, 0, unroll=16)

    st_scr[...] = jnp.zeros_like(st_scr)

    def fwd_chunk(n):
        sl = pl.ds(pl.multiple_of(n * c, c), c)
        q = _silu(q_ref[sl, :])
        v = v_ref[sl, :]
        f, gf = _hgrn_forget(ff_ref[sl, :], lb_f)
        kf = 1.0 - f
        bs = _prefix_sum_rows(gf)
        fb = fb_scr[sl, :]
        kb = 1.0 - fb
        cs = cs_scr[sl, :]

        att = None
        size = 2
        while size <= c:
            x, y = _hgrn_level_operands(size, q, kf, kb, f, fb, bs, cs)
            att = jnp.where(pair_level == size, _dot_nt(x, y), 0.0 if att is None else att)
            size *= 2

        diag = jnp.sum(q * (kf + kb), axis=-1, keepdims=True)
        o = _dot(att, v) + diag * v
        inter = jnp.concatenate([q * jnp.exp2(bs), q * jnp.exp2(cs)], axis=1)
        states = jnp.concatenate([st_scr[...], sb_scr[n]], axis=1)
        o = o + _dot_nt(inter, states)
        last = bs[c - 1:c, :]
        st_scr[...] = st_scr[...] * jnp.exp2(last) + _dot_tn(v, kf * jnp.exp2(last - bs))
        o = o * lax.rsqrt(jnp.mean(o * o, axis=-1, keepdims=True) + EPS) * gain_ref[...]
        out_ref[sl, :] = (o * _silu(g_ref[sl, :])).astype(out_ref.dtype)

    chunks_per_block = min(HGRN_GATE_CHUNKS, n_chunks)
    gate_rows = chunks_per_block * c

    col_pieces = D_MODEL // GATE_PIECE_COLS
    assert chunks_per_block % col_pieces == 0
    chunks_per_piece = chunks_per_block // col_pieces

    def block_body(m, carry):
        rows = pl.ds(pl.multiple_of(m * gate_rows, gate_rows), gate_rows)
        for u in range(chunks_per_block):
            if u % chunks_per_piece == 0 and u // chunks_per_piece < HGRN_GATE_PIECES:
                c0 = (u // chunks_per_piece) * GATE_PIECE_COLS
                gate_ref[rows, c0:c0 + GATE_PIECE_COLS] = jnp.dot(
                    h_ref[rows, :], wmg_ref[:, c0:c0 + GATE_PIECE_COLS],
                    preferred_element_type=F32).astype(gate_ref.dtype)
            fwd_chunk(m * chunks_per_block + u)
        return carry

    lax.fori_loop(0, n_chunks // chunks_per_block, block_body, 0)


def _hgrn(z3, lb_logits, gain, h3, wmg, layer):
    b, t, _ = z3.shape
    depth = lb_logits.shape[0]
    w = HGRN_DK
    n_chunks = t // HGRN_CHUNK
    return pl.pallas_call(
        functools.partial(_hgrn_kernel, layer=layer),
        grid=(b, HGRN_HEADS),
        in_specs=[
            _zspec(t, w, OFF_HQ, 0),
            _zspec(t, w, OFF_HFF, 0),
            _zspec(t, w, OFF_HFB, 0),
            _zspec(t, w, OFF_HI, 0),
            _zspec(t, w, OFF_HG, 0),
            pl.BlockSpec((depth, 2, w), lambda i, h: (0, 0, h)),
            pl.BlockSpec((1, w), lambda i, h: (0, h)),
            pl.BlockSpec((None, t, D_MODEL), lambda i, h: (i, 0, 0)),
            pl.BlockSpec((None, D_MODEL, D_MODEL), lambda i, h: (h, 0, 0)),
        ],
        out_specs=[
            pl.BlockSpec((None, t, w), lambda i, h: (i, 0, h)),
            pl.BlockSpec((None, t, HGRN_GATE_PIECES * GATE_PIECE_COLS), lambda i, h: (i, 0, h)),
        ],
        out_shape=[
            jax.ShapeDtypeStruct((b, t, BRANCH_W), BF16),
            jax.ShapeDtypeStruct((b, t, N_BRANCH * HGRN_GATE_PIECES * GATE_PIECE_COLS), BF16),
        ],
        scratch_shapes=[
            pltpu.VMEM((t, w), F32),
            pltpu.VMEM((t, w), F32),
            pltpu.VMEM((n_chunks, w, w), F32),
            pltpu.VMEM((w, w), F32),
        ],
        compiler_params=_params("parallel", "parallel"),
        name="hgrn",
    )(z3, z3, z3, z3, z3, lb_logits, gain.reshape(1, -1), h3, wmg)


def _encoder(x, p, w):
    b, t, _ = x.shape
    depth = w['w_in'].shape[0]
    x2d = x.reshape(b * t, D_MODEL)
    for l in range(depth):
        z2d, h2d = _inproj(x2d, w['norm_mix'][l], w['w_in'][l])
        z3 = z2d.reshape(b, t, W_IN)
        br_a = _retention(z3, w['cos'], w['sin'], w['lgq'][l], w['lgv'][l])
        br_b = _nat(z3, w['nat_bias'][l])
        h3 = h2d.reshape(b, t, D_MODEL)
        br_c, gtail = _rglru(z3, w['conv_w'][l], w['conv_b'][l], w['lru_wg'][l], w['lru_bg'][l], w['lam'][l],
                             h3, w['w_merge'][l])
        br_d, gates = _hgrn(z3, w['lb_logits'], w['hgrn_gain'][l], h3, w['w_merge'][l], l)
        branches = [a.reshape(b * t, BRANCH_W) for a in (br_a, br_b, br_c, br_d)]
        x2d = _merge(x2d, branches, gates.reshape(b * t, -1), gtail.reshape(b * t, -1), p[l].reshape(b * t, PLE_DIM),
                     w['w_branch'][l], w['w_out'][l], w['ple_norm'][l], w['w_ple_gate'][l],
                     w['w_ple_proj'][l], w['final_norm'], l == depth - 1)
    return x2d.reshape(b, t, D_MODEL)


def kernel(x_prompt, x_sample, p_prompt, p_sample, norm_mix, w_in, ret_decay_logit, nat_rpb, lru_conv_w,
           lru_conv_b, lru_wa, lru_ba, lru_wx, lru_bx, lru_lambda, hgrn_lb_logits, hgrn_norm, w_branch,
           w_merge, w_out, ple_norm, w_ple_gate, w_ple_proj, final_norm):
    depth = w_in.shape[0]
    t = x_prompt.shape[1]
    rows = t // GRID_W
    cos_tab, sin_tab = _rotary_tables(t)
    gate_w = [_rglru_gate_weights(lru_wa[l], lru_ba[l], lru_wx[l], lru_bx[l]) for l in range(depth)]
    weights = {
        'norm_mix': norm_mix,
        'w_in': jnp.concatenate([_pair_rotary_layout(w_in[..., OFF_RQ:OFF_RK]),
                                 _pair_rotary_layout(w_in[..., OFF_RK:OFF_RV]),
                                 w_in[..., OFF_RV:]], axis=-1).astype(BF16),
        'cos': cos_tab,
        'sin': sin_tab,
        'lgq': _pair_rotary_layout(jnp.repeat(ret_decay_logit.astype(F32), RET_QK, axis=-1)),
        'lgv': jnp.repeat(ret_decay_logit.astype(F32), RET_V, axis=-1),
        'nat_bias': [_nat_bias_tables(nat_rpb[l], rows) for l in range(depth)],
        'conv_w': lru_conv_w,
        'conv_b': lru_conv_b,
        'lru_wg': [g[0] for g in gate_w],
        'lru_bg': [g[1] for g in gate_w],
        'lam': lru_lambda,
        'lb_logits': hgrn_lb_logits,
        'hgrn_gain': hgrn_norm,
        'w_branch': w_branch.astype(BF16),
        'w_merge': w_merge.astype(BF16),
        'w_out': w_out.astype(BF16),
        'ple_norm': ple_norm,
        'w_ple_gate': w_ple_gate.astype(BF16),
        'w_ple_proj': w_ple_proj.astype(BF16),
        'final_norm': final_norm,
    }
    y_prompt = _encoder(x_prompt, p_prompt, weights)
    y_sample = _encoder(x_sample, p_sample, weights)
    return (y_prompt, y_sample)
```
